```python
import jax, jax.numpy as jnp
from jax import lax
import numpy as np

D_MODEL = 1024
BATCH = 8
SEQ = 4096
DEPTH = 1

CHUNK = 64
D_MIX = D_MODEL
SB_WIDTH = D_MIX // 2
SB_HEADS = 8
SB_HEAD_DIM = SB_WIDTH // SB_HEADS
SB_BLOCK = 128
HG_WIDTH = D_MIX - SB_WIDTH
HG_EXPAND = 128
HG_HEADS = HG_WIDTH // HG_EXPAND
PLE_DIM = 256
N_GROUPS = 4
EXPERTS_PER_GROUP = 8
N_EXPERTS = N_GROUPS * EXPERTS_PER_GROUP
TOP_K = 2
D_EXPERT = D_MODEL // 2
EPS = 1e-6
IN_COLS = 3 * SB_WIDTH + 4 * HG_WIDTH

kernel_name = "hymba_sb_hgrn2_hmoe_ple"


def rmsnorm(x, w):
    xf = x.astype(jnp.float32)
    y = xf * lax.rsqrt(jnp.mean(xf * xf, axis=-1, keepdims=True) + EPS)
    return (y * w.astype(jnp.float32)).astype(x.dtype)


def head_rmsnorm(x, w):
    h, dh = x.shape[-2], x.shape[-1]
    y = x * lax.rsqrt(jnp.mean(x * x, axis=-1, keepdims=True) + EPS)
    return y * w.astype(jnp.float32).reshape(h, dh)


def stick_breaking_attention(q, k, v):
    s_len, dh = q.shape[2], q.shape[3]
    scale = dh ** -0.5
    outs = []
    for blk in range(s_len // SB_BLOCK):
        q0 = blk * SB_BLOCK
        q1 = q0 + SB_BLOCK
        qb = q[:, :, q0:q1]
        kb = k[:, :, :q1]
        vb = v[:, :, :q1]
        z = jnp.einsum('bhtd,bhsd->bhts', qb, kb) * scale
        t_pos = q0 + jnp.arange(SB_BLOCK)[:, None]
        s_pos = jnp.arange(q1)[None, :]
        before = s_pos < t_pos
        log_keep = jnp.where(before, jax.nn.log_sigmoid(-z), 0.0)
        between = lax.cumsum(log_keep, axis=3, reverse=True) - log_keep
        log_a = jax.nn.log_sigmoid(z) + between
        a = jnp.where(before, jnp.exp(log_a), 0.0)
        outs.append(jnp.einsum('bhts,bhsd->bhtd', a, vb))
    return jnp.concatenate(outs, axis=2)


def hgrn2_chunkwise(q, k, v, log_f):
    b, h, s_len, dk = q.shape
    dv = v.shape[-1]
    n = s_len // CHUNK

    def to_chunks(t):
        return jnp.moveaxis(t.reshape(b, h, n, CHUNK, t.shape[-1]), 2, 0)

    causal = jnp.tril(jnp.ones((CHUNK, CHUNK), dtype=bool))

    def step(state, inp):
        qc, kc, vc, gc = inp
        bc = jnp.cumsum(gc, axis=2)
        diff = bc[:, :, :, None, :] - bc[:, :, None, :, :]
        decay = jnp.exp(jnp.where(causal[:, :, None], diff, -jnp.inf))
        scores = jnp.einsum('bhtk,bhsk,bhtsk->bhts', qc, kc, decay)
        o = (jnp.einsum('bhts,bhsv->bhtv', scores, vc)
             + jnp.einsum('bhtk,bhkv->bhtv', qc * jnp.exp(bc), state))
        b_last = bc[:, :, -1:, :]
        new_state = (jnp.exp(b_last[:, :, 0, :, None]) * state
                     + jnp.einsum('bhsk,bhsv->bhkv', kc * jnp.exp(b_last - bc), vc))
        return new_state, o

    s0 = jnp.zeros((b, h, dk, dv), q.dtype)
    _, o = lax.scan(step, s0, (to_chunks(q), to_chunks(k), to_chunks(v), to_chunks(log_f)))
    return jnp.moveaxis(o, 0, 2).reshape(b, h, s_len, dv)


def hierarchical_moe(x, w_group_router, b_group_router, w_expert_router, b_expert_router,
                     w_gate, w_up, w_down):
    bsz, s_len, d = x.shape
    xt = x.reshape(-1, d)
    n = xt.shape[0]
    xf = xt.astype(jnp.float32)
    group_logits = xf @ w_group_router.astype(jnp.float32) + b_group_router.astype(jnp.float32)
    group_probs = jax.nn.softmax(group_logits, axis=-1)
    g_idx = jnp.argmax(group_logits, axis=-1).astype(jnp.int32)
    g_prob = jnp.take_along_axis(group_probs, g_idx[:, None], axis=1)
    expert_logits = (xf @ w_expert_router.astype(jnp.float32)
                     + b_expert_router.astype(jnp.float32)).reshape(n, N_GROUPS, EXPERTS_PER_GROUP)
    in_group = jnp.take_along_axis(expert_logits, g_idx[:, None, None], axis=1)[:, 0]
    top_vals, top_idx = lax.top_k(in_group, TOP_K)
    gates = jax.nn.softmax(top_vals, axis=-1) * g_prob
    expert_ids = (g_idx[:, None] * EXPERTS_PER_GROUP + top_idx.astype(jnp.int32)).reshape(-1)
    order = jnp.argsort(expert_ids)
    token_ids = order // TOP_K
    group_sizes = jnp.bincount(expert_ids, length=N_EXPERTS).astype(jnp.int32)
    xs = xt[token_ids]
    hg = lax.ragged_dot(xs, w_gate.astype(xs.dtype), group_sizes)
    hu = lax.ragged_dot(xs, w_up.astype(xs.dtype), group_sizes)
    y = lax.ragged_dot(jax.nn.silu(hg) * hu, w_down.astype(xs.dtype), group_sizes)
    y = y * gates.reshape(-1)[order][:, None].astype(y.dtype)
    out = jnp.zeros_like(xt).at[token_ids].add(y.astype(xt.dtype))
    return out.reshape(bsz, s_len, d)


def setup_inputs(seed: int = 0) -> dict:
    key = jax.random.key(seed)
    ks = jax.random.split(key, 20)
    f32 = jnp.float32

    def nrm(k, shape, scale):
        return jax.random.normal(k, shape, f32) * scale

    def gain(k, shape):
        return 1.0 + 0.01 * jax.random.normal(k, shape, f32)

    return {
        "x": nrm(ks[0], (BATCH, SEQ, D_MODEL), 1.0),
        "p": nrm(ks[1], (DEPTH, BATCH, SEQ, PLE_DIM), 1.0),
        "attn_norm_w": gain(ks[2], (DEPTH, D_MODEL)),
        "w_in": nrm(ks[3], (DEPTH, D_MODEL, IN_COLS), D_MODEL ** -0.5),
        "sb_norm_w": gain(ks[4], (DEPTH, SB_WIDTH)),
        "hg_lower_bounds": nrm(ks[5], (DEPTH + 1, HG_WIDTH), 0.5),
        "hg_norm_w": gain(ks[6], (DEPTH, HG_WIDTH)),
        "w_out": nrm(ks[7], (DEPTH, D_MIX, D_MODEL), D_MIX ** -0.5),
        "ffn_norm_w": gain(ks[8], (DEPTH, D_MODEL)),
        "w_group_router": nrm(ks[9], (DEPTH, D_MODEL, N_GROUPS), D_MODEL ** -0.5),
        "b_group_router": nrm(ks[10], (DEPTH, N_GROUPS), 0.01),
        "w_expert_router": nrm(ks[11], (DEPTH, D_MODEL, N_EXPERTS), D_MODEL ** -0.5),
        "b_expert_router": nrm(ks[12], (DEPTH, N_EXPERTS), 0.01),
        "w_exp_gate": nrm(ks[13], (DEPTH, N_EXPERTS, D_MODEL, D_EXPERT), D_MODEL ** -0.5),
        "w_exp_up": nrm(ks[14], (DEPTH, N_EXPERTS, D_MODEL, D_EXPERT), D_MODEL ** -0.5),
        "w_exp_down": nrm(ks[15], (DEPTH, N_EXPERTS, D_EXPERT, D_MODEL), D_EXPERT ** -0.5),
        "ple_norm_w": gain(ks[16], (DEPTH, D_MODEL)),
        "w_ple_proj": nrm(ks[17], (DEPTH, PLE_DIM, D_MODEL), PLE_DIM ** -0.5),
        "w_ple_gate": nrm(ks[18], (DEPTH, D_MODEL, D_MODEL), D_MODEL ** -0.5),
        "final_norm_w": gain(ks[19], (D_MODEL,)),
    }


def reference(x, p, attn_norm_w, w_in, sb_norm_w, hg_lower_bounds, hg_norm_w, w_out,
              ffn_norm_w, w_group_router, b_group_router, w_expert_router, b_expert_router,
              w_exp_gate, w_exp_up, w_exp_down, ple_norm_w, w_ple_proj, w_ple_gate,
              final_norm_w):
    f32 = jnp.float32
    bsz, s_len, _ = x.shape
    lower_bounds = jnp.cumsum(jax.nn.softmax(hg_lower_bounds.astype(f32), axis=0), axis=0)
    h = x
    for i in range(DEPTH):
        a = rmsnorm(h, attn_norm_w[i])
        proj = (a @ w_in[i]).astype(f32)
        sb_q, sb_k, sb_v, hg_q, hg_f, hg_i, hg_g = jnp.split(
            proj, np.cumsum([SB_WIDTH] * 3 + [HG_WIDTH] * 3).tolist(), axis=-1)

        def sb_heads(t):
            return t.reshape(bsz, s_len, SB_HEADS, SB_HEAD_DIM).transpose(0, 2, 1, 3)

        sb_o = stick_breaking_attention(sb_heads(sb_q), sb_heads(sb_k), sb_heads(sb_v))
        sb_o = head_rmsnorm(sb_o.transpose(0, 2, 1, 3), sb_norm_w[i])
        sb_o = sb_o.reshape(bsz, s_len, SB_WIDTH)

        lb = lower_bounds[i]
        f = lb + (1.0 - lb) * jax.nn.sigmoid(hg_f)
        log_f = jnp.log(f)
        hk = 1.0 - f

        def hg_heads(t):
            return t.reshape(bsz, s_len, HG_HEADS, HG_EXPAND).transpose(0, 2, 1, 3)

        hg_o = hgrn2_chunkwise(hg_heads(jax.nn.silu(hg_q)), hg_heads(hk),
                               hg_heads(hg_i), hg_heads(log_f))
        hg_o = head_rmsnorm(hg_o.transpose(0, 2, 1, 3), hg_norm_w[i]).reshape(bsz, s_len, HG_WIDTH)
        hg_o = hg_o * jax.nn.silu(hg_g)

        mix = jnp.concatenate([sb_o, hg_o], axis=-1).astype(h.dtype)
        h = h + (mix @ w_out[i]).astype(h.dtype)

        m = rmsnorm(h, ffn_norm_w[i])
        h = h + hierarchical_moe(m, w_group_router[i], b_group_router[i], w_expert_router[i],
                                 b_expert_router[i], w_exp_gate[i], w_exp_up[i],
                                 w_exp_down[i]).astype(h.dtype)

        e = (p[i].astype(h.dtype) @ w_ple_proj[i]).astype(f32)
        gate = jax.nn.sigmoid((rmsnorm(h, ple_norm_w[i]) @ w_ple_gate[i]).astype(f32))
        h = h + (gate * e).astype(h.dtype)
    return rmsnorm(h, final_norm_w)
```

```python
import functools

import numpy as np
import jax
import jax.numpy as jnp
from jax import lax
from jax.experimental import pallas as pl
from jax.experimental.pallas import tpu as pltpu

F32 = jnp.float32
BF16 = jnp.bfloat16
EPS = 1e-6

SB_HEADS = 8
SB_HEAD_DIM = 64
HG_HEAD_DIM = 128
HG_CHUNK = 64
N_GROUPS = 4
EXPERTS_PER_GROUP = 8
N_EXPERTS = N_GROUPS * EXPERTS_PER_GROUP
TOP_K = 2
LANES = 128
ROUTER_ROW0 = 8
SB_BLOCK = 128
VMEM_LIMIT = 56 * 1024 * 1024

_NT = (((1,), (1,)), ((), ()))
_TN = (((0,), (0,)), ((), ()))


def _sigmoid(x):
    return 1.0 / (1.0 + jnp.exp(-x))


def _split2(x):
    hi = x.astype(BF16)
    lo = (x - hi.astype(F32)).astype(BF16)
    return hi, lo


def _split3(x):
    hi = x.astype(BF16)
    r = x - hi.astype(F32)
    mid = r.astype(BF16)
    lo = (r - mid.astype(F32)).astype(BF16)
    return hi, mid, lo


def _in_proj_kernel(x_ref, nw_ref, w_ref, wkt_ref, lbp_ref,
                    sbq_ref, sbkt_ref, sbv_ref, hq_ref, hgl_ref, hk_ref, hv_ref, hgate_ref,
                    *, sbw, hgw):
    x = x_ref[0]
    ms = jnp.mean(x * x, axis=-1, keepdims=True)
    a = (x * lax.rsqrt(ms + EPS) * nw_ref[...]).astype(BF16)

    def seg(lo, width):
        return jnp.dot(a, w_ref[:, lo:lo + width], preferred_element_type=F32)

    sbq_ref[0] = (seg(0, sbw) * (SB_HEAD_DIM ** -0.5)).astype(BF16)
    sbkt_ref[0] = lax.dot_general(wkt_ref[...], a, _NT, preferred_element_type=F32).astype(BF16)
    sbv_ref[0] = seg(sbw, sbw).astype(BF16)
    c = 2 * sbw
    q = seg(c, hgw)
    hq_ref[0] = q * _sigmoid(q)
    p0 = lbp_ref[0:1, :]
    p1 = lbp_ref[1:2, :]
    pm = jnp.maximum(p0, p1)
    e0 = jnp.exp(p0 - pm)
    e1 = jnp.exp(p1 - pm)
    lb = e0 / (e0 + e1)
    fz = seg(c + hgw, hgw)
    sg = _sigmoid(fz)
    f = lb + (1.0 - lb) * sg
    hgl_ref[0] = jnp.log(f)
    hk_ref[0] = 1.0 - f
    hv_ref[0] = seg(c + 2 * hgw, hgw).astype(BF16)
    g = seg(c + 3 * hgw, hgw)
    hgate_ref[0] = g * _sigmoid(g)


def _in_proj(x, nw, w_main, w_kt, lbp, *, tm):
    b, s, d = x.shape
    sbw = w_kt.shape[0]
    hgw = (w_main.shape[1] - 2 * sbw) // 4
    tok = lambda width: pl.BlockSpec((1, tm, width), lambda bi, i: (bi, i, 0))
    const = lambda shape: pl.BlockSpec(shape, lambda bi, i: (0,) * len(shape))
    out_shape = [
        jax.ShapeDtypeStruct((b, s, sbw), BF16),
        jax.ShapeDtypeStruct((b, sbw, s), BF16),
        jax.ShapeDtypeStruct((b, s, sbw), BF16),
        jax.ShapeDtypeStruct((b, s, hgw), F32),
        jax.ShapeDtypeStruct((b, s, hgw), F32),
        jax.ShapeDtypeStruct((b, s, hgw), F32),
        jax.ShapeDtypeStruct((b, s, hgw), BF16),
        jax.ShapeDtypeStruct((b, s, hgw), F32),
    ]
    out_specs = [tok(sbw), pl.BlockSpec((1, sbw, tm), lambda bi, i: (bi, 0, i)), tok(sbw),
                 tok(hgw), tok(hgw), tok(hgw), tok(hgw), tok(hgw)]
    return pl.pallas_call(
        functools.partial(_in_proj_kernel, sbw=sbw, hgw=hgw),
        grid=(b, s // tm),
        in_specs=[tok(d), const((1, d)), const(w_main.shape), const(w_kt.shape), const(lbp.shape)],
        out_specs=out_specs,
        out_shape=out_shape,
        compiler_params=pltpu.CompilerParams(
            dimension_semantics=("arbitrary", "arbitrary"), vmem_limit_bytes=VMEM_LIMIT),
        name="in_proj",
    )(x, nw, w_main, w_kt, lbp)


def _sb_cumsum_weights():
    j = np.arange(SB_BLOCK)[:, None]
    s = np.arange(SB_BLOCK)[None, :]
    w = np.concatenate([(j > s).astype(np.float32), np.ones((SB_BLOCK, SB_BLOCK), np.float32)], axis=1)
    return jnp.asarray(w, dtype=BF16)


def _sb_attn_kernel(q_ref, kt_ref, v_ref, nw_ref, cw_ref, o_ref, *, n_blk):
    blk = SB_BLOCK
    dh = SB_HEAD_DIM
    lane = lax.broadcasted_iota(jnp.int32, (blk, LANES), 1)
    row = lax.broadcasted_iota(jnp.int32, (blk, blk), 0)
    col = lax.broadcasted_iota(jnp.int32, (blk, blk), 1)
    before = col < row
    cw = cw_ref[...]

    def pair(qh, j, carry, acc, diag):
        k0 = pl.multiple_of(j * blk, blk)
        kt = kt_ref[0, :, pl.ds(k0, blk)]
        v = v_ref[0, pl.ds(k0, blk), :]
        z = jnp.dot(qh, kt, preferred_element_type=F32)
        sp = jnp.log(1.0 + jnp.exp(-jnp.abs(z)))
        ls = jnp.minimum(z, 0.0) - sp
        lk = -jnp.maximum(z, 0.0) - sp
        if diag:
            lk = jnp.where(before, lk, 0.0)
        hi, lo = _split2(lk)
        cs = (jnp.dot(hi, cw, preferred_element_type=F32)
              + jnp.dot(lo, cw, preferred_element_type=F32))
        a = jnp.exp(ls + cs[:, :blk] + carry)
        if diag:
            a = jnp.where(before, a, 0.0)
        acc = acc + jnp.dot(a.astype(BF16), v, preferred_element_type=F32)
        carry = carry + cs[:, blk:]
        return carry, acc

    def q_block(qi, _):
        q0 = pl.multiple_of(qi * blk, blk)
        q2 = q_ref[0, pl.ds(q0, blk), :].astype(F32)
        accs = []
        for h in range(2):
            in_head = (lane >= h * dh) & (lane < (h + 1) * dh)
            qh = jnp.where(in_head, q2, 0.0).astype(BF16)
            zero = jnp.zeros((blk, blk), F32)
            carry, acc = pair(qh, qi, zero, zero, True)

            def body(i, ca, qh=qh):
                return pair(qh, qi - 1 - i, ca[0], ca[1], False)

            carry, acc = lax.fori_loop(0, qi, body, (carry, acc))
            accs.append(acc)
        first = lane < dh
        o = jnp.where(first, accs[0], accs[1])
        sq = o * o
        s0 = jnp.sum(jnp.where(first, sq, 0.0), axis=-1, keepdims=True)
        s1 = jnp.sum(jnp.where(first, 0.0, sq), axis=-1, keepdims=True)
        ms = jnp.where(first, s0, s1) * (1.0 / dh)
        o_ref[0, pl.ds(q0, blk), :] = (o * lax.rsqrt(ms + EPS) * nw_ref[...]).astype(o_ref.dtype)
        return 0

    lax.fori_loop(0, n_blk, q_block, 0)


def _sb_attn(q, kt, v, nw):
    b, s, w = q.shape
    n_pairs = w // LANES
    cw = _sb_cumsum_weights()
    return pl.pallas_call(
        functools.partial(_sb_attn_kernel, n_blk=s // SB_BLOCK),
        grid=(b, n_pairs),
        in_specs=[
            pl.BlockSpec((1, s, LANES), lambda bi, hp: (bi, 0, hp)),
            pl.BlockSpec((1, LANES, s), lambda bi, hp: (bi, hp, 0)),
            pl.BlockSpec((1, s, LANES), lambda bi, hp: (bi, 0, hp)),
            pl.BlockSpec((1, LANES), lambda bi, hp: (0, hp)),
            pl.BlockSpec(cw.shape, lambda bi, hp: (0, 0)),
        ],
        out_specs=pl.BlockSpec((1, s, LANES), lambda bi, hp: (bi, 0, hp)),
        out_shape=jax.ShapeDtypeStruct((b, s, w), BF16),
        compiler_params=pltpu.CompilerParams(
            dimension_semantics=("arbitrary", "arbitrary"), vmem_limit_bytes=VMEM_LIMIT),
        name="sb_attn",
    )(q, kt, v, nw, cw)


def _hgrn_levels():
    hs = []
    h = HG_CHUNK // 2
    while h >= 1:
        hs.append(h)
        h //= 2
    return hs


def _hgrn_consts():
    n = HG_CHUNK
    t = np.arange(n)[:, None]
    j = np.arange(n)[None, :]
    mats = [j <= t, j > t]
    masks = []
    for h in _hgrn_levels():
        mid = (t // (2 * h)) * (2 * h) + h - 1
        upper = (t % (2 * h)) >= h
        mats.append(upper & (j > mid) & (j <= t))
        mats.append((~upper) & (j > t) & (j <= mid))
        masks.append((t // (2 * h)) == (j // (2 * h)))
    masks.append(t == j)
    sums = np.concatenate(mats, axis=0).astype(np.float32)
    pm = np.concatenate(masks, axis=0).astype(np.float32)
    return jnp.asarray(sums, dtype=BF16), jnp.asarray(pm, dtype=F32)


def _hgrn2_kernel(q_ref, gl_ref, k_ref, v_ref, gate_ref, nw_ref, cs_ref, pm_ref, o_ref, *, n_chunks):
    n = HG_CHUNK
    dk = HG_HEAD_DIM
    levels = _hgrn_levels()
    rows = lax.broadcasted_iota(jnp.int32, (n, dk), 0)

    def chunk(c, state_t):
        r0 = pl.multiple_of(c * n, n)
        sl = pl.ds(r0, n)
        q = q_ref[0, sl, :]
        g = gl_ref[0, sl, :]
        k = k_ref[0, sl, :]
        v = v_ref[0, sl, :]
        g_hi, g_mid, g_lo = _split3(g)
        cs = cs_ref[...]
        d = (jnp.dot(cs, g_hi, preferred_element_type=F32)
             + jnp.dot(cs, g_mid, preferred_element_type=F32)
             + jnp.dot(cs, g_lo, preferred_element_type=F32))
        bc = d[0:n]
        rem = d[n:2 * n]
        b_last = bc[n - 1:n, :]
        p = pm_ref[len(levels) * n:(len(levels) + 1) * n, :] * lax.dot_general(
            q.astype(BF16), k.astype(BF16), _NT, preferred_element_type=F32)
        for li, h in enumerate(levels):
            base = (2 + 2 * li) * n
            upper = (rows & h) != 0
            eq = jnp.where(upper, jnp.exp(d[base:base + n]), 0.0)
            ek = jnp.where(upper, 0.0, jnp.exp(d[base + n:base + 2 * n]))
            s_l = lax.dot_general((q * eq).astype(BF16), (k * ek).astype(BF16), _NT,
                                  preferred_element_type=F32)
            p = p + pm_ref[li * n:(li + 1) * n, :] * s_l
        st_bf = state_t.astype(BF16)
        o = (jnp.dot(p.astype(BF16), v, preferred_element_type=F32)
             + lax.dot_general((q * jnp.exp(bc)).astype(BF16), st_bf, _NT, preferred_element_type=F32))
        kd = (k * jnp.exp(rem)).astype(BF16)
        new_state_t = state_t * jnp.exp(b_last) + lax.dot_general(v, kd, _TN, preferred_element_type=F32)
        ms = jnp.mean(o * o, axis=-1, keepdims=True)
        o_ref[0, sl, :] = (o * lax.rsqrt(ms + EPS) * nw_ref[...] * gate_ref[0, sl, :]).astype(o_ref.dtype)
        return new_state_t

    lax.fori_loop(0, n_chunks, chunk, jnp.zeros((dk, dk), F32))


def _hgrn2(q, gl, k, v, gate, nw):
    b, s, w = q.shape
    heads = w // HG_HEAD_DIM
    cs, pm = _hgrn_consts()
    tok = pl.BlockSpec((1, s, HG_HEAD_DIM), lambda bi, h: (bi, 0, h))
    return pl.pallas_call(
        functools.partial(_hgrn2_kernel, n_chunks=s // HG_CHUNK),
        grid=(b, heads),
        in_specs=[tok, tok, tok, tok, tok,
                  pl.BlockSpec((1, HG_HEAD_DIM), lambda bi, h: (0, h)),
                  pl.BlockSpec(cs.shape, lambda bi, h: (0, 0)),
                  pl.BlockSpec(pm.shape, lambda bi, h: (0, 0))],
        out_specs=tok,
        out_shape=jax.ShapeDtypeStruct((b, s, w), BF16),
        compiler_params=pltpu.CompilerParams(
            dimension_semantics=("arbitrary", "arbitrary"), vmem_limit_bytes=VMEM_LIMIT),
        name="hgrn2",
    )(q, gl, k, v, gate, nw, cs, pm)


def _out_route_kernel(x_ref, sbo_ref, hgo_ref, wo_ref, nw_ref, wr_ref, br_ref, su_ref,
                      h1_ref, m_ref, meta_ref, gates_ref, counts_ref, carry_ref, *, sbw):
    i = pl.program_id(0)

    @pl.when(i == 0)
    def _():
        carry_ref[...] = jnp.zeros_like(carry_ref)

    h1 = (x_ref[...]
          + jnp.dot(sbo_ref[...], wo_ref[0:sbw, :], preferred_element_type=F32)
          + jnp.dot(hgo_ref[...], wo_ref[sbw:, :], preferred_element_type=F32))
    h1_ref[...] = h1
    ms = jnp.mean(h1 * h1, axis=-1, keepdims=True)
    m = h1 * lax.rsqrt(ms + EPS) * nw_ref[...]
    m_ref[...] = m

    m3 = _split3(m)
    w3 = _split3(wr_ref[...])
    logits = br_ref[...]
    for wi, mi in ((0, 0), (0, 1), (1, 0), (0, 2), (1, 1), (2, 0)):
        logits = logits + lax.dot_general(w3[wi], m3[mi], _NT, preferred_element_type=F32)
    rid = lax.broadcasted_iota(jnp.int32, logits.shape, 0)
    neg = -jnp.inf
    big = jnp.int32(2 * LANES)

    def first_argmax(vals):
        vmax = jnp.max(vals, axis=0, keepdims=True)
        idx = jnp.min(jnp.where(vals == vmax, rid, big), axis=0, keepdims=True)
        return vmax, idx

    is_group = rid < N_GROUPS
    gmax, g_idx = first_argmax(jnp.where(is_group, logits, neg))
    gsum = jnp.sum(jnp.where(is_group, jnp.exp(logits - gmax), 0.0), axis=0, keepdims=True)
    g_prob = 1.0 / gsum
    lo_row = ROUTER_ROW0 + EXPERTS_PER_GROUP * g_idx
    el = jnp.where((rid >= lo_row) & (rid < lo_row + EXPERTS_PER_GROUP), logits, neg)
    v1, i1 = first_argmax(el)
    v2, i2 = first_argmax(jnp.where(rid == i1, neg, el))
    dd = jnp.exp(v2 - v1)
    p1 = 1.0 / (1.0 + dd)
    g1 = p1 * g_prob
    g2 = dd * p1 * g_prob

    hit1 = rid == i1
    hit2 = rid == i2
    onehot = jnp.where(hit1 | hit2, 1.0, 0.0)
    before_cnt = carry_ref[...] + jnp.dot(onehot.astype(BF16), su_ref[...], preferred_element_type=F32)
    r1 = jnp.sum(jnp.where(hit1, before_cnt, 0.0), axis=0, keepdims=True)
    r2 = jnp.sum(jnp.where(hit2, before_cnt, 0.0), axis=0, keepdims=True)
    carry_ref[...] = carry_ref[...] + jnp.sum(onehot, axis=1, keepdims=True)
    counts_ref[...] = carry_ref[...]

    meta_ref[...] = jnp.zeros_like(meta_ref)
    meta_ref[0:1, :] = i1 - ROUTER_ROW0
    meta_ref[1:2, :] = i2 - ROUTER_ROW0
    meta_ref[2:3, :] = r1.astype(jnp.int32)
    meta_ref[3:4, :] = r2.astype(jnp.int32)
    gt = jnp.where(rid == 0, g1, jnp.where(rid == 1, g2, 0.0))
    gates_ref[...] = gt.T


def _out_route(x2, sbo, hgo, w_out, nw, wr_t, br, *, tm):
    n, d = x2.shape
    sbw = sbo.shape[1]
    j = np.arange(tm)[:, None]
    t = np.arange(tm)[None, :]
    su = jnp.asarray((j < t).astype(np.float32), dtype=BF16)
    tok = lambda width: pl.BlockSpec((tm, width), lambda i: (i, 0))
    const = lambda shape: pl.BlockSpec(shape, lambda i: (0,) * len(shape))
    return pl.pallas_call(
        functools.partial(_out_route_kernel, sbw=sbw),
        grid=(n // tm,),
        in_specs=[tok(d), tok(sbw), tok(hgo.shape[1]), const(w_out.shape), const((1, d)),
                  const(wr_t.shape), const(br.shape), const(su.shape)],
        out_specs=[tok(d), tok(d), pl.BlockSpec((8, tm), lambda i: (0, i)), tok(LANES),
                   const((LANES, 1))],
        out_shape=[jax.ShapeDtypeStruct((n, d), F32), jax.ShapeDtypeStruct((n, d), F32),
                   jax.ShapeDtypeStruct((8, n), jnp.int32), jax.ShapeDtypeStruct((n, LANES), F32),
                   jax.ShapeDtypeStruct((LANES, 1), F32)],
        scratch_shapes=[pltpu.VMEM((LANES, 1), F32)],
        compiler_params=pltpu.CompilerParams(
            dimension_semantics=("arbitrary",), vmem_limit_bytes=VMEM_LIMIT),
        name="out_route",
    )(x2, sbo, hgo, w_out, nw, wr_t, br, su)


def _dispatch_kernel(meta_ref, offs_ref, m_ref, xs_in_ref, xs_ref, sem, *, tm):
    del xs_in_ref

    def row_copies(r):
        p1 = offs_ref[meta_ref[0, r]] + meta_ref[2, r]
        p2 = offs_ref[meta_ref[1, r]] + meta_ref[3, r]
        src = m_ref.at[pl.ds(r, 1), :]
        return (pltpu.make_async_copy(src, xs_ref.at[pl.ds(p1, 1), :], sem),
                pltpu.make_async_copy(src, xs_ref.at[pl.ds(p2, 1), :], sem))

    def start(r, _):
        c1, c2 = row_copies(r)
        c1.start()
        c2.start()
        return 0

    def wait(r, _):
        c1, c2 = row_copies(r)
        c1.wait()
        c2.wait()
        return 0

    lax.fori_loop(0, tm, start, 0)
    lax.fori_loop(0, tm, wait, 0)


def _dispatch(meta, offs, m, n_rows, *, tm):
    n, d = m.shape
    xs0 = jnp.zeros((n_rows, d), F32)
    return pl.pallas_call(
        functools.partial(_dispatch_kernel, tm=tm),
        grid=(n // tm,),
        in_specs=[pl.BlockSpec((8, tm), lambda i: (0, i), memory_space=pltpu.SMEM),
                  pl.BlockSpec(memory_space=pltpu.SMEM),
                  pl.BlockSpec((tm, d), lambda i: (i, 0)),
                  pl.BlockSpec(memory_space=pl.ANY)],
        out_specs=pl.BlockSpec(memory_space=pl.ANY),
        out_shape=jax.ShapeDtypeStruct((n_rows, d), F32),
        scratch_shapes=[pltpu.SemaphoreType.DMA(())],
        input_output_aliases={3: 0},
        compiler_params=pltpu.CompilerParams(
            dimension_semantics=("arbitrary",), vmem_limit_bytes=VMEM_LIMIT),
        name="dispatch",
    )(meta, offs, m, xs0)


def _experts_kernel(te_ref, tv_ref, xs_ref, wg_ref, wu_ref, wd_ref, y_ref):
    t = pl.program_id(0)

    @pl.when(tv_ref[t] != 0)
    def _():
        x = xs_ref[...].astype(BF16)
        hg = jnp.dot(x, wg_ref[0], preferred_element_type=F32)
        hu = jnp.dot(x, wu_ref[0], preferred_element_type=F32)
        act = (hg * _sigmoid(hg) * hu).astype(BF16)
        y_ref[...] = jnp.dot(act, wd_ref[0], preferred_element_type=F32)

    @pl.when(tv_ref[t] == 0)
    def _():
        y_ref[...] = jnp.zeros_like(y_ref)


def _experts(tile_expert, tile_valid, xs, wg, wu, wd, *, tmm):
    n_rows, d = xs.shape
    de = wg.shape[2]
    grid_spec = pltpu.PrefetchScalarGridSpec(
        num_scalar_prefetch=2,
        grid=(n_rows // tmm,),
        in_specs=[pl.BlockSpec((tmm, d), lambda t, te, tv: (t, 0)),
                  pl.BlockSpec((1, d, de), lambda t, te, tv: (te[t], 0, 0)),
                  pl.BlockSpec((1, d, de), lambda t, te, tv: (te[t], 0, 0)),
                  pl.BlockSpec((1, de, d), lambda t, te, tv: (te[t], 0, 0))],
        out_specs=pl.BlockSpec((tmm, d), lambda t, te, tv: (t, 0)),
    )
    return pl.pallas_call(
        _experts_kernel,
        grid_spec=grid_spec,
        out_shape=jax.ShapeDtypeStruct((n_rows, d), F32),
        compiler_params=pltpu.CompilerParams(
            dimension_semantics=("arbitrary",), vmem_limit_bytes=VMEM_LIMIT),
        name="experts",
    )(tile_expert, tile_valid, xs, wg, wu, wd)


def _combine_kernel(meta_ref, offs_ref, h1_ref, gates_ref, p_ref, pnw_ref, wpp_ref, wpg_ref, fnw_ref,
                    y_ref, o_ref, ybuf, sem, *, tm):
    def row_copies(r):
        p1 = offs_ref[meta_ref[0, r]] + meta_ref[2, r]
        p2 = offs_ref[meta_ref[1, r]] + meta_ref[3, r]
        return (pltpu.make_async_copy(y_ref.at[pl.ds(p1, 1), :], ybuf.at[0, pl.ds(r, 1), :], sem),
                pltpu.make_async_copy(y_ref.at[pl.ds(p2, 1), :], ybuf.at[1, pl.ds(r, 1), :], sem))

    def start(r, _):
        c1, c2 = row_copies(r)
        c1.start()
        c2.start()
        return 0

    def wait(r, _):
        c1, c2 = row_copies(r)
        c1.wait()
        c2.wait()
        return 0

    lax.fori_loop(0, tm, start, 0)
    lax.fori_loop(0, tm, wait, 0)

    gts = gates_ref[...]
    h2 = h1_ref[...] + gts[:, 0:1] * ybuf[0] + gts[:, 1:2] * ybuf[1]
    e = jnp.dot(p_ref[...].astype(BF16), wpp_ref[...], preferred_element_type=F32)
    ms = jnp.mean(h2 * h2, axis=-1, keepdims=True)
    hn = (h2 * lax.rsqrt(ms + EPS) * pnw_ref[...]).astype(BF16)
    gate = _sigmoid(jnp.dot(hn, wpg_ref[...], preferred_element_type=F32))
    h3 = h2 + gate * e
    ms3 = jnp.mean(h3 * h3, axis=-1, keepdims=True)
    o_ref[...] = h3 * lax.rsqrt(ms3 + EPS) * fnw_ref[...]


def _combine(meta, offs, h1, gates, p2, pnw, wpp, wpg, fnw, y, *, tm):
    n, d = h1.shape
    tok = lambda width: pl.BlockSpec((tm, width), lambda i: (i, 0))
    const = lambda shape: pl.BlockSpec(shape, lambda i: (0,) * len(shape))
    return pl.pallas_call(
        functools.partial(_combine_kernel, tm=tm),
        grid=(n // tm,),
        in_specs=[pl.BlockSpec((8, tm), lambda i: (0, i), memory_space=pltpu.SMEM),
                  pl.BlockSpec(memory_space=pltpu.SMEM),
                  tok(d), tok(LANES), tok(p2.shape[1]), const((1, d)), const(wpp.shape),
                  const(wpg.shape), const((1, d)),
                  pl.BlockSpec(memory_space=pl.ANY)],
        out_specs=tok(d),
        out_shape=jax.ShapeDtypeStruct((n, d), F32),
        scratch_shapes=[pltpu.VMEM((2, tm, d), F32), pltpu.SemaphoreType.DMA(())],
        compiler_params=pltpu.CompilerParams(
            dimension_semantics=("arbitrary",), vmem_limit_bytes=VMEM_LIMIT),
        name="combine",
    )(meta, offs, h1, gates, p2, pnw, wpp, wpg, fnw, y)


def kernel(x, p, attn_norm_w, w_in, sb_norm_w, hg_lower_bounds, hg_norm_w, w_out, ffn_norm_w,
           w_group_router, b_group_router, w_expert_router, b_expert_router, w_exp_gate, w_exp_up,
           w_exp_down, ple_norm_w, w_ple_proj, w_ple_gate, final_norm_w):
    b, s, d = x.shape
    depth = w_in.shape[0]
    assert depth == 1, "single-layer trunk"
    sbw = sb_norm_w.shape[1]
    hgw = hg_norm_w.shape[1]
    n = b * s
    tm_proj = min(512, s)
    tm_route = min(512, n)
    tm_disp = min(512, n)
    tm_comb = min(256, n)
    tmm = min(256, n)

    wi = w_in[0]
    w_main = jnp.concatenate([wi[:, 0:sbw], wi[:, 2 * sbw:]], axis=1).astype(BF16)
    w_kt = wi[:, sbw:2 * sbw].T.astype(BF16)
    wr_t = jnp.zeros((LANES, d), F32)
    wr_t = wr_t.at[0:N_GROUPS].set(w_group_router[0].T)
    wr_t = wr_t.at[ROUTER_ROW0:ROUTER_ROW0 + N_EXPERTS].set(w_expert_router[0].T)
    br = jnp.zeros((LANES, 1), F32)
    br = br.at[0:N_GROUPS, 0].set(b_group_router[0])
    br = br.at[ROUTER_ROW0:ROUTER_ROW0 + N_EXPERTS, 0].set(b_expert_router[0])

    sbq, sbkt, sbv, hq, hgl, hk, hv, hgate = _in_proj(
        x, attn_norm_w[0][None, :], w_main, w_kt, hg_lower_bounds, tm=tm_proj)
    sbo = _sb_attn(sbq, sbkt, sbv, sb_norm_w)
    hgo = _hgrn2(hq, hgl, hk, hv, hgate, hg_norm_w)

    h1, m, meta, gates, counts = _out_route(
        x.reshape(n, d), sbo.reshape(n, sbw), hgo.reshape(n, hgw), w_out[0].astype(BF16),
        ffn_norm_w[0][None, :], wr_t, br, tm=tm_route)

    cnt = counts[ROUTER_ROW0:ROUTER_ROW0 + N_EXPERTS, 0].astype(jnp.int32)
    padded = ((cnt + tmm - 1) // tmm) * tmm
    ends = jnp.cumsum(padded)
    starts = ends - padded
    n_rows = n * TOP_K + N_EXPERTS * tmm
    n_tiles = n_rows // tmm
    tile_start = jnp.arange(n_tiles, dtype=jnp.int32) * tmm
    tile_valid = (tile_start < ends[-1]).astype(jnp.int32)
    last_start = jnp.maximum(ends[-1] - tmm, 0)
    tile_expert = jnp.searchsorted(ends, jnp.minimum(tile_start, last_start), side="right").astype(jnp.int32)
    tile_expert = jnp.minimum(tile_expert, N_EXPERTS - 1)
    offs = jnp.zeros((LANES,), jnp.int32).at[0:N_EXPERTS].set(starts)

    xs = _dispatch(meta, offs, m, n_rows, tm=tm_disp)
    y = _experts(tile_expert, tile_valid, xs, w_exp_gate[0].astype(BF16), w_exp_up[0].astype(BF16),
                 w_exp_down[0].astype(BF16), tmm=tmm)
    out = _combine(meta, offs, h1, gates, p[0].reshape(n, -1), ple_norm_w[0][None, :],
                   w_ple_proj[0].astype(BF16), w_ple_gate[0].astype(BF16), final_norm_w[None, :],
                   y, tm=tm_comb)
    return out.reshape(b, s, d)
```

```python
import functools

import numpy as np
import jax
import jax.numpy as jnp
from jax import lax
from jax.experimental import pallas as pl
from jax.experimental.pallas import tpu as pltpu

F32 = jnp.float32
BF16 = jnp.bfloat16
EPS = 1e-6

SB_HEADS = 8
SB_HEAD_DIM = 64
HG_HEAD_DIM = 128
HG_CHUNK = 64
N_GROUPS = 4
EXPERTS_PER_GROUP = 8
N_EXPERTS = N_GROUPS * EXPERTS_PER_GROUP
TOP_K = 2
LANES = 128
ROUTER_ROW0 = 8
SB_BLOCK = 128
SB_QUERY_BLOCKS = 2
SB_CHAINS = 2
SB_LOG2_FLOOR = -152.0
LOG2E = 1.4426950408889634
VMEM_LIMIT = 56 * 1024 * 1024

_NT = (((1,), (1,)), ((), ()))
_TN = (((0,), (0,)), ((), ()))


def _sigmoid(x):
    return 1.0 / (1.0 + jnp.exp(-x))


def _split2(x):
    hi = x.astype(BF16)
    lo = (x - hi.astype(F32)).astype(BF16)
    return hi, lo


def _split3(x):
    hi = x.astype(BF16)
    r = x - hi.astype(F32)
    mid = r.astype(BF16)
    lo = (r - mid.astype(F32)).astype(BF16)
    return hi, mid, lo


def _in_proj_kernel(x_ref, nw_ref, w_ref, wkt_ref, lbp_ref,
                    sbq_ref, sbkt_ref, sbve_ref, sbvo_ref, hq_ref, hgl_ref, hk_ref, hv_ref, hgate_ref,
                    *, sbw, hgw):
    x = x_ref[0]
    ms = jnp.mean(x * x, axis=-1, keepdims=True)
    a = (x * lax.rsqrt(ms + EPS) * nw_ref[...]).astype(BF16)

    def seg(lo, width):
        return jnp.dot(a, w_ref[:, lo:lo + width], preferred_element_type=F32)

    sbq_ref[0] = (seg(0, sbw) * (SB_HEAD_DIM ** -0.5 * LOG2E)).astype(BF16)
    sbkt_ref[0] = lax.dot_general(wkt_ref[...], a, _NT, preferred_element_type=F32).astype(BF16)
    v = seg(sbw, sbw)
    even_head = (lax.broadcasted_iota(jnp.int32, v.shape, 1) & SB_HEAD_DIM) == 0
    sbve_ref[0] = jnp.where(even_head, v, 0.0).astype(BF16)
    sbvo_ref[0] = jnp.where(even_head, 0.0, v).astype(BF16)
    c = 2 * sbw
    q = seg(c, hgw)
    hq_ref[0] = q * _sigmoid(q)
    p0 = lbp_ref[0:1, :]
    p1 = lbp_ref[1:2, :]
    pm = jnp.maximum(p0, p1)
    e0 = jnp.exp(p0 - pm)
    e1 = jnp.exp(p1 - pm)
    lb = e0 / (e0 + e1)
    fz = seg(c + hgw, hgw)
    sg = _sigmoid(fz)
    f = lb + (1.0 - lb) * sg
    hgl_ref[0] = jnp.log(f)
    hk_ref[0] = 1.0 - f
    hv_ref[0] = seg(c + 2 * hgw, hgw).astype(BF16)
    g = seg(c + 3 * hgw, hgw)
    hgate_ref[0] = g * _sigmoid(g)


def _in_proj(x, nw, w_main, w_kt, lbp, *, tm):
    b, s, d = x.shape
    sbw = w_kt.shape[0]
    hgw = (w_main.shape[1] - 2 * sbw) // 4
    tok = lambda width: pl.BlockSpec((1, tm, width), lambda bi, i: (bi, i, 0))
    const = lambda shape: pl.BlockSpec(shape, lambda bi, i: (0,) * len(shape))
    out_shape = [
        jax.ShapeDtypeStruct((b, s, sbw), BF16),
        jax.ShapeDtypeStruct((b, sbw, s), BF16),
        jax.ShapeDtypeStruct((b, s, sbw), BF16),
        jax.ShapeDtypeStruct((b, s, sbw), BF16),
        jax.ShapeDtypeStruct((b, s, hgw), F32),
        jax.ShapeDtypeStruct((b, s, hgw), F32),
        jax.ShapeDtypeStruct((b, s, hgw), F32),
        jax.ShapeDtypeStruct((b, s, hgw), BF16),
        jax.ShapeDtypeStruct((b, s, hgw), F32),
    ]
    out_specs = [tok(sbw), pl.BlockSpec((1, sbw, tm), lambda bi, i: (bi, 0, i)), tok(sbw), tok(sbw),
                 tok(hgw), tok(hgw), tok(hgw), tok(hgw), tok(hgw)]
    return pl.pallas_call(
        functools.partial(_in_proj_kernel, sbw=sbw, hgw=hgw),
        grid=(b, s // tm),
        in_specs=[tok(d), const((1, d)), const(w_main.shape), const(w_kt.shape), const(lbp.shape)],
        out_specs=out_specs,
        out_shape=out_shape,
        compiler_params=pltpu.CompilerParams(
            dimension_semantics=("arbitrary", "arbitrary"), vmem_limit_bytes=VMEM_LIMIT),
        name="in_proj",
    )(x, nw, w_main, w_kt, lbp)


def _sb_cumsum_weights():
    j = np.arange(SB_BLOCK)[:, None]
    s = np.arange(SB_BLOCK)[None, :]
    half = np.concatenate([(j > s).astype(np.float32), np.ones((SB_BLOCK, SB_BLOCK), np.float32)], axis=1)
    return jnp.asarray(np.concatenate([half, half], axis=0), dtype=BF16)


def _sb_attn_kernel(q_ref, kt_ref, ve_ref, vo_ref, nw_ref, cw_ref, o_ref, *, n_groups, chains, qb):
    blk = SB_BLOCK
    dh = SB_HEAD_DIM
    rq = qb * blk
    rr = 2 * rq
    first = lax.broadcasted_iota(jnp.int32, (rq, LANES), 1) < dh
    rowpos = lax.broadcasted_iota(jnp.int32, (rr, blk), 0) & (rq - 1)
    colpos = lax.broadcasted_iota(jnp.int32, (rr, blk), 1)
    cw = cw_ref[...]

    def step(qs, j, q0, v_scale, carry, acc, masked):
        k0 = pl.multiple_of(j * blk, blk)
        kt = kt_ref[0, :, pl.ds(k0, blk)]
        vst = jnp.concatenate([ve_ref[0, pl.ds(k0, blk), :], vo_ref[0, pl.ds(k0, blk), :]], axis=0)
        if v_scale is not None:
            vst = vst * v_scale
        z = jnp.dot(qs, kt, preferred_element_type=F32)
        sp = jnp.log2(1.0 + jnp.exp2(-jnp.abs(z)))
        ls = jnp.minimum(z, 0.0) - sp
        lk = ls - z
        if masked:
            before = (k0 + colpos) < (q0 + rowpos)
            lk = jnp.where(before, lk, 0.0)
        hi, lo = _split2(lk)
        cs = jnp.dot(jnp.concatenate([hi, lo], axis=1), cw, preferred_element_type=F32)
        a = jnp.exp2(ls + cs[:, :blk] + carry)
        if masked:
            a = jnp.where(before, a, 0.0)
        ab = a.astype(BF16)
        acc = acc + jnp.dot(jnp.concatenate([ab[:rq], ab[rq:]], axis=1), vst, preferred_element_type=F32)
        carry = carry + cs[:, blk:]
        return carry, acc

    def highest(carries):
        m = carries[0]
        for c in carries[1:]:
            m = jnp.maximum(m, c)
        return jnp.max(m)

    def group(gi, _):
        sis = [gi * chains + c for c in range(chains)]
        qss, carries, accs = [], [], []
        for si in sis:
            q0 = pl.multiple_of(si * rq, rq)
            q2 = q_ref[0, pl.ds(q0, rq), :].astype(F32)
            qs = jnp.concatenate([jnp.where(first, q2, 0.0), jnp.where(first, 0.0, q2)], axis=0).astype(BF16)
            carry = jnp.zeros((rr, blk), F32)
            acc = jnp.zeros((rq, LANES), F32)
            for i in range(qb):
                carry, acc = step(qs, si * qb + (qb - 1 - i), q0, None, carry, acc, True)
            qss.append(qs)
            carries.append(carry)
            accs.append(acc)

        def cond(st):
            n, top = st[0], st[1]
            return (n <= sis[-1] * qb) & (top > SB_LOG2_FLOOR)

        def body(st):
            n = st[0]
            cs_in, as_in = st[2], st[3]
            cs_out, as_out = [], []
            for c, si in enumerate(sis):
                d = si * qb - n
                v_scale = None
                if c < chains - 1:
                    v_scale = jnp.where(d >= 0, 1.0, 0.0).astype(BF16)
                    d = jnp.maximum(d, 0)
                cc, aa = step(qss[c], d, None, v_scale, cs_in[c], as_in[c], False)
                cs_out.append(cc)
                as_out.append(aa)
            return n + 1, highest(cs_out), tuple(cs_out), tuple(as_out)

        st = lax.while_loop(cond, body, (jnp.int32(1), highest(carries), tuple(carries), tuple(accs)))
        for c, si in enumerate(sis):
            acc = st[3][c]
            q0 = pl.multiple_of(si * rq, rq)
            sq = acc * acc
            s0 = jnp.sum(jnp.where(first, sq, 0.0), axis=-1, keepdims=True)
            s1 = jnp.sum(jnp.where(first, 0.0, sq), axis=-1, keepdims=True)
            ms = jnp.where(first, s0, s1) * (1.0 / dh)
            o_ref[0, pl.ds(q0, rq), :] = (acc * lax.rsqrt(ms + EPS) * nw_ref[...]).astype(o_ref.dtype)
        return 0

    lax.fori_loop(0, n_groups, group, 0)


def _sb_attn(q, kt, v_even, v_odd, nw, *, chains, qb):
    b, s, w = q.shape
    n_pairs = w // LANES
    cw = _sb_cumsum_weights()
    tok = pl.BlockSpec((1, s, LANES), lambda bi, hp: (bi, 0, hp))
    return pl.pallas_call(
        functools.partial(_sb_attn_kernel, n_groups=s // (chains * qb * SB_BLOCK), chains=chains, qb=qb),
        grid=(b, n_pairs),
        in_specs=[
            tok,
            pl.BlockSpec((1, LANES, s), lambda bi, hp: (bi, hp, 0)),
            tok, tok,
            pl.BlockSpec((1, LANES), lambda bi, hp: (0, hp)),
            pl.BlockSpec(cw.shape, lambda bi, hp: (0, 0)),
        ],
        out_specs=tok,
        out_shape=jax.ShapeDtypeStruct((b, s, w), BF16),
        compiler_params=pltpu.CompilerParams(
            dimension_semantics=("arbitrary", "arbitrary"), vmem_limit_bytes=VMEM_LIMIT),
        name="sb_attn",
    )(q, kt, v_even, v_odd, nw, cw)


def _hgrn_levels():
    hs = []
    h = HG_CHUNK // 2
    while h >= 1:
        hs.append(h)
        h //= 2
    return hs


def _hgrn_consts():
    n = HG_CHUNK
    t = np.arange(n)[:, None]
    j = np.arange(n)[None, :]
    mats = [j <= t, j > t]
    masks = []
    for h in _hgrn_levels():
        mid = (t // (2 * h)) * (2 * h) + h - 1
        upper = (t % (2 * h)) >= h
        mats.append(upper & (j > mid) & (j <= t))
        mats.append((~upper) & (j > t) & (j <= mid))
        masks.append((t // (2 * h)) == (j // (2 * h)))
    masks.append(t == j)
    sums = np.concatenate(mats, axis=0).astype(np.float32)
    pm = np.concatenate(masks, axis=0).astype(np.float32)
    return jnp.asarray(sums, dtype=BF16), jnp.asarray(pm, dtype=F32)


def _hgrn2_kernel(q_ref, gl_ref, k_ref, v_ref, gate_ref, nw_ref, cs_ref, pm_ref, o_ref, *, n_chunks):
    n = HG_CHUNK
    dk = HG_HEAD_DIM
    levels = _hgrn_levels()
    rows = lax.broadcasted_iota(jnp.int32, (n, dk), 0)

    def chunk(c, state_t):
        r0 = pl.multiple_of(c * n, n)
        sl = pl.ds(r0, n)
        q = q_ref[0, sl, :]
        g = gl_ref[0, sl, :]
        k = k_ref[0, sl, :]
        v = v_ref[0, sl, :]
        g_hi, g_mid, g_lo = _split3(g)
        cs = cs_ref[...]
        d = (jnp.dot(cs, g_hi, preferred_element_type=F32)
             + jnp.dot(cs, g_mid, preferred_element_type=F32)
             + jnp.dot(cs, g_lo, preferred_element_type=F32))
        bc = d[0:n]
        rem = d[n:2 * n]
        b_last = bc[n - 1:n, :]
        p = pm_ref[len(levels) * n:(len(levels) + 1) * n, :] * lax.dot_general(
            q.astype(BF16), k.astype(BF16), _NT, preferred_element_type=F32)
        for li, h in enumerate(levels):
            base = (2 + 2 * li) * n
            upper = (rows & h) != 0
            eq = jnp.where(upper, jnp.exp(d[base:base + n]), 0.0)
            ek = jnp.where(upper, 0.0, jnp.exp(d[base + n:base + 2 * n]))
            s_l = lax.dot_general((q * eq).astype(BF16), (k * ek).astype(BF16), _NT,
                                  preferred_element_type=F32)
            p = p + pm_ref[li * n:(li + 1) * n, :] * s_l
        st_bf = state_t.astype(BF16)
        o = (jnp.dot(p.astype(BF16), v, preferred_element_type=F32)
             + lax.dot_general((q * jnp.exp(bc)).astype(BF16), st_bf, _NT, preferred_element_type=F32))
        kd = (k * jnp.exp(rem)).astype(BF16)
        new_state_t = state_t * jnp.exp(b_last) + lax.dot_general(v, kd, _TN, preferred_element_type=F32)
        ms = jnp.mean(o * o, axis=-1, keepdims=True)
        o_ref[0, sl, :] = (o * lax.rsqrt(ms + EPS) * nw_ref[...] * gate_ref[0, sl, :]).astype(o_ref.dtype)
        return new_state_t

    lax.fori_loop(0, n_chunks, chunk, jnp.zeros((dk, dk), F32))


def _hgrn2(q, gl, k, v, gate, nw):
    b, s, w = q.shape
    heads = w // HG_HEAD_DIM
    cs, pm = _hgrn_consts()
    tok = pl.BlockSpec((1, s, HG_HEAD_DIM), lambda bi, h: (bi, 0, h))
    return pl.pallas_call(
        functools.partial(_hgrn2_kernel, n_chunks=s // HG_CHUNK),
        grid=(b, heads),
        in_specs=[tok, tok, tok, tok, tok,
                  pl.BlockSpec((1, HG_HEAD_DIM), lambda bi, h: (0, h)),
                  pl.BlockSpec(cs.shape, lambda bi, h: (0, 0)),
                  pl.BlockSpec(pm.shape, lambda bi, h: (0, 0))],
        out_specs=tok,
        out_shape=jax.ShapeDtypeStruct((b, s, w), BF16),
        compiler_params=pltpu.CompilerParams(
            dimension_semantics=("arbitrary", "arbitrary"), vmem_limit_bytes=VMEM_LIMIT),
        name="hgrn2",
    )(q, gl, k, v, gate, nw, cs, pm)


def _out_route_kernel(x_ref, sbo_ref, hgo_ref, wo_ref, nw_ref, wr_ref, br_ref, su_ref,
                      h1_ref, m_ref, meta_ref, gates_ref, counts_ref, carry_ref, *, sbw):
    i = pl.program_id(0)

    @pl.when(i == 0)
    def _():
        carry_ref[...] = jnp.zeros_like(carry_ref)

    h1 = (x_ref[...]
          + jnp.dot(sbo_ref[...], wo_ref[0:sbw, :], preferred_element_type=F32)
          + jnp.dot(hgo_ref[...], wo_ref[sbw:, :], preferred_element_type=F32))
    h1_ref[...] = h1
    ms = jnp.mean(h1 * h1, axis=-1, keepdims=True)
    m = h1 * lax.rsqrt(ms + EPS) * nw_ref[...]
    m_ref[...] = m

    m3 = _split3(m)
    w3 = _split3(wr_ref[...])
    logits = br_ref[...]
    for wi, mi in ((0, 0), (0, 1), (1, 0), (0, 2), (1, 1), (2, 0)):
        logits = logits + lax.dot_general(w3[wi], m3[mi], _NT, preferred_element_type=F32)
    rid = lax.broadcasted_iota(jnp.int32, logits.shape, 0)
    neg = -jnp.inf
    big = jnp.int32(2 * LANES)

    def first_argmax(vals):
        vmax = jnp.max(vals, axis=0, keepdims=True)
        idx = jnp.min(jnp.where(vals == vmax, rid, big), axis=0, keepdims=True)
        return vmax, idx

    is_group = rid < N_GROUPS
    gmax, g_idx = first_argmax(jnp.where(is_group, logits, neg))
    gsum = jnp.sum(jnp.where(is_group, jnp.exp(logits - gmax), 0.0), axis=0, keepdims=True)
    g_prob = 1.0 / gsum
    lo_row = ROUTER_ROW0 + EXPERTS_PER_GROUP * g_idx
    el = jnp.where((rid >= lo_row) & (rid < lo_row + EXPERTS_PER_GROUP), logits, neg)
    v1, i1 = first_argmax(el)
    v2, i2 = first_argmax(jnp.where(rid == i1, neg, el))
    dd = jnp.exp(v2 - v1)
    p1 = 1.0 / (1.0 + dd)
    g1 = p1 * g_prob
    g2 = dd * p1 * g_prob

    hit1 = rid == i1
    hit2 = rid == i2
    onehot = jnp.where(hit1 | hit2, 1.0, 0.0)
    before_cnt = carry_ref[...] + jnp.dot(onehot.astype(BF16), su_ref[...], preferred_element_type=F32)
    r1 = jnp.sum(jnp.where(hit1, before_cnt, 0.0), axis=0, keepdims=True)
    r2 = jnp.sum(jnp.where(hit2, before_cnt, 0.0), axis=0, keepdims=True)
    carry_ref[...] = carry_ref[...] + jnp.sum(onehot, axis=1, keepdims=True)
    counts_ref[...] = carry_ref[...]

    meta_ref[...] = jnp.zeros_like(meta_ref)
    meta_ref[0:1, :] = i1 - ROUTER_ROW0
    meta_ref[1:2, :] = i2 - ROUTER_ROW0
    meta_ref[2:3, :] = r1.astype(jnp.int32)
    meta_ref[3:4, :] = r2.astype(jnp.int32)
    gt = jnp.where(rid == 0, g1, jnp.where(rid == 1, g2, 0.0))
    gates_ref[...] = gt.T


def _out_route(x2, sbo, hgo, w_out, nw, wr_t, br, *, tm):
    n, d = x2.shape
    sbw = sbo.shape[1]
    j = np.arange(tm)[:, None]
    t = np.arange(tm)[None, :]
    su = jnp.asarray((j < t).astype(np.float32), dtype=BF16)
    tok = lambda width: pl.BlockSpec((tm, width), lambda i: (i, 0))
    const = lambda shape: pl.BlockSpec(shape, lambda i: (0,) * len(shape))
    return pl.pallas_call(
        functools.partial(_out_route_kernel, sbw=sbw),
        grid=(n // tm,),
        in_specs=[tok(d), tok(sbw), tok(hgo.shape[1]), const(w_out.shape), const((1, d)),
                  const(wr_t.shape), const(br.shape), const(su.shape)],
        out_specs=[tok(d), tok(d), pl.BlockSpec((8, tm), lambda i: (0, i)), tok(LANES),
                   const((LANES, 1))],
        out_shape=[jax.ShapeDtypeStruct((n, d), F32), jax.ShapeDtypeStruct((n, d), F32),
                   jax.ShapeDtypeStruct((8, n), jnp.int32), jax.ShapeDtypeStruct((n, LANES), F32),
                   jax.ShapeDtypeStruct((LANES, 1), F32)],
        scratch_shapes=[pltpu.VMEM((LANES, 1), F32)],
        compiler_params=pltpu.CompilerParams(
            dimension_semantics=("arbitrary",), vmem_limit_bytes=VMEM_LIMIT),
        name="out_route",
    )(x2, sbo, hgo, w_out, nw, wr_t, br, su)


def _dispatch_kernel(meta_ref, offs_ref, m_ref, xs_in_ref, xs_ref, sem, *, tm):
    del xs_in_ref

    def row_copies(r):
        p1 = offs_ref[meta_ref[0, r]] + meta_ref[2, r]
        p2 = offs_ref[meta_ref[1, r]] + meta_ref[3, r]
        src = m_ref.at[pl.ds(r, 1), :]
        return (pltpu.make_async_copy(src, xs_ref.at[pl.ds(p1, 1), :], sem),
                pltpu.make_async_copy(src, xs_ref.at[pl.ds(p2, 1), :], sem))

    def start(r, _):
        c1, c2 = row_copies(r)
        c1.start()
        c2.start()
        return 0

    def wait(r, _):
        c1, c2 = row_copies(r)
        c1.wait()
        c2.wait()
        return 0

    lax.fori_loop(0, tm, start, 0)
    lax.fori_loop(0, tm, wait, 0)


def _dispatch(meta, offs, m, n_rows, *, tm):
    n, d = m.shape
    xs0 = jnp.zeros((n_rows, d), F32)
    return pl.pallas_call(
        functools.partial(_dispatch_kernel, tm=tm),
        grid=(n // tm,),
        in_specs=[pl.BlockSpec((8, tm), lambda i: (0, i), memory_space=pltpu.SMEM),
                  pl.BlockSpec(memory_space=pltpu.SMEM),
                  pl.BlockSpec((tm, d), lambda i: (i, 0)),
                  pl.BlockSpec(memory_space=pl.ANY)],
        out_specs=pl.BlockSpec(memory_space=pl.ANY),
        out_shape=jax.ShapeDtypeStruct((n_rows, d), F32),
        scratch_shapes=[pltpu.SemaphoreType.DMA(())],
        input_output_aliases={3: 0},
        compiler_params=pltpu.CompilerParams(
            dimension_semantics=("arbitrary",), vmem_limit_bytes=VMEM_LIMIT),
        name="dispatch",
    )(meta, offs, m, xs0)


def _experts_kernel(te_ref, tv_ref, xs_ref, wg_ref, wu_ref, wd_ref, y_ref):
    t = pl.program_id(0)

    @pl.when(tv_ref[t] != 0)
    def _():
        x = xs_ref[...].astype(BF16)
        hg = jnp.dot(x, wg_ref[0], preferred_element_type=F32)
        hu = jnp.dot(x, wu_ref[0], preferred_element_type=F32)
        act = (hg * _sigmoid(hg) * hu).astype(BF16)
        y_ref[...] = jnp.dot(act, wd_ref[0], preferred_element_type=F32)

    @pl.when(tv_ref[t] == 0)
    def _():
        y_ref[...] = jnp.zeros_like(y_ref)


def _experts(tile_expert, tile_valid, xs, wg, wu, wd, *, tmm):
    n_rows, d = xs.shape
    de = wg.shape[2]
    grid_spec = pltpu.PrefetchScalarGridSpec(
        num_scalar_prefetch=2,
        grid=(n_rows // tmm,),
        in_specs=[pl.BlockSpec((tmm, d), lambda t, te, tv: (t, 0)),
                  pl.BlockSpec((1, d, de), lambda t, te, tv: (te[t], 0, 0)),
                  pl.BlockSpec((1, d, de), lambda t, te, tv: (te[t], 0, 0)),
                  pl.BlockSpec((1, de, d), lambda t, te, tv: (te[t], 0, 0))],
        out_specs=pl.BlockSpec((tmm, d), lambda t, te, tv: (t, 0)),
    )
    return pl.pallas_call(
        _experts_kernel,
        grid_spec=grid_spec,
        out_shape=jax.ShapeDtypeStruct((n_rows, d), F32),
        compiler_params=pltpu.CompilerParams(
            dimension_semantics=("arbitrary",), vmem_limit_bytes=VMEM_LIMIT),
        name="experts",
    )(tile_expert, tile_valid, xs, wg, wu, wd)


def _combine_kernel(meta_ref, offs_ref, h1_ref, gates_ref, p_ref, pnw_ref, wpp_ref, wpg_ref, fnw_ref,
                    y_ref, o_ref, ybuf, sem, *, tm):
    def row_copies(r):
        p1 = offs_ref[meta_ref[0, r]] + meta_ref[2, r]
        p2 = offs_ref[meta_ref[1, r]] + meta_ref[3, r]
        return (pltpu.make_async_copy(y_ref.at[pl.ds(p1, 1), :], ybuf.at[0, pl.ds(r, 1), :], sem),
                pltpu.make_async_copy(y_ref.at[pl.ds(p2, 1), :], ybuf.at[1, pl.ds(r, 1), :], sem))

    def start(r, _):
        c1, c2 = row_copies(r)
        c1.start()
        c2.start()
        return 0

    def wait(r, _):
        c1, c2 = row_copies(r)
        c1.wait()
        c2.wait()
        return 0

    lax.fori_loop(0, tm, start, 0)
    lax.fori_loop(0, tm, wait, 0)

    gts = gates_ref[...]
    h2 = h1_ref[...] + gts[:, 0:1] * ybuf[0] + gts[:, 1:2] * ybuf[1]
    e = jnp.dot(p_ref[...].astype(BF16), wpp_ref[...], preferred_element_type=F32)
    ms = jnp.mean(h2 * h2, axis=-1, keepdims=True)
    hn = (h2 * lax.rsqrt(ms + EPS) * pnw_ref[...]).astype(BF16)
    gate = _sigmoid(jnp.dot(hn, wpg_ref[...], preferred_element_type=F32))
    h3 = h2 + gate * e
    ms3 = jnp.mean(h3 * h3, axis=-1, keepdims=True)
    o_ref[...] = h3 * lax.rsqrt(ms3 + EPS) * fnw_ref[...]


def _combine(meta, offs, h1, gates, p2, pnw, wpp, wpg, fnw, y, *, tm):
    n, d = h1.shape
    tok = lambda width: pl.BlockSpec((tm, width), lambda i: (i, 0))
    const = lambda shape: pl.BlockSpec(shape, lambda i: (0,) * len(shape))
    return pl.pallas_call(
        functools.partial(_combine_kernel, tm=tm),
        grid=(n // tm,),
        in_specs=[pl.BlockSpec((8, tm), lambda i: (0, i), memory_space=pltpu.SMEM),
                  pl.BlockSpec(memory_space=pltpu.SMEM),
                  tok(d), tok(LANES), tok(p2.shape[1]), const((1, d)), const(wpp.shape),
                  const(wpg.shape), const((1, d)),
                  pl.BlockSpec(memory_space=pl.ANY)],
        out_specs=tok(d),
        out_shape=jax.ShapeDtypeStruct((n, d), F32),
        scratch_shapes=[pltpu.VMEM((2, tm, d), F32), pltpu.SemaphoreType.DMA(())],
        compiler_params=pltpu.CompilerParams(
            dimension_semantics=("arbitrary",), vmem_limit_bytes=VMEM_LIMIT),
        name="combine",
    )(meta, offs, h1, gates, p2, pnw, wpp, wpg, fnw, y)


def kernel(x, p, attn_norm_w, w_in, sb_norm_w, hg_lower_bounds, hg_norm_w, w_out, ffn_norm_w,
           w_group_router, b_group_router, w_expert_router, b_expert_router, w_exp_gate, w_exp_up,
           w_exp_down, ple_norm_w, w_ple_proj, w_ple_gate, final_norm_w):
    b, s, d = x.shape
    depth = w_in.shape[0]
    assert depth == 1, "single-layer trunk"
    sbw = sb_norm_w.shape[1]
    hgw = hg_norm_w.shape[1]
    n = b * s
    tm_proj = min(512, s)
    tm_route = min(512, n)
    tm_disp = min(512, n)
    tm_comb = min(256, n)
    tmm = min(256, n)

    wi = w_in[0]
    w_main = jnp.concatenate([wi[:, 0:sbw], wi[:, 2 * sbw:]], axis=1).astype(BF16)
    w_kt = wi[:, sbw:2 * sbw].T.astype(BF16)
    wr_t = jnp.zeros((LANES, d), F32)
    wr_t = wr_t.at[0:N_GROUPS].set(w_group_router[0].T)
    wr_t = wr_t.at[ROUTER_ROW0:ROUTER_ROW0 + N_EXPERTS].set(w_expert_router[0].T)
    br = jnp.zeros((LANES, 1), F32)
    br = br.at[0:N_GROUPS, 0].set(b_group_router[0])
    br = br.at[ROUTER_ROW0:ROUTER_ROW0 + N_EXPERTS, 0].set(b_expert_router[0])

    sbq, sbkt, sbve, sbvo, hq, hgl, hk, hv, hgate = _in_proj(
        x, attn_norm_w[0][None, :], w_main, w_kt, hg_lower_bounds, tm=tm_proj)
    sbo = _sb_attn(sbq, sbkt, sbve, sbvo, sb_norm_w, chains=SB_CHAINS, qb=SB_QUERY_BLOCKS)
    hgo = _hgrn2(hq, hgl, hk, hv, hgate, hg_norm_w)

    h1, m, meta, gates, counts = _out_route(
        x.reshape(n, d), sbo.reshape(n, sbw), hgo.reshape(n, hgw), w_out[0].astype(BF16),
        ffn_norm_w[0][None, :], wr_t, br, tm=tm_route)

    cnt = counts[ROUTER_ROW0:ROUTER_ROW0 + N_EXPERTS, 0].astype(jnp.int32)
    padded = ((cnt + tmm - 1) // tmm) * tmm
    ends = jnp.cumsum(padded)
    starts = ends - padded
    n_rows = n * TOP_K + N_EXPERTS * tmm
    n_tiles = n_rows // tmm
    tile_start = jnp.arange(n_tiles, dtype=jnp.int32) * tmm
    tile_valid = (tile_start < ends[-1]).astype(jnp.int32)
    last_start = jnp.maximum(ends[-1] - tmm, 0)
    tile_expert = jnp.searchsorted(ends, jnp.minimum(tile_start, last_start), side="right").astype(jnp.int32)
    tile_expert = jnp.minimum(tile_expert, N_EXPERTS - 1)
    offs = jnp.zeros((LANES,), jnp.int32).at[0:N_EXPERTS].set(starts)

    xs = _dispatch(meta, offs, m, n_rows, tm=tm_disp)
    y = _experts(tile_expert, tile_valid, xs, w_exp_gate[0].astype(BF16), w_exp_up[0].astype(BF16),
                 w_exp_down[0].astype(BF16), tmm=tmm)
    out = _combine(meta, offs, h1, gates, p[0].reshape(n, -1), ple_norm_w[0][None, :],
                   w_ple_proj[0].astype(BF16), w_ple_gate[0].astype(BF16), final_norm_w[None, :],
                   y, tm=tm_comb)
    return out.reshape(b, s, d)
```

```python
import functools

import numpy as np
import jax
import jax.numpy as jnp
from jax import lax
from jax.experimental import pallas as pl
from jax.experimental.pallas import tpu as pltpu

F32 = jnp.float32
BF16 = jnp.bfloat16
EPS = 1e-6

SB_HEADS = 8
SB_HEAD_DIM = 64
HG_HEAD_DIM = 128
HG_CHUNK = 64
N_GROUPS = 4
EXPERTS_PER_GROUP = 8
N_EXPERTS = N_GROUPS * EXPERTS_PER_GROUP
TOP_K = 2
LANES = 128
ROUTER_ROW0 = 8
SB_BLOCK = 128
SB_QUERY_BLOCKS = 2
SB_CHAINS = 2
SB_LOG2_FLOOR = -152.0
LOG2E = 1.4426950408889634
VMEM_LIMIT = 56 * 1024 * 1024

_NT = (((1,), (1,)), ((), ()))
_TN = (((0,), (0,)), ((), ()))


def _sigmoid(x):
    return 1.0 / (1.0 + jnp.exp(-x))


def _split2(x):
    hi = x.astype(BF16)
    lo = (x - hi.astype(F32)).astype(BF16)
    return hi, lo


def _split3(x):
    hi = x.astype(BF16)
    r = x - hi.astype(F32)
    mid = r.astype(BF16)
    lo = (r - mid.astype(F32)).astype(BF16)
    return hi, mid, lo


def _in_proj_kernel(x_ref, nw_ref, w_ref, wkt_ref, lbp_ref,
                    sbq_ref, sbkt_ref, sbve_ref, sbvo_ref, hq_ref, hgl_ref, hk_ref, hv_ref, hgate_ref,
                    *, sbw, hgw):
    x = x_ref[0]
    ms = jnp.mean(x * x, axis=-1, keepdims=True)
    a = (x * lax.rsqrt(ms + EPS) * nw_ref[...]).astype(BF16)

    def seg(lo, width):
        return jnp.dot(a, w_ref[:, lo:lo + width], preferred_element_type=F32)

    sbq_ref[0] = (seg(0, sbw) * (SB_HEAD_DIM ** -0.5 * LOG2E)).astype(BF16)
    sbkt_ref[0] = lax.dot_general(wkt_ref[...], a, _NT, preferred_element_type=F32).astype(BF16)
    v = seg(sbw, sbw)
    even_head = (lax.broadcasted_iota(jnp.int32, v.shape, 1) & SB_HEAD_DIM) == 0
    sbve_ref[0] = jnp.where(even_head, v, 0.0).astype(BF16)
    sbvo_ref[0] = jnp.where(even_head, 0.0, v).astype(BF16)
    c = 2 * sbw
    q = seg(c, hgw)
    hq_ref[0] = q * _sigmoid(q)
    p0 = lbp_ref[0:1, :]
    p1 = lbp_ref[1:2, :]
    pm = jnp.maximum(p0, p1)
    e0 = jnp.exp(p0 - pm)
    e1 = jnp.exp(p1 - pm)
    lb = e0 / (e0 + e1)
    fz = seg(c + hgw, hgw)
    sg = _sigmoid(fz)
    f = lb + (1.0 - lb) * sg
    hgl_ref[0] = jnp.log(f)
    hk_ref[0] = 1.0 - f
    hv_ref[0] = seg(c + 2 * hgw, hgw).astype(BF16)
    g = seg(c + 3 * hgw, hgw)
    hgate_ref[0] = g * _sigmoid(g)


def _in_proj(x, nw, w_main, w_kt, lbp, *, tm):
    b, s, d = x.shape
    sbw = w_kt.shape[0]
    hgw = (w_main.shape[1] - 2 * sbw) // 4
    tok = lambda width: pl.BlockSpec((1, tm, width), lambda bi, i: (bi, i, 0))
    const = lambda shape: pl.BlockSpec(shape, lambda bi, i: (0,) * len(shape))
    out_shape = [
        jax.ShapeDtypeStruct((b, s, sbw), BF16),
        jax.ShapeDtypeStruct((b, sbw, s), BF16),
        jax.ShapeDtypeStruct((b, s, sbw), BF16),
        jax.ShapeDtypeStruct((b, s, sbw), BF16),
        jax.ShapeDtypeStruct((b, s, hgw), F32),
        jax.ShapeDtypeStruct((b, s, hgw), F32),
        jax.ShapeDtypeStruct((b, s, hgw), F32),
        jax.ShapeDtypeStruct((b, s, hgw), BF16),
        jax.ShapeDtypeStruct((b, s, hgw), F32),
    ]
    out_specs = [tok(sbw), pl.BlockSpec((1, sbw, tm), lambda bi, i: (bi, 0, i)), tok(sbw), tok(sbw),
                 tok(hgw), tok(hgw), tok(hgw), tok(hgw), tok(hgw)]
    return pl.pallas_call(
        functools.partial(_in_proj_kernel, sbw=sbw, hgw=hgw),
        grid=(b, s // tm),
        in_specs=[tok(d), const((1, d)), const(w_main.shape), const(w_kt.shape), const(lbp.shape)],
        out_specs=out_specs,
        out_shape=out_shape,
        compiler_params=pltpu.CompilerParams(
            dimension_semantics=("arbitrary", "arbitrary"), vmem_limit_bytes=VMEM_LIMIT),
        name="in_proj",
    )(x, nw, w_main, w_kt, lbp)


def _sb_cumsum_weights():
    j = np.arange(SB_BLOCK)[:, None]
    s = np.arange(SB_BLOCK)[None, :]
    half = np.concatenate([(j > s).astype(np.float32), np.ones((SB_BLOCK, SB_BLOCK), np.float32)], axis=1)
    return jnp.asarray(np.concatenate([half, half], axis=0), dtype=BF16)


def _sb_attn_kernel(q_ref, kt_ref, ve_ref, vo_ref, nw_ref, cw_ref, o_ref, *, n_groups, chains, qb):
    blk = SB_BLOCK
    dh = SB_HEAD_DIM
    rq = qb * blk
    rr = 2 * rq
    first = lax.broadcasted_iota(jnp.int32, (rq, LANES), 1) < dh
    rowpos = lax.broadcasted_iota(jnp.int32, (rr, blk), 0) & (rq - 1)
    colpos = lax.broadcasted_iota(jnp.int32, (rr, blk), 1)
    cw = cw_ref[...]

    def step(qs, j, q0, v_scale, carry, acc, masked):
        k0 = pl.multiple_of(j * blk, blk)
        kt = kt_ref[0, :, pl.ds(k0, blk)]
        vst = jnp.concatenate([ve_ref[0, pl.ds(k0, blk), :], vo_ref[0, pl.ds(k0, blk), :]], axis=0)
        if v_scale is not None:
            vst = vst * v_scale
        z = jnp.dot(qs, kt, preferred_element_type=F32)
        sp = jnp.log2(1.0 + jnp.exp2(-jnp.abs(z)))
        ls = jnp.minimum(z, 0.0) - sp
        lk = ls - z
        if masked:
            before = (k0 + colpos) < (q0 + rowpos)
            lk = jnp.where(before, lk, 0.0)
        hi, lo = _split2(lk)
        cs = jnp.dot(jnp.concatenate([hi, lo], axis=1), cw, preferred_element_type=F32)
        a = jnp.exp2(ls + cs[:, :blk] + carry)
        if masked:
            a = jnp.where(before, a, 0.0)
        ab = a.astype(BF16)
        acc = acc + jnp.dot(jnp.concatenate([ab[:rq], ab[rq:]], axis=1), vst, preferred_element_type=F32)
        carry = carry + cs[:, blk:]
        return carry, acc

    def highest(carries):
        m = carries[0]
        for c in carries[1:]:
            m = jnp.maximum(m, c)
        return jnp.max(m)

    def group(gi, _):
        sis = [gi * chains + c for c in range(chains)]
        qss, carries, accs = [], [], []
        for si in sis:
            q0 = pl.multiple_of(si * rq, rq)
            q2 = q_ref[0, pl.ds(q0, rq), :].astype(F32)
            qs = jnp.concatenate([jnp.where(first, q2, 0.0), jnp.where(first, 0.0, q2)], axis=0).astype(BF16)
            carry = jnp.zeros((rr, blk), F32)
            acc = jnp.zeros((rq, LANES), F32)
            for i in range(qb):
                carry, acc = step(qs, si * qb + (qb - 1 - i), q0, None, carry, acc, True)
            qss.append(qs)
            carries.append(carry)
            accs.append(acc)

        def cond(st):
            n, top = st[0], st[1]
            return (n <= sis[-1] * qb) & (top > SB_LOG2_FLOOR)

        def body(st):
            n = st[0]
            cs_in, as_in = st[2], st[3]
            cs_out, as_out = [], []
            for c, si in enumerate(sis):
                d = si * qb - n
                v_scale = None
                if c < chains - 1:
                    v_scale = jnp.where(d >= 0, 1.0, 0.0).astype(BF16)
                    d = jnp.maximum(d, 0)
                cc, aa = step(qss[c], d, None, v_scale, cs_in[c], as_in[c], False)
                cs_out.append(cc)
                as_out.append(aa)
            return n + 1, highest(cs_out), tuple(cs_out), tuple(as_out)

        st = lax.while_loop(cond, body, (jnp.int32(1), highest(carries), tuple(carries), tuple(accs)))
        for c, si in enumerate(sis):
            acc = st[3][c]
            q0 = pl.multiple_of(si * rq, rq)
            sq = acc * acc
            s0 = jnp.sum(jnp.where(first, sq, 0.0), axis=-1, keepdims=True)
            s1 = jnp.sum(jnp.where(first, 0.0, sq), axis=-1, keepdims=True)
            ms = jnp.where(first, s0, s1) * (1.0 / dh)
            o_ref[0, pl.ds(q0, rq), :] = (acc * lax.rsqrt(ms + EPS) * nw_ref[...]).astype(o_ref.dtype)
        return 0

    lax.fori_loop(0, n_groups, group, 0)


def _sb_attn(q, kt, v_even, v_odd, nw, *, chains, qb):
    b, s, w = q.shape
    n_pairs = w // LANES
    cw = _sb_cumsum_weights()
    tok = pl.BlockSpec((1, s, LANES), lambda bi, hp: (bi, 0, hp))
    return pl.pallas_call(
        functools.partial(_sb_attn_kernel, n_groups=s // (chains * qb * SB_BLOCK), chains=chains, qb=qb),
        grid=(b, n_pairs),
        in_specs=[
            tok,
            pl.BlockSpec((1, LANES, s), lambda bi, hp: (bi, hp, 0)),
            tok, tok,
            pl.BlockSpec((1, LANES), lambda bi, hp: (0, hp)),
            pl.BlockSpec(cw.shape, lambda bi, hp: (0, 0)),
        ],
        out_specs=tok,
        out_shape=jax.ShapeDtypeStruct((b, s, w), BF16),
        compiler_params=pltpu.CompilerParams(
            dimension_semantics=("arbitrary", "arbitrary"), vmem_limit_bytes=VMEM_LIMIT),
        name="sb_attn",
    )(q, kt, v_even, v_odd, nw, cw)


def _hgrn_levels():
    hs = []
    h = HG_CHUNK // 2
    while h >= 1:
        hs.append(h)
        h //= 2
    return hs


def _hgrn_consts():
    n = HG_CHUNK
    t = np.arange(n)[:, None]
    j = np.arange(n)[None, :]
    mats = [j <= t, j > t]
    masks = []
    for h in _hgrn_levels():
        mid = (t // (2 * h)) * (2 * h) + h - 1
        upper = (t % (2 * h)) >= h
        mats.append(upper & (j > mid) & (j <= t))
        mats.append((~upper) & (j > t) & (j <= mid))
        masks.append((t // (2 * h)) == (j // (2 * h)))
    masks.append(t == j)
    sums = np.concatenate(mats, axis=0).astype(np.float32)
    pm = np.concatenate(masks, axis=0).astype(np.float32)
    return jnp.asarray(sums, dtype=BF16), jnp.asarray(pm, dtype=F32)


def _hgrn2_kernel(q_ref, gl_ref, k_ref, v_ref, gate_ref, nw_ref, cs_ref, pm_ref, o_ref, *, n_chunks):
    n = HG_CHUNK
    dk = HG_HEAD_DIM
    levels = _hgrn_levels()
    rows = lax.broadcasted_iota(jnp.int32, (n, dk), 0)

    def chunk(c, state_t):
        r0 = pl.multiple_of(c * n, n)
        sl = pl.ds(r0, n)
        q = q_ref[0, sl, :]
        g = gl_ref[0, sl, :]
        k = k_ref[0, sl, :]
        v = v_ref[0, sl, :]
        g_hi, g_mid, g_lo = _split3(g)
        cs = cs_ref[...]
        d = (jnp.dot(cs, g_hi, preferred_element_type=F32)
             + jnp.dot(cs, g_mid, preferred_element_type=F32)
             + jnp.dot(cs, g_lo, preferred_element_type=F32))
        bc = d[0:n]
        rem = d[n:2 * n]
        b_last = bc[n - 1:n, :]
        p = pm_ref[len(levels) * n:(len(levels) + 1) * n, :] * lax.dot_general(
            q.astype(BF16), k.astype(BF16), _NT, preferred_element_type=F32)
        for li, h in enumerate(levels):
            base = (2 + 2 * li) * n
            upper = (rows & h) != 0
            eq = jnp.where(upper, jnp.exp(d[base:base + n]), 0.0)
            ek = jnp.where(upper, 0.0, jnp.exp(d[base + n:base + 2 * n]))
            s_l = lax.dot_general((q * eq).astype(BF16), (k * ek).astype(BF16), _NT,
                                  preferred_element_type=F32)
            p = p + pm_ref[li * n:(li + 1) * n, :] * s_l
        st_bf = state_t.astype(BF16)
        o = (jnp.dot(p.astype(BF16), v, preferred_element_type=F32)
             + lax.dot_general((q * jnp.exp(bc)).astype(BF16), st_bf, _NT, preferred_element_type=F32))
        kd = (k * jnp.exp(rem)).astype(BF16)
        new_state_t = state_t * jnp.exp(b_last) + lax.dot_general(v, kd, _TN, preferred_element_type=F32)
        ms = jnp.mean(o * o, axis=-1, keepdims=True)
        o_ref[0, sl, :] = (o * lax.rsqrt(ms + EPS) * nw_ref[...] * gate_ref[0, sl, :]).astype(o_ref.dtype)
        return new_state_t

    lax.fori_loop(0, n_chunks, chunk, jnp.zeros((dk, dk), F32))


def _hgrn2(q, gl, k, v, gate, nw):
    b, s, w = q.shape
    heads = w // HG_HEAD_DIM
    cs, pm = _hgrn_consts()
    tok = pl.BlockSpec((1, s, HG_HEAD_DIM), lambda bi, h: (bi, 0, h))
    return pl.pallas_call(
        functools.partial(_hgrn2_kernel, n_chunks=s // HG_CHUNK),
        grid=(b, heads),
        in_specs=[tok, tok, tok, tok, tok,
                  pl.BlockSpec((1, HG_HEAD_DIM), lambda bi, h: (0, h)),
                  pl.BlockSpec(cs.shape, lambda bi, h: (0, 0)),
                  pl.BlockSpec(pm.shape, lambda bi, h: (0, 0))],
        out_specs=tok,
        out_shape=jax.ShapeDtypeStruct((b, s, w), BF16),
        compiler_params=pltpu.CompilerParams(
            dimension_semantics=("arbitrary", "arbitrary"), vmem_limit_bytes=VMEM_LIMIT),
        name="hgrn2",
    )(q, gl, k, v, gate, nw, cs, pm)


def _out_route_kernel(x_ref, sbo_ref, hgo_ref, wo_ref, nw_ref, wr_ref, br_ref, su_ref,
                      h1_ref, m_ref, meta_ref, gates_ref, counts_ref, carry_ref, *, sbw):
    i = pl.program_id(0)

    @pl.when(i == 0)
    def _():
        carry_ref[...] = jnp.zeros_like(carry_ref)

    h1 = (x_ref[...]
          + jnp.dot(sbo_ref[...], wo_ref[0:sbw, :], preferred_element_type=F32)
          + jnp.dot(hgo_ref[...], wo_ref[sbw:, :], preferred_element_type=F32))
    h1_ref[...] = h1
    ms = jnp.mean(h1 * h1, axis=-1, keepdims=True)
    m = h1 * lax.rsqrt(ms + EPS) * nw_ref[...]
    m_ref[...] = m

    m3 = _split3(m)
    w3 = _split3(wr_ref[...])
    logits = br_ref[...]
    for wi, mi in ((0, 0), (0, 1), (1, 0), (0, 2), (1, 1), (2, 0)):
        logits = logits + lax.dot_general(w3[wi], m3[mi], _NT, preferred_element_type=F32)
    rid = lax.broadcasted_iota(jnp.int32, logits.shape, 0)
    neg = -jnp.inf
    big = jnp.int32(2 * LANES)

    def first_argmax(vals):
        vmax = jnp.max(vals, axis=0, keepdims=True)
        idx = jnp.min(jnp.where(vals == vmax, rid, big), axis=0, keepdims=True)
        return vmax, idx

    is_group = rid < N_GROUPS
    gmax, g_idx = first_argmax(jnp.where(is_group, logits, neg))
    gsum = jnp.sum(jnp.where(is_group, jnp.exp(logits - gmax), 0.0), axis=0, keepdims=True)
    g_prob = 1.0 / gsum
    lo_row = ROUTER_ROW0 + EXPERTS_PER_GROUP * g_idx
    el = jnp.where((rid >= lo_row) & (rid < lo_row + EXPERTS_PER_GROUP), logits, neg)
    v1, i1 = first_argmax(el)
    v2, i2 = first_argmax(jnp.where(rid == i1, neg, el))
    dd = jnp.exp(v2 - v1)
    p1 = 1.0 / (1.0 + dd)
    g1 = p1 * g_prob
    g2 = dd * p1 * g_prob

    hit1 = rid == i1
    hit2 = rid == i2
    onehot = jnp.where(hit1 | hit2, 1.0, 0.0)
    before_cnt = carry_ref[...] + jnp.dot(onehot.astype(BF16), su_ref[...], preferred_element_type=F32)
    r1 = jnp.sum(jnp.where(hit1, before_cnt, 0.0), axis=0, keepdims=True)
    r2 = jnp.sum(jnp.where(hit2, before_cnt, 0.0), axis=0, keepdims=True)
    carry_ref[...] = carry_ref[...] + jnp.sum(onehot, axis=1, keepdims=True)
    counts_ref[...] = carry_ref[...]

    meta_ref[...] = jnp.zeros_like(meta_ref)
    meta_ref[0:1, :] = i1 - ROUTER_ROW0
    meta_ref[1:2, :] = i2 - ROUTER_ROW0
    meta_ref[2:3, :] = r1.astype(jnp.int32)
    meta_ref[3:4, :] = r2.astype(jnp.int32)
    gt = jnp.where(rid == 0, g1, jnp.where(rid == 1, g2, 0.0))
    gates_ref[...] = gt.T


def _out_route(x2, sbo, hgo, w_out, nw, wr_t, br, *, tm):
    n, d = x2.shape
    sbw = sbo.shape[1]
    j = np.arange(tm)[:, None]
    t = np.arange(tm)[None, :]
    su = jnp.asarray((j < t).astype(np.float32), dtype=BF16)
    tok = lambda width: pl.BlockSpec((tm, width), lambda i: (i, 0))
    const = lambda shape: pl.BlockSpec(shape, lambda i: (0,) * len(shape))
    return pl.pallas_call(
        functools.partial(_out_route_kernel, sbw=sbw),
        grid=(n // tm,),
        in_specs=[tok(d), tok(sbw), tok(hgo.shape[1]), const(w_out.shape), const((1, d)),
                  const(wr_t.shape), const(br.shape), const(su.shape)],
        out_specs=[tok(d), tok(d), pl.BlockSpec((8, tm), lambda i: (0, i)), tok(LANES),
                   const((LANES, 1))],
        out_shape=[jax.ShapeDtypeStruct((n, d), F32), jax.ShapeDtypeStruct((n, d), F32),
                   jax.ShapeDtypeStruct((8, n), jnp.int32), jax.ShapeDtypeStruct((n, LANES), F32),
                   jax.ShapeDtypeStruct((LANES, 1), F32)],
        scratch_shapes=[pltpu.VMEM((LANES, 1), F32)],
        compiler_params=pltpu.CompilerParams(
            dimension_semantics=("arbitrary",), vmem_limit_bytes=VMEM_LIMIT),
        name="out_route",
    )(x2, sbo, hgo, w_out, nw, wr_t, br, su)


def _positions_kernel(seg_ref, meta_ref, pos_ref):
    e = meta_ref[0:TOP_K, :]
    start = jnp.zeros(e.shape, jnp.int32)
    for x in range(N_EXPERTS):
        start = jnp.where(e == x, seg_ref[x], start)
    pos_ref[...] = jnp.zeros_like(pos_ref)
    pos_ref[0:TOP_K, :] = start + meta_ref[TOP_K:2 * TOP_K, :]


def _positions(seg, meta):
    return pl.pallas_call(
        _positions_kernel,
        in_specs=[pl.BlockSpec(memory_space=pltpu.SMEM), pl.BlockSpec(memory_space=pltpu.VMEM)],
        out_specs=pl.BlockSpec(memory_space=pltpu.VMEM),
        out_shape=jax.ShapeDtypeStruct(meta.shape, jnp.int32),
        compiler_params=pltpu.CompilerParams(vmem_limit_bytes=VMEM_LIMIT),
        name="positions",
    )(seg, meta)


ROW_UNROLL = 8


def _dispatch_kernel(seg_ref, pos_ref, m_ref, xs_ref, zbuf, sem, zsem, *, tm, tmm, n_tiles):
    i = pl.program_id(0)

    def zero_copy(e):
        tail = pl.multiple_of(seg_ref[N_EXPERTS + e] - tmm, tmm)
        return pltpu.make_async_copy(zbuf, xs_ref.at[pl.ds(tail, tmm), :], zsem)

    @pl.when(i == 0)
    def _():
        zbuf[...] = jnp.zeros_like(zbuf)
        for e in range(N_EXPERTS):
            @pl.when(seg_ref[2 * N_EXPERTS + e] > 0)
            def _():
                zero_copy(e).start()
        for e in range(N_EXPERTS):
            @pl.when(seg_ref[2 * N_EXPERTS + e] > 0)
            def _():
                zero_copy(e).wait()

        def unused_copy(t):
            return pltpu.make_async_copy(zbuf, xs_ref.at[pl.ds(pl.multiple_of(t * tmm, tmm), tmm), :], zsem)

        def start_unused(t, _):
            unused_copy(t).start()
            return 0

        def wait_unused(t, _):
            unused_copy(t).wait()
            return 0

        lax.fori_loop(seg_ref[3 * N_EXPERTS], n_tiles, start_unused, 0)
        lax.fori_loop(seg_ref[3 * N_EXPERTS], n_tiles, wait_unused, 0)

    def start(g, _):
        for u in range(ROW_UNROLL):
            src = m_ref.at[g, pl.ds(u, 1), :]
            for k in range(TOP_K):
                p = pos_ref[k, g * ROW_UNROLL + u]
                pltpu.make_async_copy(src, xs_ref.at[pl.ds(p, 1), :], sem).start()
        return 0

    lax.fori_loop(0, tm // ROW_UNROLL, start, 0)
    for _ in range(TOP_K * tm // tmm):
        pltpu.make_async_copy(zbuf, xs_ref.at[pl.ds(0, tmm), :], sem).wait()


def _dispatch(seg, pos, m, n_rows, *, tm, tmm):
    n, d = m.shape
    assert tm % tmm == 0 and ROW_UNROLL == 8
    m = m.reshape(n // ROW_UNROLL, ROW_UNROLL, d)
    return pl.pallas_call(
        functools.partial(_dispatch_kernel, tm=tm, tmm=tmm, n_tiles=n_rows // tmm),
        grid=(n // tm,),
        in_specs=[pl.BlockSpec(memory_space=pltpu.SMEM),
                  pl.BlockSpec((8, tm), lambda i: (0, i), memory_space=pltpu.SMEM),
                  pl.BlockSpec((tm // ROW_UNROLL, ROW_UNROLL, d), lambda i: (i, 0, 0))],
        out_specs=pl.BlockSpec(memory_space=pl.ANY),
        out_shape=jax.ShapeDtypeStruct((n_rows, d), F32),
        scratch_shapes=[pltpu.VMEM((tmm, d), F32), pltpu.SemaphoreType.DMA(()),
                        pltpu.SemaphoreType.DMA(())],
        compiler_params=pltpu.CompilerParams(
            dimension_semantics=("arbitrary",), vmem_limit_bytes=VMEM_LIMIT),
        name="dispatch",
    )(seg, pos, m)


def _experts_kernel(te_ref, ts_ref, xs_ref, wg_ref, wu_ref, wd_ref, y_ref):
    t = pl.program_id(0)

    @pl.when(ts_ref[t] == t)
    def _():
        x = xs_ref[...].astype(BF16)
        hg = jnp.dot(x, wg_ref[0], preferred_element_type=F32)
        hu = jnp.dot(x, wu_ref[0], preferred_element_type=F32)
        act = (hg * _sigmoid(hg) * hu).astype(BF16)
        y_ref[...] = jnp.dot(act, wd_ref[0], preferred_element_type=F32)

    @pl.when(ts_ref[t] != t)
    def _():
        y_ref[...] = jnp.zeros_like(y_ref)


def _experts(tile_expert, tile_src, xs, wg, wu, wd, *, tmm):
    n_rows, d = xs.shape
    de = wg.shape[2]
    grid_spec = pltpu.PrefetchScalarGridSpec(
        num_scalar_prefetch=2,
        grid=(n_rows // tmm,),
        in_specs=[pl.BlockSpec((tmm, d), lambda t, te, ts: (ts[t], 0)),
                  pl.BlockSpec((1, d, de), lambda t, te, ts: (te[t], 0, 0)),
                  pl.BlockSpec((1, d, de), lambda t, te, ts: (te[t], 0, 0)),
                  pl.BlockSpec((1, de, d), lambda t, te, ts: (te[t], 0, 0))],
        out_specs=pl.BlockSpec((tmm, d), lambda t, te, ts: (t, 0)),
    )
    return pl.pallas_call(
        _experts_kernel,
        grid_spec=grid_spec,
        out_shape=jax.ShapeDtypeStruct((n_rows, d), F32),
        compiler_params=pltpu.CompilerParams(
            dimension_semantics=("arbitrary",), vmem_limit_bytes=VMEM_LIMIT),
        name="experts",
    )(tile_expert, tile_src, xs, wg, wu, wd)


def _combine_kernel(pos_ref, posn_ref, h1_ref, gates_ref, p_ref, pnw_ref, wpp_ref, wpg_ref, fnw_ref,
                    y_ref, o_ref, ybuf, sem, *, tm, n_steps):
    i = pl.program_id(0)

    def gather(p_ref_, slot):
        def start(g, _):
            for u in range(ROW_UNROLL):
                r = g * ROW_UNROLL + u
                for k in range(TOP_K):
                    pltpu.make_async_copy(y_ref.at[pl.ds(p_ref_[k, r], 1), :],
                                          ybuf.at[slot, k, pl.ds(r, 1), :], sem.at[slot]).start()
            return 0

        lax.fori_loop(0, tm // ROW_UNROLL, start, 0)

    slot = i % 2

    @pl.when(i == 0)
    def _():
        gather(pos_ref, 0)

    @pl.when(i + 1 < n_steps)
    def _():
        gather(posn_ref, 1 - slot)

    for k in range(TOP_K):
        pltpu.make_async_copy(y_ref.at[pl.ds(0, tm), :], ybuf.at[slot, k], sem.at[slot]).wait()

    gts = gates_ref[...]
    h2 = h1_ref[...] + gts[:, 0:1] * ybuf[slot, 0] + gts[:, 1:2] * ybuf[slot, 1]
    e = jnp.dot(p_ref[...].astype(BF16), wpp_ref[...], preferred_element_type=F32)
    ms = jnp.mean(h2 * h2, axis=-1, keepdims=True)
    hn = (h2 * lax.rsqrt(ms + EPS) * pnw_ref[...]).astype(BF16)
    gate = _sigmoid(jnp.dot(hn, wpg_ref[...], preferred_element_type=F32))
    h3 = h2 + gate * e
    ms3 = jnp.mean(h3 * h3, axis=-1, keepdims=True)
    o_ref[...] = h3 * lax.rsqrt(ms3 + EPS) * fnw_ref[...]


def _combine(pos, h1, gates, p2, pnw, wpp, wpg, fnw, y, *, tm):
    n, d = h1.shape
    n_steps = n // tm
    tok = lambda width: pl.BlockSpec((tm, width), lambda i: (i, 0))
    const = lambda shape: pl.BlockSpec(shape, lambda i: (0,) * len(shape))
    return pl.pallas_call(
        functools.partial(_combine_kernel, tm=tm, n_steps=n_steps),
        grid=(n_steps,),
        in_specs=[pl.BlockSpec((8, tm), lambda i: (0, i), memory_space=pltpu.SMEM),
                  pl.BlockSpec((8, tm), lambda i: (0, jnp.minimum(i + 1, n_steps - 1)),
                               memory_space=pltpu.SMEM),
                  tok(d), tok(LANES), tok(p2.shape[1]), const((1, d)), const(wpp.shape),
                  const(wpg.shape), const((1, d)),
                  pl.BlockSpec(memory_space=pl.ANY)],
        out_specs=tok(d),
        out_shape=jax.ShapeDtypeStruct((n, d), F32),
        scratch_shapes=[pltpu.VMEM((2, TOP_K, tm, d), F32), pltpu.SemaphoreType.DMA((2,))],
        compiler_params=pltpu.CompilerParams(
            dimension_semantics=("arbitrary",), vmem_limit_bytes=VMEM_LIMIT),
        name="combine",
    )(pos, pos, h1, gates, p2, pnw, wpp, wpg, fnw, y)


def kernel(x, p, attn_norm_w, w_in, sb_norm_w, hg_lower_bounds, hg_norm_w, w_out, ffn_norm_w,
           w_group_router, b_group_router, w_expert_router, b_expert_router, w_exp_gate, w_exp_up,
           w_exp_down, ple_norm_w, w_ple_proj, w_ple_gate, final_norm_w):
    b, s, d = x.shape
    depth = w_in.shape[0]
    assert depth == 1, "single-layer trunk"
    sbw = sb_norm_w.shape[1]
    hgw = hg_norm_w.shape[1]
    n = b * s
    tm_proj = min(512, s)
    tm_route = min(512, n)
    tm_disp = min(512, n)
    tm_comb = min(256, n)
    tmm = min(256, n)

    wi = w_in[0]
    w_main = jnp.concatenate([wi[:, 0:sbw], wi[:, 2 * sbw:]], axis=1).astype(BF16)
    w_kt = wi[:, sbw:2 * sbw].T.astype(BF16)
    gap = ROUTER_ROW0 - N_GROUPS
    tail = LANES - ROUTER_ROW0 - N_EXPERTS
    wr_t = jnp.concatenate([w_group_router[0].T, jnp.zeros((gap, d), F32), w_expert_router[0].T,
                            jnp.zeros((tail, d), F32)], axis=0)
    br = jnp.concatenate([b_group_router[0], jnp.zeros((gap,), F32), b_expert_router[0],
                          jnp.zeros((tail,), F32)])[:, None]

    sbq, sbkt, sbve, sbvo, hq, hgl, hk, hv, hgate = _in_proj(
        x, attn_norm_w[0][None, :], w_main, w_kt, hg_lower_bounds, tm=tm_proj)
    sbo = _sb_attn(sbq, sbkt, sbve, sbvo, sb_norm_w, chains=SB_CHAINS, qb=SB_QUERY_BLOCKS)
    hgo = _hgrn2(hq, hgl, hk, hv, hgate, hg_norm_w)

    h1, m, meta, gates, counts = _out_route(
        x.reshape(n, d), sbo.reshape(n, sbw), hgo.reshape(n, hgw), w_out[0].astype(BF16),
        ffn_norm_w[0][None, :], wr_t, br, tm=tm_route)

    cnt = counts[ROUTER_ROW0:ROUTER_ROW0 + N_EXPERTS, 0].astype(jnp.int32)
    padded = ((cnt + tmm - 1) // tmm) * tmm
    ends = jnp.cumsum(padded)
    starts = ends - padded
    n_rows = n * TOP_K + N_EXPERTS * tmm
    n_tiles = n_rows // tmm
    last_tile = ends[-1] // tmm - 1
    tile_src = jnp.minimum(jnp.arange(n_tiles, dtype=jnp.int32), last_tile)
    tile_expert = jnp.sum((ends[None, :] <= (tile_src * tmm)[:, None]).astype(jnp.int32), axis=1)
    seg = jnp.concatenate([starts, ends, padded, (last_tile + 1)[None],
                           jnp.zeros((LANES - 3 * N_EXPERTS - 1,), jnp.int32)])

    pos = _positions(seg, meta)
    xs = _dispatch(seg, pos, m, n_rows, tm=tm_disp, tmm=tmm)
    y = _experts(tile_expert, tile_src, xs, w_exp_gate[0].astype(BF16), w_exp_up[0].astype(BF16),
                 w_exp_down[0].astype(BF16), tmm=tmm)
    out = _combine(pos, h1, gates, p[0].reshape(n, -1), ple_norm_w[0][None, :],
                   w_ple_proj[0].astype(BF16), w_ple_gate[0].astype(BF16), final_norm_w[None, :],
                   y, tm=tm_comb)
    return out.reshape(b, s, d)
```

```python
import functools

import numpy as np
import jax
import jax.numpy as jnp
from jax import lax
from jax.experimental import pallas as pl
from jax.experimental.pallas import tpu as pltpu

F32 = jnp.float32
BF16 = jnp.bfloat16
EPS = 1e-6

SB_HEADS = 8
SB_HEAD_DIM = 64
HG_HEAD_DIM = 128
HG_CHUNK = 64
HG_CHUNKS_PER_ITER = 8
N_GROUPS = 4
EXPERTS_PER_GROUP = 8
N_EXPERTS = N_GROUPS * EXPERTS_PER_GROUP
TOP_K = 2
LANES = 128
ROUTER_ROW0 = 8
SB_BLOCK = 128
SB_QUERY_BLOCKS = 2
SB_CHAINS = 4
SB_LOG2_FLOOR = -152.0
LOG2E = 1.4426950408889634
VMEM_LIMIT = 56 * 1024 * 1024

_NT = (((1,), (1,)), ((), ()))
_TN = (((0,), (0,)), ((), ()))


def _sigmoid(x):
    return 1.0 / (1.0 + jnp.exp(-x))


def _split2(x):
    hi = x.astype(BF16)
    lo = (x - hi.astype(F32)).astype(BF16)
    return hi, lo


def _split3(x):
    hi = x.astype(BF16)
    r = x - hi.astype(F32)
    mid = r.astype(BF16)
    lo = (r - mid.astype(F32)).astype(BF16)
    return hi, mid, lo


def _in_proj_kernel(x_ref, nw_ref, w_ref, wkt_ref, lbp_ref,
                    sbq_ref, sbkt_ref, sbve_ref, sbvo_ref, hq_ref, hgl_ref, hk_ref, hv_ref, hgate_ref,
                    *, sbw, hgw):
    x = x_ref[0]
    ms = jnp.mean(x * x, axis=-1, keepdims=True)
    a = (x * lax.rsqrt(ms + EPS) * nw_ref[...]).astype(BF16)

    def seg(lo, width):
        return jnp.dot(a, w_ref[:, lo:lo + width], preferred_element_type=F32)

    sbq_ref[0] = (seg(0, sbw) * (SB_HEAD_DIM ** -0.5 * LOG2E)).astype(BF16)
    sbkt_ref[0] = lax.dot_general(wkt_ref[...], a, _NT, preferred_element_type=F32).astype(BF16)
    v = seg(sbw, sbw)
    even_head = (lax.broadcasted_iota(jnp.int32, v.shape, 1) & SB_HEAD_DIM) == 0
    sbve_ref[0] = jnp.where(even_head, v, 0.0).astype(BF16)
    sbvo_ref[0] = jnp.where(even_head, 0.0, v).astype(BF16)
    c = 2 * sbw
    q = seg(c, hgw)
    hq_ref[0] = q * _sigmoid(q)
    p0 = lbp_ref[0:1, :]
    p1 = lbp_ref[1:2, :]
    pm = jnp.maximum(p0, p1)
    e0 = jnp.exp(p0 - pm)
    e1 = jnp.exp(p1 - pm)
    lb = e0 / (e0 + e1)
    fz = seg(c + hgw, hgw)
    sg = _sigmoid(fz)
    f = lb + (1.0 - lb) * sg
    hgl_ref[0] = jnp.log(f)
    hk_ref[0] = 1.0 - f
    hv_ref[0] = seg(c + 2 * hgw, hgw).astype(BF16)
    g = seg(c + 3 * hgw, hgw)
    hgate_ref[0] = g * _sigmoid(g)


def _in_proj(x, nw, w_main, w_kt, lbp, *, tm):
    b, s, d = x.shape
    sbw = w_kt.shape[0]
    hgw = (w_main.shape[1] - 2 * sbw) // 4
    tok = lambda width: pl.BlockSpec((1, tm, width), lambda bi, i: (bi, i, 0))
    const = lambda shape: pl.BlockSpec(shape, lambda bi, i: (0,) * len(shape))
    out_shape = [
        jax.ShapeDtypeStruct((b, s, sbw), BF16),
        jax.ShapeDtypeStruct((b, sbw, s), BF16),
        jax.ShapeDtypeStruct((b, s, sbw), BF16),
        jax.ShapeDtypeStruct((b, s, sbw), BF16),
        jax.ShapeDtypeStruct((b, s, hgw), F32),
        jax.ShapeDtypeStruct((b, s, hgw), F32),
        jax.ShapeDtypeStruct((b, s, hgw), F32),
        jax.ShapeDtypeStruct((b, s, hgw), BF16),
        jax.ShapeDtypeStruct((b, s, hgw), F32),
    ]
    out_specs = [tok(sbw), pl.BlockSpec((1, sbw, tm), lambda bi, i: (bi, 0, i)), tok(sbw), tok(sbw),
                 tok(hgw), tok(hgw), tok(hgw), tok(hgw), tok(hgw)]
    return pl.pallas_call(
        functools.partial(_in_proj_kernel, sbw=sbw, hgw=hgw),
        grid=(b, s // tm),
        in_specs=[tok(d), const((1, d)), const(w_main.shape), const(w_kt.shape), const(lbp.shape)],
        out_specs=out_specs,
        out_shape=out_shape,
        compiler_params=pltpu.CompilerParams(
            dimension_semantics=("arbitrary", "arbitrary"), vmem_limit_bytes=VMEM_LIMIT),
        name="in_proj",
    )(x, nw, w_main, w_kt, lbp)


def _sb_cumsum_weights():
    j = np.arange(SB_BLOCK)[:, None]
    s = np.arange(SB_BLOCK)[None, :]
    half = np.concatenate([(j > s).astype(np.float32), np.ones((SB_BLOCK, SB_BLOCK), np.float32)], axis=1)
    return jnp.asarray(np.concatenate([half, half], axis=0), dtype=BF16)


def _sb_attn_kernel(q_ref, kt_ref, ve_ref, vo_ref, nw_ref, cw_ref, o_ref, *, n_groups, chains, qb):
    blk = SB_BLOCK
    dh = SB_HEAD_DIM
    rq = qb * blk
    rr = 2 * rq
    first = lax.broadcasted_iota(jnp.int32, (rq, LANES), 1) < dh
    rowpos = lax.broadcasted_iota(jnp.int32, (rr, blk), 0) & (rq - 1)
    colpos = lax.broadcasted_iota(jnp.int32, (rr, blk), 1)
    cw = cw_ref[...]

    def steps(qss, js, q0s, v_scales, carries, accs, masked):
        k0s = [pl.multiple_of(j * blk, blk) for j in js]
        zs = [jnp.dot(qs, kt_ref[0, :, pl.ds(k0, blk)], preferred_element_type=F32)
              for qs, k0 in zip(qss, k0s)]
        lss, hls, befores = [], [], []
        for z, k0, q0 in zip(zs, k0s, q0s):
            sp = jnp.log2(1.0 + jnp.exp2(-jnp.abs(z)))
            ls = jnp.minimum(z, 0.0) - sp
            lk = ls - z
            before = None
            if masked:
                before = (k0 + colpos) < (q0 + rowpos)
                lk = jnp.where(before, lk, 0.0)
            hi, lo = _split2(lk)
            lss.append(ls)
            hls.append(jnp.concatenate([hi, lo], axis=1))
            befores.append(before)
        css = [jnp.dot(hl, cw, preferred_element_type=F32) for hl in hls]
        abs_ = []
        for ls, cs, carry, before in zip(lss, css, carries, befores):
            a = jnp.exp2(ls + cs[:, :blk] + carry)
            if masked:
                a = jnp.where(before, a, 0.0)
            ab = a.astype(BF16)
            abs_.append(jnp.concatenate([ab[:rq], ab[rq:]], axis=1))
        new_accs = []
        for ab, k0, v_scale, acc in zip(abs_, k0s, v_scales, accs):
            vst = jnp.concatenate([ve_ref[0, pl.ds(k0, blk), :], vo_ref[0, pl.ds(k0, blk), :]], axis=0)
            if v_scale is not None:
                vst = vst * v_scale
            new_accs.append(acc + jnp.dot(ab, vst, preferred_element_type=F32))
        new_carries = [carry + cs[:, blk:] for carry, cs in zip(carries, css)]
        return new_carries, new_accs

    def highest(carries):
        m = carries[0]
        for c in carries[1:]:
            m = jnp.maximum(m, c)
        return jnp.max(m)

    def group(gi, _):
        sis = [gi * chains + c for c in range(chains)]
        q0s = [pl.multiple_of(si * rq, rq) for si in sis]
        qss = []
        for q0 in q0s:
            q2 = q_ref[0, pl.ds(q0, rq), :].astype(F32)
            qss.append(jnp.concatenate([jnp.where(first, q2, 0.0), jnp.where(first, 0.0, q2)],
                                       axis=0).astype(BF16))
        carries = [jnp.zeros((rr, blk), F32)] * chains
        accs = [jnp.zeros((rq, LANES), F32)] * chains
        for i in range(qb):
            carries, accs = steps(qss, [si * qb + (qb - 1 - i) for si in sis], q0s, [None] * chains,
                                  carries, accs, True)

        def cond(st):
            n, top = st[0], st[1]
            return (n <= sis[-1] * qb) & (top > SB_LOG2_FLOOR)

        def body(st):
            n = st[0]
            js, v_scales = [], []
            for c, si in enumerate(sis):
                d = si * qb - n
                if c < chains - 1:
                    v_scales.append(jnp.where(d >= 0, 1.0, 0.0).astype(BF16))
                    d = jnp.maximum(d, 0)
                else:
                    v_scales.append(None)
                js.append(d)
            cs_out, as_out = steps(qss, js, q0s, v_scales, list(st[2]), list(st[3]), False)
            return n + 1, highest(cs_out), tuple(cs_out), tuple(as_out)

        st = lax.while_loop(cond, body, (jnp.int32(1), highest(carries), tuple(carries), tuple(accs)))
        for c, si in enumerate(sis):
            acc = st[3][c]
            q0 = pl.multiple_of(si * rq, rq)
            sq = acc * acc
            s0 = jnp.sum(jnp.where(first, sq, 0.0), axis=-1, keepdims=True)
            s1 = jnp.sum(jnp.where(first, 0.0, sq), axis=-1, keepdims=True)
            ms = jnp.where(first, s0, s1) * (1.0 / dh)
            o_ref[0, pl.ds(q0, rq), :] = (acc * lax.rsqrt(ms + EPS) * nw_ref[...]).astype(o_ref.dtype)
        return 0

    lax.fori_loop(0, n_groups, group, 0)


def _sb_attn(q, kt, v_even, v_odd, nw, *, chains, qb):
    b, s, w = q.shape
    n_pairs = w // LANES
    cw = _sb_cumsum_weights()
    tok = pl.BlockSpec((1, s, LANES), lambda bi, hp: (bi, 0, hp))
    return pl.pallas_call(
        functools.partial(_sb_attn_kernel, n_groups=s // (chains * qb * SB_BLOCK), chains=chains, qb=qb),
        grid=(b, n_pairs),
        in_specs=[
            tok,
            pl.BlockSpec((1, LANES, s), lambda bi, hp: (bi, hp, 0)),
            tok, tok,
            pl.BlockSpec((1, LANES), lambda bi, hp: (0, hp)),
            pl.BlockSpec(cw.shape, lambda bi, hp: (0, 0)),
        ],
        out_specs=tok,
        out_shape=jax.ShapeDtypeStruct((b, s, w), BF16),
        compiler_params=pltpu.CompilerParams(
            dimension_semantics=("arbitrary", "arbitrary"), vmem_limit_bytes=VMEM_LIMIT),
        name="sb_attn",
    )(q, kt, v_even, v_odd, nw, cw)


def _hgrn_levels():
    hs = []
    h = HG_CHUNK // 2
    while h >= 1:
        hs.append(h)
        h //= 2
    return hs


def _hgrn_consts():
    n = HG_CHUNK
    t = np.arange(n)[:, None]
    j = np.arange(n)[None, :]
    mats = [j <= t]
    masks = []
    for h in _hgrn_levels():
        mid = (t // (2 * h)) * (2 * h) + h - 1
        upper = (t % (2 * h)) >= h
        if h > 1:
            mats.append(np.where(upper, (j > mid) & (j <= t), (j > t) & (j <= mid)))
        masks.append(((t // (2 * h)) == (j // (2 * h))) & upper & ((j % (2 * h)) < h))
    masks.append(t == j)
    sums = np.concatenate(mats, axis=0).astype(np.float32)
    pm = np.concatenate(masks, axis=0).astype(np.float32)
    return jnp.asarray(sums, dtype=BF16), jnp.asarray(pm, dtype=F32)


def _hgrn2_kernel(q_ref, gl_ref, k_ref, v_ref, gate_ref, nw_ref, cs_ref, pm_ref, o_ref,
                  *, n_iters, chunks_per_iter):
    n = HG_CHUNK
    dk = HG_HEAD_DIM
    heads = 2
    levels = _hgrn_levels()
    rows = lax.broadcasted_iota(jnp.int32, (n, heads * dk), 0)
    odd = (rows & 1) != 0

    def body(it, states):
        base = it * (chunks_per_iter * n)
        cs = cs_ref[...]
        half = cs.shape[0] // 2
        units = []
        for c in range(chunks_per_iter):
            sl = pl.ds(pl.multiple_of(base + c * n, n), n)
            g = gl_ref[0, sl, :]
            g_hi, g_lo = _split2(g)
            d = jnp.concatenate(
                [jnp.dot(cs[:half], g_hi, preferred_element_type=F32)
                 + jnp.dot(cs[:half], g_lo, preferred_element_type=F32),
                 jnp.dot(cs[half:], g_hi, preferred_element_type=F32)
                 + jnp.dot(cs[half:], g_lo, preferred_element_type=F32)], axis=0)
            units.append(dict(sl=sl, g=g, d=d, q=q_ref[0, sl, :], k=k_ref[0, sl, :], v=v_ref[0, sl, :]))
        for un in units:
            q, k, g, d = un["q"], un["k"], un["g"], un["d"]
            bc = d[0:n]
            b_last = bc[n - 1:n, :]
            ws = []
            for li, h in enumerate(levels):
                dl = d[(1 + li) * n:(2 + li) * n] if h > 1 else jnp.where(odd, g, 0.0)
                upper = (rows & h) != 0
                ws.append((jnp.where(upper, q, k) * jnp.exp(dl)).astype(BF16))
            un["ws"] = ws
            un["qk"] = q * k
            un["qd"] = (q * jnp.exp(bc)).astype(BF16)
            un["kd"] = (k * jnp.exp(b_last - bc)).astype(BF16)
            un["decay"] = jnp.exp(b_last)
        hslices = [slice(hh * dk, (hh + 1) * dk) for hh in range(heads)]
        for un in units:
            un["gram"] = [[lax.dot_general(w[:, ls], w[:, ls], _NT, preferred_element_type=F32)
                           for w in un["ws"]] for ls in hslices]
            un["kv"] = [lax.dot_general(un["v"][:, ls], un["kd"][:, ls], _TN, preferred_element_type=F32)
                        for ls in hslices]
        for un in units:
            un["st"] = states
            states = tuple(states[hh] * un["decay"][:, ls] + un["kv"][hh] for hh, ls in enumerate(hslices))
            ps = []
            for hh, ls in enumerate(hslices):
                p = pm_ref[len(levels) * n:(len(levels) + 1) * n, :] * jnp.sum(un["qk"][:, ls], axis=1, keepdims=True)
                for li in range(len(levels)):
                    p = p + pm_ref[li * n:(li + 1) * n, :] * un["gram"][hh][li]
                ps.append(p.astype(BF16))
            un["p"] = ps
        for un in units:
            un["o"] = [jnp.dot(un["p"][hh], un["v"][:, ls], preferred_element_type=F32)
                       + lax.dot_general(un["qd"][:, ls], un["st"][hh].astype(BF16), _NT,
                                         preferred_element_type=F32)
                       for hh, ls in enumerate(hslices)]
        for un in units:
            outs = []
            for o in un["o"]:
                ms = jnp.mean(o * o, axis=-1, keepdims=True)
                outs.append(o * lax.rsqrt(ms + EPS))
            o2 = jnp.concatenate(outs, axis=1)
            o_ref[0, un["sl"], :] = (o2 * nw_ref[...] * gate_ref[0, un["sl"], :]).astype(o_ref.dtype)
        return states

    zero = jnp.zeros((dk, dk), F32)
    lax.fori_loop(0, n_iters, body, (zero, zero))


def _hgrn2(q, gl, k, v, gate, nw, *, chunks_per_iter):
    b, s, w = q.shape
    pair = 2 * HG_HEAD_DIM
    heads = w // pair
    cs, pm = _hgrn_consts()
    tok = pl.BlockSpec((1, s, pair), lambda bi, h: (bi, 0, h))
    return pl.pallas_call(
        functools.partial(_hgrn2_kernel, n_iters=s // (HG_CHUNK * chunks_per_iter),
                          chunks_per_iter=chunks_per_iter),
        grid=(b, heads),
        in_specs=[tok, tok, tok, tok, tok,
                  pl.BlockSpec((1, pair), lambda bi, h: (0, h)),
                  pl.BlockSpec(cs.shape, lambda bi, h: (0, 0)),
                  pl.BlockSpec(pm.shape, lambda bi, h: (0, 0))],
        out_specs=tok,
        out_shape=jax.ShapeDtypeStruct((b, s, w), BF16),
        compiler_params=pltpu.CompilerParams(
            dimension_semantics=("arbitrary", "arbitrary"), vmem_limit_bytes=VMEM_LIMIT),
        name="hgrn2",
    )(q, gl, k, v, gate, nw, cs, pm)


def _out_route_kernel(x_ref, sbo_ref, hgo_ref, wo_ref, nw_ref, wr_ref, br_ref, su_ref,
                      h1_ref, m_ref, meta_ref, gates_ref, counts_ref, carry_ref, *, sbw):
    i = pl.program_id(0)

    @pl.when(i == 0)
    def _():
        carry_ref[...] = jnp.zeros_like(carry_ref)

    h1 = (x_ref[...]
          + jnp.dot(sbo_ref[...], wo_ref[0:sbw, :], preferred_element_type=F32)
          + jnp.dot(hgo_ref[...], wo_ref[sbw:, :], preferred_element_type=F32))
    h1_ref[...] = h1
    ms = jnp.mean(h1 * h1, axis=-1, keepdims=True)
    m = h1 * lax.rsqrt(ms + EPS) * nw_ref[...]
    m_ref[...] = m

    m3 = _split3(m)
    w3 = _split3(wr_ref[...])
    logits = br_ref[...]
    for wi, mi in ((0, 0), (0, 1), (1, 0), (0, 2), (1, 1), (2, 0)):
        logits = logits + lax.dot_general(w3[wi], m3[mi], _NT, preferred_element_type=F32)
    rid = lax.broadcasted_iota(jnp.int32, logits.shape, 0)
    neg = -jnp.inf
    big = jnp.int32(2 * LANES)

    def first_argmax(vals):
        vmax = jnp.max(vals, axis=0, keepdims=True)
        idx = jnp.min(jnp.where(vals == vmax, rid, big), axis=0, keepdims=True)
        return vmax, idx

    is_group = rid < N_GROUPS
    gmax, g_idx = first_argmax(jnp.where(is_group, logits, neg))
    gsum = jnp.sum(jnp.where(is_group, jnp.exp(logits - gmax), 0.0), axis=0, keepdims=True)
    g_prob = 1.0 / gsum
    lo_row = ROUTER_ROW0 + EXPERTS_PER_GROUP * g_idx
    el = jnp.where((rid >= lo_row) & (rid < lo_row + EXPERTS_PER_GROUP), logits, neg)
    v1, i1 = first_argmax(el)
    v2, i2 = first_argmax(jnp.where(rid == i1, neg, el))
    dd = jnp.exp(v2 - v1)
    p1 = 1.0 / (1.0 + dd)
    g1 = p1 * g_prob
    g2 = dd * p1 * g_prob

    hit1 = rid == i1
    hit2 = rid == i2
    onehot = jnp.where(hit1 | hit2, 1.0, 0.0)
    before_cnt = carry_ref[...] + jnp.dot(onehot.astype(BF16), su_ref[...], preferred_element_type=F32)
    r1 = jnp.sum(jnp.where(hit1, before_cnt, 0.0), axis=0, keepdims=True)
    r2 = jnp.sum(jnp.where(hit2, before_cnt, 0.0), axis=0, keepdims=True)
    carry_ref[...] = carry_ref[...] + jnp.sum(onehot, axis=1, keepdims=True)
    counts_ref[...] = carry_ref[...]

    meta_ref[...] = jnp.zeros_like(meta_ref)
    meta_ref[0:1, :] = i1 - ROUTER_ROW0
    meta_ref[1:2, :] = i2 - ROUTER_ROW0
    meta_ref[2:3, :] = r1.astype(jnp.int32)
    meta_ref[3:4, :] = r2.astype(jnp.int32)
    gt = jnp.where(rid == 0, g1, jnp.where(rid == 1, g2, 0.0))
    gates_ref[...] = gt.T


def _out_route(x2, sbo, hgo, w_out, nw, wr_t, br, *, tm):
    n, d = x2.shape
    sbw = sbo.shape[1]
    j = np.arange(tm)[:, None]
    t = np.arange(tm)[None, :]
    su = jnp.asarray((j < t).astype(np.float32), dtype=BF16)
    tok = lambda width: pl.BlockSpec((tm, width), lambda i: (i, 0))
    const = lambda shape: pl.BlockSpec(shape, lambda i: (0,) * len(shape))
    return pl.pallas_call(
        functools.partial(_out_route_kernel, sbw=sbw),
        grid=(n // tm,),
        in_specs=[tok(d), tok(sbw), tok(hgo.shape[1]), const(w_out.shape), const((1, d)),
                  const(wr_t.shape), const(br.shape), const(su.shape)],
        out_specs=[tok(d), tok(d), pl.BlockSpec((8, tm), lambda i: (0, i)), tok(LANES),
                   const((LANES, 1))],
        out_shape=[jax.ShapeDtypeStruct((n, d), F32), jax.ShapeDtypeStruct((n, d), F32),
                   jax.ShapeDtypeStruct((8, n), jnp.int32), jax.ShapeDtypeStruct((n, LANES), F32),
                   jax.ShapeDtypeStruct((LANES, 1), F32)],
        scratch_shapes=[pltpu.VMEM((LANES, 1), F32)],
        compiler_params=pltpu.CompilerParams(
            dimension_semantics=("arbitrary",), vmem_limit_bytes=VMEM_LIMIT),
        name="out_route",
    )(x2, sbo, hgo, w_out, nw, wr_t, br, su)


def _positions_kernel(seg_ref, meta_ref, pos_ref):
    e = meta_ref[0:TOP_K, :]
    start = jnp.zeros(e.shape, jnp.int32)
    for x in range(N_EXPERTS):
        start = jnp.where(e == x, seg_ref[x], start)
    pos_ref[...] = jnp.zeros_like(pos_ref)
    pos_ref[0:TOP_K, :] = start + meta_ref[TOP_K:2 * TOP_K, :]


def _positions(seg, meta):
    return pl.pallas_call(
        _positions_kernel,
        in_specs=[pl.BlockSpec(memory_space=pltpu.SMEM), pl.BlockSpec(memory_space=pltpu.VMEM)],
        out_specs=pl.BlockSpec(memory_space=pltpu.VMEM),
        out_shape=jax.ShapeDtypeStruct(meta.shape, jnp.int32),
        compiler_params=pltpu.CompilerParams(vmem_limit_bytes=VMEM_LIMIT),
        name="positions",
    )(seg, meta)


ROW_UNROLL = 8


def _dispatch_kernel(seg_ref, pos_ref, m_ref, xs_ref, zbuf, sem, zsem, *, tm, tmm, n_tiles):
    i = pl.program_id(0)

    def zero_copy(e):
        tail = pl.multiple_of(seg_ref[N_EXPERTS + e] - tmm, tmm)
        return pltpu.make_async_copy(zbuf, xs_ref.at[pl.ds(tail, tmm), :], zsem)

    @pl.when(i == 0)
    def _():
        zbuf[...] = jnp.zeros_like(zbuf)
        for e in range(N_EXPERTS):
            @pl.when(seg_ref[2 * N_EXPERTS + e] > 0)
            def _():
                zero_copy(e).start()
        for e in range(N_EXPERTS):
            @pl.when(seg_ref[2 * N_EXPERTS + e] > 0)
            def _():
                zero_copy(e).wait()

        def unused_copy(t):
            return pltpu.make_async_copy(zbuf, xs_ref.at[pl.ds(pl.multiple_of(t * tmm, tmm), tmm), :], zsem)

        def start_unused(t, _):
            unused_copy(t).start()
            return 0

        def wait_unused(t, _):
            unused_copy(t).wait()
            return 0

        lax.fori_loop(seg_ref[3 * N_EXPERTS], n_tiles, start_unused, 0)
        lax.fori_loop(seg_ref[3 * N_EXPERTS], n_tiles, wait_unused, 0)

    def start(g, _):
        for u in range(ROW_UNROLL):
            src = m_ref.at[g, pl.ds(u, 1), :]
            for k in range(TOP_K):
                p = pos_ref[k, g * ROW_UNROLL + u]
                pltpu.make_async_copy(src, xs_ref.at[pl.ds(p, 1), :], sem).start()
        return 0

    lax.fori_loop(0, tm // ROW_UNROLL, start, 0)
    for _ in range(TOP_K * tm // tmm):
        pltpu.make_async_copy(zbuf, xs_ref.at[pl.ds(0, tmm), :], sem).wait()


def _dispatch(seg, pos, m, n_rows, *, tm, tmm):
    n, d = m.shape
    assert tm % tmm == 0 and ROW_UNROLL == 8
    m = m.reshape(n // ROW_UNROLL, ROW_UNROLL, d)
    return pl.pallas_call(
        functools.partial(_dispatch_kernel, tm=tm, tmm=tmm, n_tiles=n_rows // tmm),
        grid=(n // tm,),
        in_specs=[pl.BlockSpec(memory_space=pltpu.SMEM),
                  pl.BlockSpec((8, tm), lambda i: (0, i), memory_space=pltpu.SMEM),
                  pl.BlockSpec((tm // ROW_UNROLL, ROW_UNROLL, d), lambda i: (i, 0, 0))],
        out_specs=pl.BlockSpec(memory_space=pl.ANY),
        out_shape=jax.ShapeDtypeStruct((n_rows, d), F32),
        scratch_shapes=[pltpu.VMEM((tmm, d), F32), pltpu.SemaphoreType.DMA(()),
                        pltpu.SemaphoreType.DMA(())],
        compiler_params=pltpu.CompilerParams(
            dimension_semantics=("arbitrary",), vmem_limit_bytes=VMEM_LIMIT),
        name="dispatch",
    )(seg, pos, m)


def _experts_kernel(te_ref, ts_ref, xs_ref, wg_ref, wu_ref, wd_ref, y_ref):
    t = pl.program_id(0)

    @pl.when(ts_ref[t] == t)
    def _():
        x = xs_ref[...].astype(BF16)
        hg = jnp.dot(x, wg_ref[0], preferred_element_type=F32)
        hu = jnp.dot(x, wu_ref[0], preferred_element_type=F32)
        act = (hg * _sigmoid(hg) * hu).astype(BF16)
        y_ref[...] = jnp.dot(act, wd_ref[0], preferred_element_type=F32)

    @pl.when(ts_ref[t] != t)
    def _():
        y_ref[...] = jnp.zeros_like(y_ref)


def _experts(tile_expert, tile_src, xs, wg, wu, wd, *, tmm):
    n_rows, d = xs.shape
    de = wg.shape[2]
    grid_spec = pltpu.PrefetchScalarGridSpec(
        num_scalar_prefetch=2,
        grid=(n_rows // tmm,),
        in_specs=[pl.BlockSpec((tmm, d), lambda t, te, ts: (ts[t], 0)),
                  pl.BlockSpec((1, d, de), lambda t, te, ts: (te[t], 0, 0)),
                  pl.BlockSpec((1, d, de), lambda t, te, ts: (te[t], 0, 0)),
                  pl.BlockSpec((1, de, d), lambda t, te, ts: (te[t], 0, 0))],
        out_specs=pl.BlockSpec((tmm, d), lambda t, te, ts: (t, 0)),
    )
    return pl.pallas_call(
        _experts_kernel,
        grid_spec=grid_spec,
        out_shape=jax.ShapeDtypeStruct((n_rows, d), F32),
        compiler_params=pltpu.CompilerParams(
            dimension_semantics=("arbitrary",), vmem_limit_bytes=VMEM_LIMIT),
        name="experts",
    )(tile_expert, tile_src, xs, wg, wu, wd)


def _combine_kernel(pos_ref, posn_ref, h1_ref, gates_ref, p_ref, pnw_ref, wpp_ref, wpg_ref, fnw_ref,
                    y_ref, o_ref, ybuf, sem, *, tm, n_steps):
    i = pl.program_id(0)

    def gather(p_ref_, slot):
        def start(g, _):
            for u in range(ROW_UNROLL):
                r = g * ROW_UNROLL + u
                for k in range(TOP_K):
                    pltpu.make_async_copy(y_ref.at[pl.ds(p_ref_[k, r], 1), :],
                                          ybuf.at[slot, k, pl.ds(r, 1), :], sem.at[slot]).start()
            return 0

        lax.fori_loop(0, tm // ROW_UNROLL, start, 0)

    slot = i % 2

    @pl.when(i == 0)
    def _():
        gather(pos_ref, 0)

    @pl.when(i + 1 < n_steps)
    def _():
        gather(posn_ref, 1 - slot)

    for k in range(TOP_K):
        pltpu.make_async_copy(y_ref.at[pl.ds(0, tm), :], ybuf.at[slot, k], sem.at[slot]).wait()

    gts = gates_ref[...]
    h2 = h1_ref[...] + gts[:, 0:1] * ybuf[slot, 0] + gts[:, 1:2] * ybuf[slot, 1]
    e = jnp.dot(p_ref[...].astype(BF16), wpp_ref[...], preferred_element_type=F32)
    ms = jnp.mean(h2 * h2, axis=-1, keepdims=True)
    hn = (h2 * lax.rsqrt(ms + EPS) * pnw_ref[...]).astype(BF16)
    gate = _sigmoid(jnp.dot(hn, wpg_ref[...], preferred_element_type=F32))
    h3 = h2 + gate * e
    ms3 = jnp.mean(h3 * h3, axis=-1, keepdims=True)
    o_ref[...] = h3 * lax.rsqrt(ms3 + EPS) * fnw_ref[...]


def _combine(pos, h1, gates, p2, pnw, wpp, wpg, fnw, y, *, tm):
    n, d = h1.shape
    n_steps = n // tm
    tok = lambda width: pl.BlockSpec((tm, width), lambda i: (i, 0))
    const = lambda shape: pl.BlockSpec(shape, lambda i: (0,) * len(shape))
    return pl.pallas_call(
        functools.partial(_combine_kernel, tm=tm, n_steps=n_steps),
        grid=(n_steps,),
        in_specs=[pl.BlockSpec((8, tm), lambda i: (0, i), memory_space=pltpu.SMEM),
                  pl.BlockSpec((8, tm), lambda i: (0, jnp.minimum(i + 1, n_steps - 1)),
                               memory_space=pltpu.SMEM),
                  tok(d), tok(LANES), tok(p2.shape[1]), const((1, d)), const(wpp.shape),
                  const(wpg.shape), const((1, d)),
                  pl.BlockSpec(memory_space=pl.ANY)],
        out_specs=tok(d),
        out_shape=jax.ShapeDtypeStruct((n, d), F32),
        scratch_shapes=[pltpu.VMEM((2, TOP_K, tm, d), F32), pltpu.SemaphoreType.DMA((2,))],
        compiler_params=pltpu.CompilerParams(
            dimension_semantics=("arbitrary",), vmem_limit_bytes=VMEM_LIMIT),
        name="combine",
    )(pos, pos, h1, gates, p2, pnw, wpp, wpg, fnw, y)


def kernel(x, p, attn_norm_w, w_in, sb_norm_w, hg_lower_bounds, hg_norm_w, w_out, ffn_norm_w,
           w_group_router, b_group_router, w_expert_router, b_expert_router, w_exp_gate, w_exp_up,
           w_exp_down, ple_norm_w, w_ple_proj, w_ple_gate, final_norm_w):
    b, s, d = x.shape
    depth = w_in.shape[0]
    assert depth == 1, "single-layer trunk"
    sbw = sb_norm_w.shape[1]
    hgw = hg_norm_w.shape[1]
    n = b * s
    tm_proj = min(512, s)
    tm_route = min(512, n)
    tm_disp = min(512, n)
    tm_comb = min(256, n)
    tmm = min(256, n)

    wi = w_in[0]
    w_main = jnp.concatenate([wi[:, 0:sbw], wi[:, 2 * sbw:]], axis=1).astype(BF16)
    w_kt = wi[:, sbw:2 * sbw].T.astype(BF16)
    gap = ROUTER_ROW0 - N_GROUPS
    tail = LANES - ROUTER_ROW0 - N_EXPERTS
    wr_t = jnp.concatenate([w_group_router[0].T, jnp.zeros((gap, d), F32), w_expert_router[0].T,
                            jnp.zeros((tail, d), F32)], axis=0)
    br = jnp.concatenate([b_group_router[0], jnp.zeros((gap,), F32), b_expert_router[0],
                          jnp.zeros((tail,), F32)])[:, None]

    sbq, sbkt, sbve, sbvo, hq, hgl, hk, hv, hgate = _in_proj(
        x, attn_norm_w[0][None, :], w_main, w_kt, hg_lower_bounds, tm=tm_proj)
    sbo = _sb_attn(sbq, sbkt, sbve, sbvo, sb_norm_w, chains=SB_CHAINS, qb=SB_QUERY_BLOCKS)
    hgo = _hgrn2(hq, hgl, hk, hv, hgate, hg_norm_w, chunks_per_iter=HG_CHUNKS_PER_ITER)

    h1, m, meta, gates, counts = _out_route(
        x.reshape(n, d), sbo.reshape(n, sbw), hgo.reshape(n, hgw), w_out[0].astype(BF16),
        ffn_norm_w[0][None, :], wr_t, br, tm=tm_route)

    cnt = counts[ROUTER_ROW0:ROUTER_ROW0 + N_EXPERTS, 0].astype(jnp.int32)
    padded = ((cnt + tmm - 1) // tmm) * tmm
    ends = jnp.cumsum(padded)
    starts = ends - padded
    n_rows = n * TOP_K + N_EXPERTS * tmm
    n_tiles = n_rows // tmm
    last_tile = ends[-1] // tmm - 1
    tile_src = jnp.minimum(jnp.arange(n_tiles, dtype=jnp.int32), last_tile)
    tile_expert = jnp.sum((ends[None, :] <= (tile_src * tmm)[:, None]).astype(jnp.int32), axis=1)
    seg = jnp.concatenate([starts, ends, padded, (last_tile + 1)[None],
                           jnp.zeros((LANES - 3 * N_EXPERTS - 1,), jnp.int32)])

    pos = _positions(seg, meta)
    xs = _dispatch(seg, pos, m, n_rows, tm=tm_disp, tmm=tmm)
    y = _experts(tile_expert, tile_src, xs, w_exp_gate[0].astype(BF16), w_exp_up[0].astype(BF16),
                 w_exp_down[0].astype(BF16), tmm=tmm)
    out = _combine(pos, h1, gates, p[0].reshape(n, -1), ple_norm_w[0][None, :],
                   w_ple_proj[0].astype(BF16), w_ple_gate[0].astype(BF16), final_norm_w[None, :],
                   y, tm=tm_comb)
    return out.reshape(b, s, d)
```

```python
import functools

import numpy as np
import jax
import jax.numpy as jnp
from jax import lax
from jax.experimental import pallas as pl
from jax.experimental.pallas import tpu as pltpu

F32 = jnp.float32
BF16 = jnp.bfloat16
EPS = 1e-6

SB_HEADS = 8
SB_HEAD_DIM = 64
HG_HEAD_DIM = 128
HG_CHUNK = 64
HG_CHUNKS_PER_ITER = 8
N_GROUPS = 4
EXPERTS_PER_GROUP = 8
N_EXPERTS = N_GROUPS * EXPERTS_PER_GROUP
TOP_K = 2
LANES = 128
ROUTER_ROW0 = 8
SB_BLOCK = 128
SB_QUERY_BLOCKS = 2
SB_CHAINS = 4
SB_LOG2_FLOOR = -152.0
LOG2E = 1.4426950408889634
VMEM_LIMIT = 56 * 1024 * 1024

_NT = (((1,), (1,)), ((), ()))
_TN = (((0,), (0,)), ((), ()))


def _sigmoid(x):
    return 1.0 / (1.0 + jnp.exp(-x))


def _split2(x):
    hi = x.astype(BF16)
    lo = (x - hi.astype(F32)).astype(BF16)
    return hi, lo


def _split3(x):
    hi = x.astype(BF16)
    r = x - hi.astype(F32)
    mid = r.astype(BF16)
    lo = (r - mid.astype(F32)).astype(BF16)
    return hi, mid, lo


def _in_proj_kernel(x_ref, nw_ref, w_ref, wkt_ref, lbp_ref,
                    sbq_ref, sbkt_ref, sbve_ref, sbvo_ref, hq_ref, hgl_ref, hk_ref, hv_ref, hgate_ref,
                    *, sbw, hgw):
    x = x_ref[0]
    ms = jnp.mean(x * x, axis=-1, keepdims=True)
    a = (x * lax.rsqrt(ms + EPS) * nw_ref[...]).astype(BF16)

    def seg(lo, width):
        return jnp.dot(a, w_ref[:, lo:lo + width], preferred_element_type=F32)

    sbq_ref[0] = (seg(0, sbw) * (SB_HEAD_DIM ** -0.5 * LOG2E)).astype(BF16)
    sbkt_ref[0] = lax.dot_general(wkt_ref[...], a, _NT, preferred_element_type=F32).astype(BF16)
    v = seg(sbw, sbw)
    even_head = (lax.broadcasted_iota(jnp.int32, v.shape, 1) & SB_HEAD_DIM) == 0
    sbve_ref[0] = jnp.where(even_head, v, 0.0).astype(BF16)
    sbvo_ref[0] = jnp.where(even_head, 0.0, v).astype(BF16)
    c = 2 * sbw
    q = seg(c, hgw)
    hq_ref[0] = q * _sigmoid(q)
    p0 = lbp_ref[0:1, :]
    p1 = lbp_ref[1:2, :]
    pm = jnp.maximum(p0, p1)
    e0 = jnp.exp(p0 - pm)
    e1 = jnp.exp(p1 - pm)
    lb = e0 / (e0 + e1)
    fz = seg(c + hgw, hgw)
    sg = _sigmoid(fz)
    f = lb + (1.0 - lb) * sg
    hgl_ref[0] = jnp.log(f)
    hk_ref[0] = 1.0 - f
    hv_ref[0] = seg(c + 2 * hgw, hgw).astype(BF16)
    g = seg(c + 3 * hgw, hgw)
    hgate_ref[0] = g * _sigmoid(g)


def _in_proj(x, nw, w_main, w_kt, lbp, *, tm):
    b, s, d = x.shape
    sbw = w_kt.shape[0]
    hgw = (w_main.shape[1] - 2 * sbw) // 4
    tok = lambda width: pl.BlockSpec((1, tm, width), lambda bi, i: (bi, i, 0))
    const = lambda shape: pl.BlockSpec(shape, lambda bi, i: (0,) * len(shape))
    out_shape = [
        jax.ShapeDtypeStruct((b, s, sbw), BF16),
        jax.ShapeDtypeStruct((b, sbw, s), BF16),
        jax.ShapeDtypeStruct((b, s, sbw), BF16),
        jax.ShapeDtypeStruct((b, s, sbw), BF16),
        jax.ShapeDtypeStruct((b, s, hgw), F32),
        jax.ShapeDtypeStruct((b, s, hgw), F32),
        jax.ShapeDtypeStruct((b, s, hgw), F32),
        jax.ShapeDtypeStruct((b, s, hgw), BF16),
        jax.ShapeDtypeStruct((b, s, hgw), F32),
    ]
    out_specs = [tok(sbw), pl.BlockSpec((1, sbw, tm), lambda bi, i: (bi, 0, i)), tok(sbw), tok(sbw),
                 tok(hgw), tok(hgw), tok(hgw), tok(hgw), tok(hgw)]
    return pl.pallas_call(
        functools.partial(_in_proj_kernel, sbw=sbw, hgw=hgw),
        grid=(b, s // tm),
        in_specs=[tok(d), const((1, d)), const(w_main.shape), const(w_kt.shape), const(lbp.shape)],
        out_specs=out_specs,
        out_shape=out_shape,
        compiler_params=pltpu.CompilerParams(
            dimension_semantics=("arbitrary", "arbitrary"), vmem_limit_bytes=VMEM_LIMIT),
        name="in_proj",
    )(x, nw, w_main, w_kt, lbp)


def _sb_cumsum_weights():
    j = np.arange(SB_BLOCK)[:, None]
    s = np.arange(SB_BLOCK)[None, :]
    half = np.concatenate([(j > s).astype(np.float32), np.ones((SB_BLOCK, SB_BLOCK), np.float32)], axis=1)
    return jnp.asarray(np.concatenate([half, half], axis=0), dtype=BF16)


def _sb_attn_kernel(q_ref, kt_ref, ve_ref, vo_ref, nw_ref, cw_ref, o_ref, *, n_groups, chains, qb):
    blk = SB_BLOCK
    dh = SB_HEAD_DIM
    rq = qb * blk
    rr = 2 * rq
    first = lax.broadcasted_iota(jnp.int32, (rq, LANES), 1) < dh
    rowpos = lax.broadcasted_iota(jnp.int32, (rr, blk), 0) & (rq - 1)
    colpos = lax.broadcasted_iota(jnp.int32, (rr, blk), 1)
    cw = cw_ref[...]

    def steps(qss, js, q0s, v_scales, carries, accs, masked):
        k0s = [pl.multiple_of(j * blk, blk) for j in js]
        zs = [jnp.dot(qs, kt_ref[0, :, pl.ds(k0, blk)], preferred_element_type=F32)
              for qs, k0 in zip(qss, k0s)]
        lss, hls, befores = [], [], []
        for z, k0, q0 in zip(zs, k0s, q0s):
            sp = jnp.log2(1.0 + jnp.exp2(-jnp.abs(z)))
            ls = jnp.minimum(z, 0.0) - sp
            lk = ls - z
            before = None
            if masked:
                before = (k0 + colpos) < (q0 + rowpos)
                lk = jnp.where(before, lk, 0.0)
            hi, lo = _split2(lk)
            lss.append(ls)
            hls.append(jnp.concatenate([hi, lo], axis=1))
            befores.append(before)
        css = [jnp.dot(hl, cw, preferred_element_type=F32) for hl in hls]
        abs_ = []
        for ls, cs, carry, before in zip(lss, css, carries, befores):
            a = jnp.exp2(ls + cs[:, :blk] + carry)
            if masked:
                a = jnp.where(before, a, 0.0)
            ab = a.astype(BF16)
            abs_.append(jnp.concatenate([ab[:rq], ab[rq:]], axis=1))
        new_accs = []
        for ab, k0, v_scale, acc in zip(abs_, k0s, v_scales, accs):
            vst = jnp.concatenate([ve_ref[0, pl.ds(k0, blk), :], vo_ref[0, pl.ds(k0, blk), :]], axis=0)
            if v_scale is not None:
                vst = vst * v_scale
            new_accs.append(acc + jnp.dot(ab, vst, preferred_element_type=F32))
        new_carries = [carry + cs[:, blk:] for carry, cs in zip(carries, css)]
        return new_carries, new_accs

    def highest(carries):
        m = carries[0]
        for c in carries[1:]:
            m = jnp.maximum(m, c)
        return jnp.max(m)

    def group(gi, _):
        sis = [gi * chains + c for c in range(chains)]
        q0s = [pl.multiple_of(si * rq, rq) for si in sis]
        qss = []
        for q0 in q0s:
            q2 = q_ref[0, pl.ds(q0, rq), :].astype(F32)
            qss.append(jnp.concatenate([jnp.where(first, q2, 0.0), jnp.where(first, 0.0, q2)],
                                       axis=0).astype(BF16))
        carries = [jnp.zeros((rr, blk), F32)] * chains
        accs = [jnp.zeros((rq, LANES), F32)] * chains
        for i in range(qb):
            carries, accs = steps(qss, [si * qb + (qb - 1 - i) for si in sis], q0s, [None] * chains,
                                  carries, accs, True)

        def cond(st):
            n, top = st[0], st[1]
            return (n <= sis[-1] * qb) & (top > SB_LOG2_FLOOR)

        def body(st):
            n = st[0]
            js, v_scales = [], []
            for c, si in enumerate(sis):
                d = si * qb - n
                if c < chains - 1:
                    v_scales.append(jnp.where(d >= 0, 1.0, 0.0).astype(BF16))
                    d = jnp.maximum(d, 0)
                else:
                    v_scales.append(None)
                js.append(d)
            cs_out, as_out = steps(qss, js, q0s, v_scales, list(st[2]), list(st[3]), False)
            return n + 1, highest(cs_out), tuple(cs_out), tuple(as_out)

        st = lax.while_loop(cond, body, (jnp.int32(1), highest(carries), tuple(carries), tuple(accs)))
        for c, si in enumerate(sis):
            acc = st[3][c]
            q0 = pl.multiple_of(si * rq, rq)
            sq = acc * acc
            s0 = jnp.sum(jnp.where(first, sq, 0.0), axis=-1, keepdims=True)
            s1 = jnp.sum(jnp.where(first, 0.0, sq), axis=-1, keepdims=True)
            ms = jnp.where(first, s0, s1) * (1.0 / dh)
            o_ref[0, pl.ds(q0, rq), :] = (acc * lax.rsqrt(ms + EPS) * nw_ref[...]).astype(o_ref.dtype)
        return 0

    lax.fori_loop(0, n_groups, group, 0)


def _sb_attn(q, kt, v_even, v_odd, nw, *, chains, qb):
    b, s, w = q.shape
    assert s % (chains * qb * SB_BLOCK) == 0 and w % LANES == 0
    n_pairs = w // LANES
    cw = _sb_cumsum_weights()
    tok = pl.BlockSpec((1, s, LANES), lambda bi, hp: (bi, 0, hp))
    return pl.pallas_call(
        functools.partial(_sb_attn_kernel, n_groups=s // (chains * qb * SB_BLOCK), chains=chains, qb=qb),
        grid=(b, n_pairs),
        in_specs=[
            tok,
            pl.BlockSpec((1, LANES, s), lambda bi, hp: (bi, hp, 0)),
            tok, tok,
            pl.BlockSpec((1, LANES), lambda bi, hp: (0, hp)),
            pl.BlockSpec(cw.shape, lambda bi, hp: (0, 0)),
        ],
        out_specs=tok,
        out_shape=jax.ShapeDtypeStruct((b, s, w), BF16),
        compiler_params=pltpu.CompilerParams(
            dimension_semantics=("arbitrary", "arbitrary"), vmem_limit_bytes=VMEM_LIMIT),
        name="sb_attn",
    )(q, kt, v_even, v_odd, nw, cw)


def _hgrn_levels():
    hs = []
    h = HG_CHUNK // 2
    while h >= 1:
        hs.append(h)
        h //= 2
    return hs


def _hgrn_consts():
    n = HG_CHUNK
    t = np.arange(n)[:, None]
    j = np.arange(n)[None, :]
    mats = [j <= t]
    masks = []
    for h in _hgrn_levels():
        mid = (t // (2 * h)) * (2 * h) + h - 1
        upper = (t % (2 * h)) >= h
        if h > 1:
            mats.append(np.where(upper, (j > mid) & (j <= t), (j > t) & (j <= mid)))
        masks.append(((t // (2 * h)) == (j // (2 * h))) & upper & ((j % (2 * h)) < h))
    masks.append(t == j)
    sums = np.concatenate(mats, axis=0).astype(np.float32)
    pm = np.concatenate(masks, axis=0).astype(np.float32)
    return jnp.asarray(sums, dtype=BF16), jnp.asarray(pm, dtype=F32)


def _hgrn2_kernel(q_ref, gl_ref, k_ref, v_ref, gate_ref, nw_ref, cs_ref, pm_ref, o_ref,
                  *, n_iters, chunks_per_iter):
    n = HG_CHUNK
    dk = HG_HEAD_DIM
    heads = 2
    levels = _hgrn_levels()
    rows = lax.broadcasted_iota(jnp.int32, (n, heads * dk), 0)
    odd = (rows & 1) != 0

    def body(it, states):
        base = it * (chunks_per_iter * n)
        cs = cs_ref[...]
        half = cs.shape[0] // 2
        units = []
        for c in range(chunks_per_iter):
            sl = pl.ds(pl.multiple_of(base + c * n, n), n)
            g = gl_ref[0, sl, :]
            g_hi, g_lo = _split2(g)
            d = jnp.concatenate(
                [jnp.dot(cs[:half], g_hi, preferred_element_type=F32)
                 + jnp.dot(cs[:half], g_lo, preferred_element_type=F32),
                 jnp.dot(cs[half:], g_hi, preferred_element_type=F32)
                 + jnp.dot(cs[half:], g_lo, preferred_element_type=F32)], axis=0)
            units.append(dict(sl=sl, g=g, d=d, q=q_ref[0, sl, :], k=k_ref[0, sl, :], v=v_ref[0, sl, :]))
        for un in units:
            q, k, g, d = un["q"], un["k"], un["g"], un["d"]
            bc = d[0:n]
            b_last = bc[n - 1:n, :]
            ws = []
            for li, h in enumerate(levels):
                dl = d[(1 + li) * n:(2 + li) * n] if h > 1 else jnp.where(odd, g, 0.0)
                upper = (rows & h) != 0
                ws.append((jnp.where(upper, q, k) * jnp.exp(dl)).astype(BF16))
            un["ws"] = ws
            un["qk"] = q * k
            un["qd"] = (q * jnp.exp(bc)).astype(BF16)
            un["kd"] = (k * jnp.exp(b_last - bc)).astype(BF16)
            un["decay"] = jnp.exp(b_last)
        hslices = [slice(hh * dk, (hh + 1) * dk) for hh in range(heads)]
        for un in units:
            un["gram"] = [[lax.dot_general(w[:, ls], w[:, ls], _NT, preferred_element_type=F32)
                           for w in un["ws"]] for ls in hslices]
            un["kv"] = [lax.dot_general(un["v"][:, ls], un["kd"][:, ls], _TN, preferred_element_type=F32)
                        for ls in hslices]
        for un in units:
            un["st"] = states
            states = tuple(states[hh] * un["decay"][:, ls] + un["kv"][hh] for hh, ls in enumerate(hslices))
            ps = []
            for hh, ls in enumerate(hslices):
                p = pm_ref[len(levels) * n:(len(levels) + 1) * n, :] * jnp.sum(un["qk"][:, ls], axis=1, keepdims=True)
                for li in range(len(levels)):
                    p = p + pm_ref[li * n:(li + 1) * n, :] * un["gram"][hh][li]
                ps.append(p.astype(BF16))
            un["p"] = ps
        for un in units:
            un["o"] = [jnp.dot(un["p"][hh], un["v"][:, ls], preferred_element_type=F32)
                       + lax.dot_general(un["qd"][:, ls], un["st"][hh].astype(BF16), _NT,
                                         preferred_element_type=F32)
                       for hh, ls in enumerate(hslices)]
        for un in units:
            outs = []
            for o in un["o"]:
                ms = jnp.mean(o * o, axis=-1, keepdims=True)
                outs.append(o * lax.rsqrt(ms + EPS))
            o2 = jnp.concatenate(outs, axis=1)
            o_ref[0, un["sl"], :] = (o2 * nw_ref[...] * gate_ref[0, un["sl"], :]).astype(o_ref.dtype)
        return states

    zero = jnp.zeros((dk, dk), F32)
    lax.fori_loop(0, n_iters, body, (zero, zero))


def _hgrn2(q, gl, k, v, gate, nw, *, chunks_per_iter):
    b, s, w = q.shape
    assert s % (HG_CHUNK * chunks_per_iter) == 0 and w % (2 * HG_HEAD_DIM) == 0
    pair = 2 * HG_HEAD_DIM
    heads = w // pair
    cs, pm = _hgrn_consts()
    tok = pl.BlockSpec((1, s, pair), lambda bi, h: (bi, 0, h))
    return pl.pallas_call(
        functools.partial(_hgrn2_kernel, n_iters=s // (HG_CHUNK * chunks_per_iter),
                          chunks_per_iter=chunks_per_iter),
        grid=(b, heads),
        in_specs=[tok, tok, tok, tok, tok,
                  pl.BlockSpec((1, pair), lambda bi, h: (0, h)),
                  pl.BlockSpec(cs.shape, lambda bi, h: (0, 0)),
                  pl.BlockSpec(pm.shape, lambda bi, h: (0, 0))],
        out_specs=tok,
        out_shape=jax.ShapeDtypeStruct((b, s, w), BF16),
        compiler_params=pltpu.CompilerParams(
            dimension_semantics=("arbitrary", "arbitrary"), vmem_limit_bytes=VMEM_LIMIT),
        name="hgrn2",
    )(q, gl, k, v, gate, nw, cs, pm)


def _out_route_kernel(x_ref, sbo_ref, hgo_ref, wo_ref, nw_ref, wr_ref, br_ref, su_ref,
                      h1_ref, m_ref, meta_ref, gates_ref, counts_ref, carry_ref, *, sbw):
    i = pl.program_id(0)

    @pl.when(i == 0)
    def _():
        carry_ref[...] = jnp.zeros_like(carry_ref)

    h1 = (x_ref[...]
          + jnp.dot(sbo_ref[...], wo_ref[0:sbw, :], preferred_element_type=F32)
          + jnp.dot(hgo_ref[...], wo_ref[sbw:, :], preferred_element_type=F32))
    h1_ref[...] = h1
    ms = jnp.mean(h1 * h1, axis=-1, keepdims=True)
    m = h1 * lax.rsqrt(ms + EPS) * nw_ref[...]
    m_ref[...] = m

    m3 = _split3(m)
    w3 = _split3(wr_ref[...])
    logits = br_ref[...]
    for wi, mi in ((0, 0), (0, 1), (1, 0), (0, 2), (1, 1), (2, 0)):
        logits = logits + lax.dot_general(w3[wi], m3[mi], _NT, preferred_element_type=F32)
    rid = lax.broadcasted_iota(jnp.int32, logits.shape, 0)
    neg = -jnp.inf
    big = jnp.int32(2 * LANES)

    def first_argmax(vals):
        vmax = jnp.max(vals, axis=0, keepdims=True)
        idx = jnp.min(jnp.where(vals == vmax, rid, big), axis=0, keepdims=True)
        return vmax, idx

    is_group = rid < N_GROUPS
    gmax, g_idx = first_argmax(jnp.where(is_group, logits, neg))
    gsum = jnp.sum(jnp.where(is_group, jnp.exp(logits - gmax), 0.0), axis=0, keepdims=True)
    g_prob = 1.0 / gsum
    lo_row = ROUTER_ROW0 + EXPERTS_PER_GROUP * g_idx
    el = jnp.where((rid >= lo_row) & (rid < lo_row + EXPERTS_PER_GROUP), logits, neg)
    v1, i1 = first_argmax(el)
    v2, i2 = first_argmax(jnp.where(rid == i1, neg, el))
    dd = jnp.exp(v2 - v1)
    p1 = 1.0 / (1.0 + dd)
    g1 = p1 * g_prob
    g2 = dd * p1 * g_prob

    hit1 = rid == i1
    hit2 = rid == i2
    onehot = jnp.where(hit1 | hit2, 1.0, 0.0)
    before_cnt = carry_ref[...] + jnp.dot(onehot.astype(BF16), su_ref[...], preferred_element_type=F32)
    r1 = jnp.sum(jnp.where(hit1, before_cnt, 0.0), axis=0, keepdims=True)
    r2 = jnp.sum(jnp.where(hit2, before_cnt, 0.0), axis=0, keepdims=True)
    carry_ref[...] = carry_ref[...] + jnp.sum(onehot, axis=1, keepdims=True)
    counts_ref[...] = carry_ref[...]

    meta_ref[...] = jnp.zeros_like(meta_ref)
    meta_ref[0:1, :] = i1 - ROUTER_ROW0
    meta_ref[1:2, :] = i2 - ROUTER_ROW0
    meta_ref[2:3, :] = r1.astype(jnp.int32)
    meta_ref[3:4, :] = r2.astype(jnp.int32)
    gt = jnp.where(rid == 0, g1, jnp.where(rid == 1, g2, 0.0))
    gates_ref[...] = gt.T


def _out_route(x2, sbo, hgo, w_out, nw, wr_t, br, *, tm):
    n, d = x2.shape
    sbw = sbo.shape[1]
    j = np.arange(tm)[:, None]
    t = np.arange(tm)[None, :]
    su = jnp.asarray((j < t).astype(np.float32), dtype=BF16)
    tok = lambda width: pl.BlockSpec((tm, width), lambda i: (i, 0))
    const = lambda shape: pl.BlockSpec(shape, lambda i: (0,) * len(shape))
    return pl.pallas_call(
        functools.partial(_out_route_kernel, sbw=sbw),
        grid=(n // tm,),
        in_specs=[tok(d), tok(sbw), tok(hgo.shape[1]), const(w_out.shape), const((1, d)),
                  const(wr_t.shape), const(br.shape), const(su.shape)],
        out_specs=[tok(d), tok(d), pl.BlockSpec((8, tm), lambda i: (0, i)), tok(LANES),
                   const((LANES, 1))],
        out_shape=[jax.ShapeDtypeStruct((n, d), F32), jax.ShapeDtypeStruct((n, d), F32),
                   jax.ShapeDtypeStruct((8, n), jnp.int32), jax.ShapeDtypeStruct((n, LANES), F32),
                   jax.ShapeDtypeStruct((LANES, 1), F32)],
        scratch_shapes=[pltpu.VMEM((LANES, 1), F32)],
        compiler_params=pltpu.CompilerParams(
            dimension_semantics=("arbitrary",), vmem_limit_bytes=VMEM_LIMIT),
        name="out_route",
    )(x2, sbo, hgo, w_out, nw, wr_t, br, su)


def _positions_kernel(seg_ref, meta_ref, pos_ref):
    e = meta_ref[0:TOP_K, :]
    start = jnp.zeros(e.shape, jnp.int32)
    for x in range(N_EXPERTS):
        start = jnp.where(e == x, seg_ref[x], start)
    pos_ref[...] = jnp.zeros_like(pos_ref)
    pos_ref[0:TOP_K, :] = start + meta_ref[TOP_K:2 * TOP_K, :]


def _positions(seg, meta):
    return pl.pallas_call(
        _positions_kernel,
        in_specs=[pl.BlockSpec(memory_space=pltpu.SMEM), pl.BlockSpec(memory_space=pltpu.VMEM)],
        out_specs=pl.BlockSpec(memory_space=pltpu.VMEM),
        out_shape=jax.ShapeDtypeStruct(meta.shape, jnp.int32),
        compiler_params=pltpu.CompilerParams(vmem_limit_bytes=VMEM_LIMIT),
        name="positions",
    )(seg, meta)


def _dispatch_kernel(seg_ref, pos_ref, m_ref, xs_ref, zbuf, sem, zsem, *, tm, tmm, n_tiles):
    i = pl.program_id(0)

    def zero_copy(e):
        tail = pl.multiple_of(seg_ref[N_EXPERTS + e] - tmm, tmm)
        return pltpu.make_async_copy(zbuf, xs_ref.at[pl.ds(tail, tmm), :], zsem)

    @pl.when(i == 0)
    def _():
        zbuf[...] = jnp.zeros_like(zbuf)
        for e in range(N_EXPERTS):
            @pl.when(seg_ref[2 * N_EXPERTS + e] > 0)
            def _():
                zero_copy(e).start()
        for e in range(N_EXPERTS):
            @pl.when(seg_ref[2 * N_EXPERTS + e] > 0)
            def _():
                zero_copy(e).wait()

        def unused_copy(t):
            return pltpu.make_async_copy(zbuf, xs_ref.at[pl.ds(pl.multiple_of(t * tmm, tmm), tmm), :], zsem)

        def start_unused(t, _):
            unused_copy(t).start()
            return 0

        def wait_unused(t, _):
            unused_copy(t).wait()
            return 0

        lax.fori_loop(seg_ref[3 * N_EXPERTS], n_tiles, start_unused, 0)
        lax.fori_loop(seg_ref[3 * N_EXPERTS], n_tiles, wait_unused, 0)

    for r in range(tm):
        src = m_ref.at[pl.ds(r, 1), :]
        for k in range(TOP_K):
            pltpu.make_async_copy(src, xs_ref.at[pl.ds(pos_ref[k, r], 1), :], sem).start()
    for _ in range(TOP_K):
        pltpu.make_async_copy(m_ref, xs_ref.at[pl.ds(0, tm), :], sem).wait()


def _dispatch(seg, pos, m, n_rows, *, tm, tmm):
    n, d = m.shape
    return pl.pallas_call(
        functools.partial(_dispatch_kernel, tm=tm, tmm=tmm, n_tiles=n_rows // tmm),
        grid=(n // tm,),
        in_specs=[pl.BlockSpec(memory_space=pltpu.SMEM),
                  pl.BlockSpec((8, tm), lambda i: (0, i), memory_space=pltpu.SMEM),
                  pl.BlockSpec((tm, d), lambda i: (i, 0))],
        out_specs=pl.BlockSpec(memory_space=pl.ANY),
        out_shape=jax.ShapeDtypeStruct((n_rows, d), F32),
        scratch_shapes=[pltpu.VMEM((tmm, d), F32), pltpu.SemaphoreType.DMA(()),
                        pltpu.SemaphoreType.DMA(())],
        compiler_params=pltpu.CompilerParams(
            dimension_semantics=("arbitrary",), vmem_limit_bytes=VMEM_LIMIT),
        name="dispatch",
    )(seg, pos, m)


def _experts_kernel(te_ref, ts_ref, xs_ref, wg_ref, wu_ref, wd_ref, y_ref):
    t = pl.program_id(0)

    @pl.when(ts_ref[t] == t)
    def _():
        x = xs_ref[...].astype(BF16)
        hg = jnp.dot(x, wg_ref[0], preferred_element_type=F32)
        hu = jnp.dot(x, wu_ref[0], preferred_element_type=F32)
        act = (hg * _sigmoid(hg) * hu).astype(BF16)
        y_ref[...] = jnp.dot(act, wd_ref[0], preferred_element_type=F32)

    @pl.when(ts_ref[t] != t)
    def _():
        y_ref[...] = jnp.zeros_like(y_ref)


def _experts(tile_expert, tile_src, xs, wg, wu, wd, *, tmm):
    n_rows, d = xs.shape
    de = wg.shape[2]
    grid_spec = pltpu.PrefetchScalarGridSpec(
        num_scalar_prefetch=2,
        grid=(n_rows // tmm,),
        in_specs=[pl.BlockSpec((tmm, d), lambda t, te, ts: (ts[t], 0)),
                  pl.BlockSpec((1, d, de), lambda t, te, ts: (te[t], 0, 0)),
                  pl.BlockSpec((1, d, de), lambda t, te, ts: (te[t], 0, 0)),
                  pl.BlockSpec((1, de, d), lambda t, te, ts: (te[t], 0, 0))],
        out_specs=pl.BlockSpec((tmm, d), lambda t, te, ts: (t, 0)),
    )
    return pl.pallas_call(
        _experts_kernel,
        grid_spec=grid_spec,
        out_shape=jax.ShapeDtypeStruct((n_rows, d), F32),
        compiler_params=pltpu.CompilerParams(
            dimension_semantics=("arbitrary",), vmem_limit_bytes=VMEM_LIMIT),
        name="experts",
    )(tile_expert, tile_src, xs, wg, wu, wd)


def _combine_kernel(pos_ref, posn_ref, h1_ref, gates_ref, p_ref, pnw_ref, wpp_ref, wpg_ref, fnw_ref,
                    y_ref, o_ref, ybuf, sem, *, tm, n_steps):
    i = pl.program_id(0)

    def gather(p_ref_, slot):
        for r in range(tm):
            for k in range(TOP_K):
                pltpu.make_async_copy(y_ref.at[pl.ds(p_ref_[k, r], 1), :],
                                      ybuf.at[slot, k, pl.ds(r, 1), :], sem.at[slot]).start()

    def wait(slot):
        for k in range(TOP_K):
            pltpu.make_async_copy(y_ref.at[pl.ds(0, tm), :], ybuf.at[slot, k], sem.at[slot]).wait()

    slot = i % 2

    @pl.when(i == 0)
    def _():
        gather(pos_ref, 0)

    wait(slot)
    gather(posn_ref, 1 - slot)

    gts = gates_ref[...]
    h2 = h1_ref[...] + gts[:, 0:1] * ybuf[slot, 0] + gts[:, 1:2] * ybuf[slot, 1]
    e = jnp.dot(p_ref[...].astype(BF16), wpp_ref[...], preferred_element_type=F32)
    ms = jnp.mean(h2 * h2, axis=-1, keepdims=True)
    hn = (h2 * lax.rsqrt(ms + EPS) * pnw_ref[...]).astype(BF16)
    gate = _sigmoid(jnp.dot(hn, wpg_ref[...], preferred_element_type=F32))
    h3 = h2 + gate * e
    ms3 = jnp.mean(h3 * h3, axis=-1, keepdims=True)
    o_ref[...] = h3 * lax.rsqrt(ms3 + EPS) * fnw_ref[...]

    @pl.when(i == n_steps - 1)
    def _():
        wait(1 - slot)


def _combine(pos, h1, gates, p2, pnw, wpp, wpg, fnw, y, *, tm):
    n, d = h1.shape
    n_steps = n // tm
    tok = lambda width: pl.BlockSpec((tm, width), lambda i: (i, 0))
    const = lambda shape: pl.BlockSpec(shape, lambda i: (0,) * len(shape))
    return pl.pallas_call(
        functools.partial(_combine_kernel, tm=tm, n_steps=n_steps),
        grid=(n_steps,),
        in_specs=[pl.BlockSpec((8, tm), lambda i: (0, i), memory_space=pltpu.SMEM),
                  pl.BlockSpec((8, tm), lambda i: (0, jnp.minimum(i + 1, n_steps - 1)),
                               memory_space=pltpu.SMEM),
                  tok(d), tok(LANES), tok(p2.shape[1]), const((1, d)), const(wpp.shape),
                  const(wpg.shape), const((1, d)),
                  pl.BlockSpec(memory_space=pl.ANY)],
        out_specs=tok(d),
        out_shape=jax.ShapeDtypeStruct((n, d), F32),
        scratch_shapes=[pltpu.VMEM((2, TOP_K, tm, d), F32), pltpu.SemaphoreType.DMA((2,))],
        compiler_params=pltpu.CompilerParams(
            dimension_semantics=("arbitrary",), vmem_limit_bytes=VMEM_LIMIT),
        name="combine",
    )(pos, pos, h1, gates, p2, pnw, wpp, wpg, fnw, y)


def kernel(x, p, attn_norm_w, w_in, sb_norm_w, hg_lower_bounds, hg_norm_w, w_out, ffn_norm_w,
           w_group_router, b_group_router, w_expert_router, b_expert_router, w_exp_gate, w_exp_up,
           w_exp_down, ple_norm_w, w_ple_proj, w_ple_gate, final_norm_w):
    b, s, d = x.shape
    depth = w_in.shape[0]
    assert depth == 1, "single-layer trunk"
    sbw = sb_norm_w.shape[1]
    hgw = hg_norm_w.shape[1]
    n = b * s
    tm_proj = min(512, s)
    tm_route = min(512, n)
    tm_disp = min(512, n)
    tm_comb = min(256, n)
    tmm = min(512, n)

    wi = w_in[0]
    w_main = jnp.concatenate([wi[:, 0:sbw], wi[:, 2 * sbw:]], axis=1).astype(BF16)
    w_kt = wi[:, sbw:2 * sbw].T.astype(BF16)
    gap = ROUTER_ROW0 - N_GROUPS
    tail = LANES - ROUTER_ROW0 - N_EXPERTS
    wr_t = jnp.concatenate([w_group_router[0].T, jnp.zeros((gap, d), F32), w_expert_router[0].T,
                            jnp.zeros((tail, d), F32)], axis=0)
    br = jnp.concatenate([b_group_router[0], jnp.zeros((gap,), F32), b_expert_router[0],
                          jnp.zeros((tail,), F32)])[:, None]

    sbq, sbkt, sbve, sbvo, hq, hgl, hk, hv, hgate = _in_proj(
        x, attn_norm_w[0][None, :], w_main, w_kt, hg_lower_bounds, tm=tm_proj)
    sbo = _sb_attn(sbq, sbkt, sbve, sbvo, sb_norm_w, chains=SB_CHAINS, qb=SB_QUERY_BLOCKS)
    hgo = _hgrn2(hq, hgl, hk, hv, hgate, hg_norm_w, chunks_per_iter=HG_CHUNKS_PER_ITER)

    h1, m, meta, gates, counts = _out_route(
        x.reshape(n, d), sbo.reshape(n, sbw), hgo.reshape(n, hgw), w_out[0].astype(BF16),
        ffn_norm_w[0][None, :], wr_t, br, tm=tm_route)

    cnt = counts[ROUTER_ROW0:ROUTER_ROW0 + N_EXPERTS, 0].astype(jnp.int32)
    padded = ((cnt + tmm - 1) // tmm) * tmm
    ends = jnp.cumsum(padded)
    starts = ends - padded
    n_rows = n * TOP_K + N_EXPERTS * tmm
    n_tiles = n_rows // tmm
    last_tile = ends[-1] // tmm - 1
    tile_src = jnp.minimum(jnp.arange(n_tiles, dtype=jnp.int32), last_tile)
    tile_expert = jnp.sum((ends[None, :] <= (tile_src * tmm)[:, None]).astype(jnp.int32), axis=1)
    seg = jnp.concatenate([starts, ends, padded, (last_tile + 1)[None],
                           jnp.zeros((LANES - 3 * N_EXPERTS - 1,), jnp.int32)])

    pos = _positions(seg, meta)
    xs = _dispatch(seg, pos, m, n_rows, tm=tm_disp, tmm=tmm)
    y = _experts(tile_expert, tile_src, xs, w_exp_gate[0].astype(BF16), w_exp_up[0].astype(BF16),
                 w_exp_down[0].astype(BF16), tmm=tmm)
    out = _combine(pos, h1, gates, p[0].reshape(n, -1), ple_norm_w[0][None, :],
                   w_ple_proj[0].astype(BF16), w_ple_gate[0].astype(BF16), final_norm_w[None, :],
                   y, tm=tm_comb)
    return out.reshape(b, s, d)
```

```python
import functools

import numpy as np
import jax
import jax.numpy as jnp
from jax import lax
from jax.experimental import pallas as pl
from jax.experimental.pallas import tpu as pltpu

F32 = jnp.float32
BF16 = jnp.bfloat16
EPS = 1e-6

SB_HEADS = 8
SB_HEAD_DIM = 64
HG_HEAD_DIM = 128
HG_CHUNK = 64
HG_CHUNKS_PER_ITER = 8
N_GROUPS = 4
EXPERTS_PER_GROUP = 8
N_EXPERTS = N_GROUPS * EXPERTS_PER_GROUP
TOP_K = 2
LANES = 128
ROUTER_ROW0 = 8
SB_BLOCK = 128
SB_QUERY_BLOCKS = 1
SB_CHAINS = 8
SB_LOG2_FLOOR = -152.0
LOG2E = 1.4426950408889634
VMEM_LIMIT = 56 * 1024 * 1024

_NT = (((1,), (1,)), ((), ()))
_TN = (((0,), (0,)), ((), ()))


def _sigmoid(x):
    return 1.0 / (1.0 + jnp.exp(-x))


def _split2(x):
    hi = x.astype(BF16)
    lo = (x - hi.astype(F32)).astype(BF16)
    return hi, lo


def _in_proj_kernel(x_ref, nw_ref, w_ref, wkt_ref, lbp_ref,
                    sbq_ref, sbkt_ref, sbve_ref, sbvo_ref, hq_ref, hgl_ref, hk_ref, hv_ref, hgate_ref,
                    *, sbw, hgw):
    x = x_ref[0]
    ms = jnp.mean(x * x, axis=-1, keepdims=True)
    a = (x * lax.rsqrt(ms + EPS) * nw_ref[...]).astype(BF16)

    def seg(lo, width):
        return jnp.dot(a, w_ref[:, lo:lo + width], preferred_element_type=F32)

    sbq_ref[0] = (seg(0, sbw) * (SB_HEAD_DIM ** -0.5 * LOG2E)).astype(BF16)
    sbkt_ref[0] = lax.dot_general(wkt_ref[...], a, _NT, preferred_element_type=F32).astype(BF16)
    v = seg(sbw, sbw)
    even_head = (lax.broadcasted_iota(jnp.int32, v.shape, 1) & SB_HEAD_DIM) == 0
    sbve_ref[0] = jnp.where(even_head, v, 0.0).astype(BF16)
    sbvo_ref[0] = jnp.where(even_head, 0.0, v).astype(BF16)
    c = 2 * sbw
    q = seg(c, hgw)
    hq_ref[0] = q * _sigmoid(q)
    p0 = lbp_ref[0:1, :]
    p1 = lbp_ref[1:2, :]
    pm = jnp.maximum(p0, p1)
    e0 = jnp.exp(p0 - pm)
    e1 = jnp.exp(p1 - pm)
    lb = e0 / (e0 + e1)
    fz = seg(c + hgw, hgw)
    sg = _sigmoid(fz)
    f = lb + (1.0 - lb) * sg
    hgl_ref[0] = jnp.log(f)
    hk_ref[0] = 1.0 - f
    hv_ref[0] = seg(c + 2 * hgw, hgw).astype(BF16)
    g = seg(c + 3 * hgw, hgw)
    hgate_ref[0] = g * _sigmoid(g)


def _in_proj(x, nw, w_main, w_kt, lbp, *, tm):
    b, s, d = x.shape
    sbw = w_kt.shape[0]
    hgw = (w_main.shape[1] - 2 * sbw) // 4
    tok = lambda width: pl.BlockSpec((1, tm, width), lambda bi, i: (bi, i, 0))
    const = lambda shape: pl.BlockSpec(shape, lambda bi, i: (0,) * len(shape))
    out_shape = [
        jax.ShapeDtypeStruct((b, s, sbw), BF16),
        jax.ShapeDtypeStruct((b, sbw, s), BF16),
        jax.ShapeDtypeStruct((b, s, sbw), BF16),
        jax.ShapeDtypeStruct((b, s, sbw), BF16),
        jax.ShapeDtypeStruct((b, s, hgw), F32),
        jax.ShapeDtypeStruct((b, s, hgw), F32),
        jax.ShapeDtypeStruct((b, s, hgw), F32),
        jax.ShapeDtypeStruct((b, s, hgw), BF16),
        jax.ShapeDtypeStruct((b, s, hgw), F32),
    ]
    out_specs = [tok(sbw), pl.BlockSpec((1, sbw, tm), lambda bi, i: (bi, 0, i)), tok(sbw), tok(sbw),
                 tok(hgw), tok(hgw), tok(hgw), tok(hgw), tok(hgw)]
    return pl.pallas_call(
        functools.partial(_in_proj_kernel, sbw=sbw, hgw=hgw),
        grid=(b, s // tm),
        in_specs=[tok(d), const((1, d)), const(w_main.shape), const(w_kt.shape), const(lbp.shape)],
        out_specs=out_specs,
        out_shape=out_shape,
        compiler_params=pltpu.CompilerParams(
            dimension_semantics=("arbitrary", "arbitrary"), vmem_limit_bytes=VMEM_LIMIT),
        name="in_proj",
    )(x, nw, w_main, w_kt, lbp)


def _sb_cumsum_weights():
    j = np.arange(SB_BLOCK)[:, None]
    s = np.arange(SB_BLOCK)[None, :]
    half = np.concatenate([(j > s).astype(np.float32), np.ones((SB_BLOCK, SB_BLOCK), np.float32)], axis=1)
    return jnp.asarray(np.concatenate([half, half], axis=0), dtype=BF16)


def _sb_attn_kernel(q_ref, kt_ref, ve_ref, vo_ref, nw_ref, cw_ref, o_ref, *, n_groups, chains, qb):
    blk = SB_BLOCK
    dh = SB_HEAD_DIM
    rq = qb * blk
    rr = 2 * rq
    first = lax.broadcasted_iota(jnp.int32, (rq, LANES), 1) < dh
    rowpos = lax.broadcasted_iota(jnp.int32, (rr, blk), 0) & (rq - 1)
    colpos = lax.broadcasted_iota(jnp.int32, (rr, blk), 1)
    cw = cw_ref[...]

    def steps(qss, js, q0s, v_scales, carries, accs, masked):
        k0s = [pl.multiple_of(j * blk, blk) for j in js]
        zs = [jnp.dot(qs, kt_ref[0, :, pl.ds(k0, blk)], preferred_element_type=F32)
              for qs, k0 in zip(qss, k0s)]
        lss, hls, befores = [], [], []
        for z, k0, q0 in zip(zs, k0s, q0s):
            sp = jnp.log2(1.0 + jnp.exp2(-jnp.abs(z)))
            ls = jnp.minimum(z, 0.0) - sp
            lk = ls - z
            before = None
            if masked:
                before = (k0 + colpos) < (q0 + rowpos)
                lk = jnp.where(before, lk, 0.0)
            hi, lo = _split2(lk)
            lss.append(ls)
            hls.append(jnp.concatenate([hi, lo], axis=1))
            befores.append(before)
        css = [jnp.dot(hl, cw, preferred_element_type=F32) for hl in hls]
        abs_ = []
        for ls, cs, carry, before in zip(lss, css, carries, befores):
            a = jnp.exp2(ls + cs[:, :blk] + carry)
            if masked:
                a = jnp.where(before, a, 0.0)
            ab = a.astype(BF16)
            abs_.append(jnp.concatenate([ab[:rq], ab[rq:]], axis=1))
        new_accs = []
        for ab, k0, v_scale, acc in zip(abs_, k0s, v_scales, accs):
            vst = jnp.concatenate([ve_ref[0, pl.ds(k0, blk), :], vo_ref[0, pl.ds(k0, blk), :]], axis=0)
            if v_scale is not None:
                vst = vst * v_scale
            new_accs.append(acc + jnp.dot(ab, vst, preferred_element_type=F32))
        new_carries = [carry + cs[:, blk:] for carry, cs in zip(carries, css)]
        return new_carries, new_accs

    def highest(carries):
        m = carries[0]
        for c in carries[1:]:
            m = jnp.maximum(m, c)
        return jnp.max(m)

    def group(gi, _):
        sis = [gi * chains + c for c in range(chains)]
        q0s = [pl.multiple_of(si * rq, rq) for si in sis]
        qss = []
        for q0 in q0s:
            q2 = q_ref[0, pl.ds(q0, rq), :].astype(F32)
            qss.append(jnp.concatenate([jnp.where(first, q2, 0.0), jnp.where(first, 0.0, q2)],
                                       axis=0).astype(BF16))
        carries = [jnp.zeros((rr, blk), F32)] * chains
        accs = [jnp.zeros((rq, LANES), F32)] * chains
        for i in range(qb):
            carries, accs = steps(qss, [si * qb + (qb - 1 - i) for si in sis], q0s, [None] * chains,
                                  carries, accs, True)

        def cond(st):
            n, top = st[0], st[1]
            return (n <= sis[-1] * qb) & (top > SB_LOG2_FLOOR)

        def body(st):
            n = st[0]
            js, v_scales = [], []
            for c, si in enumerate(sis):
                d = si * qb - n
                if c < chains - 1:
                    v_scales.append(jnp.where(d >= 0, 1.0, 0.0).astype(BF16))
                    d = jnp.maximum(d, 0)
                else:
                    v_scales.append(None)
                js.append(d)
            cs_out, as_out = steps(qss, js, q0s, v_scales, list(st[2]), list(st[3]), False)
            return n + 1, highest(cs_out), tuple(cs_out), tuple(as_out)

        st = lax.while_loop(cond, body, (jnp.int32(1), highest(carries), tuple(carries), tuple(accs)))
        for c, si in enumerate(sis):
            acc = st[3][c]
            q0 = pl.multiple_of(si * rq, rq)
            sq = acc * acc
            s0 = jnp.sum(jnp.where(first, sq, 0.0), axis=-1, keepdims=True)
            s1 = jnp.sum(jnp.where(first, 0.0, sq), axis=-1, keepdims=True)
            ms = jnp.where(first, s0, s1) * (1.0 / dh)
            o_ref[0, pl.ds(q0, rq), :] = (acc * lax.rsqrt(ms + EPS) * nw_ref[...]).astype(o_ref.dtype)
        return 0

    lax.fori_loop(0, n_groups, group, 0)


def _sb_attn(q, kt, v_even, v_odd, nw, *, chains, qb):
    b, s, w = q.shape
    assert s % (chains * qb * SB_BLOCK) == 0 and w % LANES == 0
    n_pairs = w // LANES
    cw = _sb_cumsum_weights()
    tok = pl.BlockSpec((1, s, LANES), lambda bi, hp: (bi, 0, hp))
    return pl.pallas_call(
        functools.partial(_sb_attn_kernel, n_groups=s // (chains * qb * SB_BLOCK), chains=chains, qb=qb),
        grid=(b, n_pairs),
        in_specs=[
            tok,
            pl.BlockSpec((1, LANES, s), lambda bi, hp: (bi, hp, 0)),
            tok, tok,
            pl.BlockSpec((1, LANES), lambda bi, hp: (0, hp)),
            pl.BlockSpec(cw.shape, lambda bi, hp: (0, 0)),
        ],
        out_specs=tok,
        out_shape=jax.ShapeDtypeStruct((b, s, w), BF16),
        compiler_params=pltpu.CompilerParams(
            dimension_semantics=("arbitrary", "arbitrary"), vmem_limit_bytes=VMEM_LIMIT),
        name="sb_attn",
    )(q, kt, v_even, v_odd, nw, cw)


def _hgrn_levels():
    hs = []
    h = HG_CHUNK // 2
    while h >= 1:
        hs.append(h)
        h //= 2
    return hs


def _hgrn_consts():
    n = HG_CHUNK
    t = np.arange(n)[:, None]
    j = np.arange(n)[None, :]
    mats = [j <= t]
    masks = []
    for h in _hgrn_levels():
        mid = (t // (2 * h)) * (2 * h) + h - 1
        upper = (t % (2 * h)) >= h
        if h > 1:
            mats.append(np.where(upper, (j > mid) & (j <= t), (j > t) & (j <= mid)))
        masks.append(((t // (2 * h)) == (j // (2 * h))) & upper & ((j % (2 * h)) < h))
    masks.append(t == j)
    sums = np.concatenate(mats, axis=0).astype(np.float32)
    pm = np.concatenate(masks, axis=0).astype(np.float32)
    return jnp.asarray(sums, dtype=BF16), jnp.asarray(pm, dtype=F32)


def _hgrn2_kernel(q_ref, gl_ref, k_ref, v_ref, gate_ref, nw_ref, cs_ref, pm_ref, o_ref,
                  *, n_iters, chunks_per_iter):
    n = HG_CHUNK
    dk = HG_HEAD_DIM
    heads = 2
    levels = _hgrn_levels()
    rows = lax.broadcasted_iota(jnp.int32, (n, heads * dk), 0)
    odd = (rows & 1) != 0

    def body(it, states):
        base = it * (chunks_per_iter * n)
        cs = cs_ref[...]
        half = cs.shape[0] // 2
        units = []
        for c in range(chunks_per_iter):
            sl = pl.ds(pl.multiple_of(base + c * n, n), n)
            g = gl_ref[0, sl, :]
            g_hi, g_lo = _split2(g)
            d = jnp.concatenate(
                [jnp.dot(cs[:half], g_hi, preferred_element_type=F32)
                 + jnp.dot(cs[:half], g_lo, preferred_element_type=F32),
                 jnp.dot(cs[half:], g_hi, preferred_element_type=F32)
                 + jnp.dot(cs[half:], g_lo, preferred_element_type=F32)], axis=0)
            units.append(dict(sl=sl, g=g, d=d, q=q_ref[0, sl, :], k=k_ref[0, sl, :], v=v_ref[0, sl, :]))
        for un in units:
            q, k, g, d = un["q"], un["k"], un["g"], un["d"]
            bc = d[0:n]
            b_last = bc[n - 1:n, :]
            ws = []
            for li, h in enumerate(levels):
                dl = d[(1 + li) * n:(2 + li) * n] if h > 1 else jnp.where(odd, g, 0.0)
                upper = (rows & h) != 0
                ws.append((jnp.where(upper, q, k) * jnp.exp(dl)).astype(BF16))
            un["ws"] = ws
            un["qk"] = q * k
            un["qd"] = (q * jnp.exp(bc)).astype(BF16)
            un["kd"] = (k * jnp.exp(b_last - bc)).astype(BF16)
            un["decay"] = jnp.exp(b_last)
        hslices = [slice(hh * dk, (hh + 1) * dk) for hh in range(heads)]
        for un in units:
            un["gram"] = [[lax.dot_general(w[:, ls], w[:, ls], _NT, preferred_element_type=F32)
                           for w in un["ws"]] for ls in hslices]
            un["kv"] = [lax.dot_general(un["v"][:, ls], un["kd"][:, ls], _TN, preferred_element_type=F32)
                        for ls in hslices]
        for un in units:
            un["st"] = states
            states = tuple(states[hh] * un["decay"][:, ls] + un["kv"][hh] for hh, ls in enumerate(hslices))
            ps = []
            for hh, ls in enumerate(hslices):
                p = pm_ref[len(levels) * n:(len(levels) + 1) * n, :] * jnp.sum(un["qk"][:, ls], axis=1, keepdims=True)
                for li in range(len(levels)):
                    p = p + pm_ref[li * n:(li + 1) * n, :] * un["gram"][hh][li]
                ps.append(p.astype(BF16))
            un["p"] = ps
        for un in units:
            un["o"] = [jnp.dot(un["p"][hh], un["v"][:, ls], preferred_element_type=F32)
                       + lax.dot_general(un["qd"][:, ls], un["st"][hh].astype(BF16), _NT,
                                         preferred_element_type=F32)
                       for hh, ls in enumerate(hslices)]
        for un in units:
            outs = []
            for o in un["o"]:
                ms = jnp.mean(o * o, axis=-1, keepdims=True)
                outs.append(o * lax.rsqrt(ms + EPS))
            o2 = jnp.concatenate(outs, axis=1)
            o_ref[0, un["sl"], :] = (o2 * nw_ref[...] * gate_ref[0, un["sl"], :]).astype(o_ref.dtype)
        return states

    zero = jnp.zeros((dk, dk), F32)
    lax.fori_loop(0, n_iters, body, (zero, zero))


def _hgrn2(q, gl, k, v, gate, nw, *, chunks_per_iter):
    b, s, w = q.shape
    assert s % (HG_CHUNK * chunks_per_iter) == 0 and w % (2 * HG_HEAD_DIM) == 0
    pair = 2 * HG_HEAD_DIM
    heads = w // pair
    cs, pm = _hgrn_consts()
    tok = pl.BlockSpec((1, s, pair), lambda bi, h: (bi, 0, h))
    return pl.pallas_call(
        functools.partial(_hgrn2_kernel, n_iters=s // (HG_CHUNK * chunks_per_iter),
                          chunks_per_iter=chunks_per_iter),
        grid=(b, heads),
        in_specs=[tok, tok, tok, tok, tok,
                  pl.BlockSpec((1, pair), lambda bi, h: (0, h)),
                  pl.BlockSpec(cs.shape, lambda bi, h: (0, 0)),
                  pl.BlockSpec(pm.shape, lambda bi, h: (0, 0))],
        out_specs=tok,
        out_shape=jax.ShapeDtypeStruct((b, s, w), BF16),
        compiler_params=pltpu.CompilerParams(
            dimension_semantics=("arbitrary", "arbitrary"), vmem_limit_bytes=VMEM_LIMIT),
        name="hgrn2",
    )(q, gl, k, v, gate, nw, cs, pm)


def _out_route_kernel(x_ref, sbo_ref, hgo_ref, wo_ref, nw_ref, wr_ref, br_ref, su_ref,
                      h1_ref, m_ref, meta_ref, gates_ref, counts_ref, carry_ref, *, sbw):
    i = pl.program_id(0)

    @pl.when(i == 0)
    def _():
        carry_ref[...] = jnp.zeros_like(carry_ref)

    h1 = (x_ref[...]
          + jnp.dot(sbo_ref[...], wo_ref[0:sbw, :], preferred_element_type=F32)
          + jnp.dot(hgo_ref[...], wo_ref[sbw:, :], preferred_element_type=F32))
    h1_ref[...] = h1
    ms = jnp.mean(h1 * h1, axis=-1, keepdims=True)
    m = h1 * lax.rsqrt(ms + EPS) * nw_ref[...]
    m_ref[...] = m

    m2 = _split2(m)
    w2 = _split2(wr_ref[...])
    logits = br_ref[...]
    for wi, mi in ((0, 0), (0, 1), (1, 0)):
        logits = logits + lax.dot_general(w2[wi], m2[mi], _NT, preferred_element_type=F32)
    rid = lax.broadcasted_iota(jnp.int32, logits.shape, 0)
    neg = -jnp.inf
    big = jnp.int32(2 * LANES)

    def first_argmax(vals):
        vmax = jnp.max(vals, axis=0, keepdims=True)
        idx = jnp.min(jnp.where(vals == vmax, rid, big), axis=0, keepdims=True)
        return vmax, idx

    is_group = rid < N_GROUPS
    gmax, g_idx = first_argmax(jnp.where(is_group, logits, neg))
    gsum = jnp.sum(jnp.where(is_group, jnp.exp(logits - gmax), 0.0), axis=0, keepdims=True)
    g_prob = 1.0 / gsum
    lo_row = ROUTER_ROW0 + EXPERTS_PER_GROUP * g_idx
    el = jnp.where((rid >= lo_row) & (rid < lo_row + EXPERTS_PER_GROUP), logits, neg)
    v1, i1 = first_argmax(el)
    v2, i2 = first_argmax(jnp.where(rid == i1, neg, el))
    dd = jnp.exp(v2 - v1)
    p1 = 1.0 / (1.0 + dd)
    g1 = p1 * g_prob
    g2 = dd * p1 * g_prob

    hit1 = rid == i1
    hit2 = rid == i2
    onehot = jnp.where(hit1 | hit2, 1.0, 0.0)
    before_cnt = carry_ref[...] + jnp.dot(onehot.astype(BF16), su_ref[...], preferred_element_type=F32)
    r1 = jnp.sum(jnp.where(hit1, before_cnt, 0.0), axis=0, keepdims=True)
    r2 = jnp.sum(jnp.where(hit2, before_cnt, 0.0), axis=0, keepdims=True)
    carry_ref[...] = carry_ref[...] + jnp.sum(onehot, axis=1, keepdims=True)
    counts_ref[...] = carry_ref[...]

    meta_ref[...] = jnp.zeros_like(meta_ref)
    meta_ref[0:1, :] = i1 - ROUTER_ROW0
    meta_ref[1:2, :] = i2 - ROUTER_ROW0
    meta_ref[2:3, :] = r1.astype(jnp.int32)
    meta_ref[3:4, :] = r2.astype(jnp.int32)
    gt = jnp.where(rid == 0, g1, jnp.where(rid == 1, g2, 0.0))
    gates_ref[...] = gt.T


def _out_route(x2, sbo, hgo, w_out, nw, wr_t, br, *, tm):
    n, d = x2.shape
    sbw = sbo.shape[1]
    j = np.arange(tm)[:, None]
    t = np.arange(tm)[None, :]
    su = jnp.asarray((j < t).astype(np.float32), dtype=BF16)
    tok = lambda width: pl.BlockSpec((tm, width), lambda i: (i, 0))
    const = lambda shape: pl.BlockSpec(shape, lambda i: (0,) * len(shape))
    return pl.pallas_call(
        functools.partial(_out_route_kernel, sbw=sbw),
        grid=(n // tm,),
        in_specs=[tok(d), tok(sbw), tok(hgo.shape[1]), const(w_out.shape), const((1, d)),
                  const(wr_t.shape), const(br.shape), const(su.shape)],
        out_specs=[tok(d), tok(d), pl.BlockSpec((8, tm), lambda i: (0, i)), tok(LANES),
                   const((LANES, 1))],
        out_shape=[jax.ShapeDtypeStruct((n, d), F32), jax.ShapeDtypeStruct((n, d), F32),
                   jax.ShapeDtypeStruct((8, n), jnp.int32), jax.ShapeDtypeStruct((n, LANES), F32),
                   jax.ShapeDtypeStruct((LANES, 1), F32)],
        scratch_shapes=[pltpu.VMEM((LANES, 1), F32)],
        compiler_params=pltpu.CompilerParams(
            dimension_semantics=("arbitrary",), vmem_limit_bytes=VMEM_LIMIT),
        name="out_route",
    )(x2, sbo, hgo, w_out, nw, wr_t, br, su)


def _positions_kernel(seg_ref, meta_ref, pos_ref):
    e = meta_ref[0:TOP_K, :]
    start = jnp.zeros(e.shape, jnp.int32)
    for x in range(N_EXPERTS):
        start = jnp.where(e == x, seg_ref[x], start)
    pos_ref[...] = jnp.zeros_like(pos_ref)
    pos_ref[0:TOP_K, :] = start + meta_ref[TOP_K:2 * TOP_K, :]


def _positions(seg, meta):
    return pl.pallas_call(
        _positions_kernel,
        in_specs=[pl.BlockSpec(memory_space=pltpu.SMEM), pl.BlockSpec(memory_space=pltpu.VMEM)],
        out_specs=pl.BlockSpec(memory_space=pltpu.VMEM),
        out_shape=jax.ShapeDtypeStruct(meta.shape, jnp.int32),
        compiler_params=pltpu.CompilerParams(vmem_limit_bytes=VMEM_LIMIT),
        name="positions",
    )(seg, meta)


def _dispatch_kernel(seg_ref, pos_ref, m_ref, xs_ref, zbuf, sem, zsem, *, tm, tmm, n_tiles):
    i = pl.program_id(0)

    def zero_copy(e):
        tail = pl.multiple_of(seg_ref[N_EXPERTS + e] - tmm, tmm)
        return pltpu.make_async_copy(zbuf, xs_ref.at[pl.ds(tail, tmm), :], zsem)

    @pl.when(i == 0)
    def _():
        zbuf[...] = jnp.zeros_like(zbuf)
        for e in range(N_EXPERTS):
            @pl.when(seg_ref[2 * N_EXPERTS + e] > 0)
            def _():
                zero_copy(e).start()
        for e in range(N_EXPERTS):
            @pl.when(seg_ref[2 * N_EXPERTS + e] > 0)
            def _():
                zero_copy(e).wait()

        def unused_copy(t):
            return pltpu.make_async_copy(zbuf, xs_ref.at[pl.ds(pl.multiple_of(t * tmm, tmm), tmm), :], zsem)

        def start_unused(t, _):
            unused_copy(t).start()
            return 0

        def wait_unused(t, _):
            unused_copy(t).wait()
            return 0

        lax.fori_loop(seg_ref[3 * N_EXPERTS], n_tiles, start_unused, 0)
        lax.fori_loop(seg_ref[3 * N_EXPERTS], n_tiles, wait_unused, 0)

    for r in range(tm):
        src = m_ref.at[pl.ds(r, 1), :]
        for k in range(TOP_K):
            pltpu.make_async_copy(src, xs_ref.at[pl.ds(pos_ref[k, r], 1), :], sem).start(priority=k)
    for _ in range(TOP_K):
        pltpu.make_async_copy(m_ref, xs_ref.at[pl.ds(0, tm), :], sem).wait()


def _dispatch(seg, pos, m, n_rows, *, tm, tmm):
    n, d = m.shape
    return pl.pallas_call(
        functools.partial(_dispatch_kernel, tm=tm, tmm=tmm, n_tiles=n_rows // tmm),
        grid=(n // tm,),
        in_specs=[pl.BlockSpec(memory_space=pltpu.SMEM),
                  pl.BlockSpec((8, tm), lambda i: (0, i), memory_space=pltpu.SMEM),
                  pl.BlockSpec((tm, d), lambda i: (i, 0))],
        out_specs=pl.BlockSpec(memory_space=pl.ANY),
        out_shape=jax.ShapeDtypeStruct((n_rows, d), F32),
        scratch_shapes=[pltpu.VMEM((tmm, d), F32), pltpu.SemaphoreType.DMA(()),
                        pltpu.SemaphoreType.DMA(())],
        compiler_params=pltpu.CompilerParams(
            dimension_semantics=("arbitrary",), vmem_limit_bytes=VMEM_LIMIT),
        name="dispatch",
    )(seg, pos, m)


def _experts_kernel(te_ref, ts_ref, xs_ref, wg_ref, wu_ref, wd_ref, y_ref):
    t = pl.program_id(0)

    @pl.when(ts_ref[t] == t)
    def _():
        x = xs_ref[...].astype(BF16)
        hg = jnp.dot(x, wg_ref[0], preferred_element_type=F32)
        hu = jnp.dot(x, wu_ref[0], preferred_element_type=F32)
        act = (hg * _sigmoid(hg) * hu).astype(BF16)
        y_ref[...] = jnp.dot(act, wd_ref[0], preferred_element_type=F32)

    @pl.when(ts_ref[t] != t)
    def _():
        y_ref[...] = jnp.zeros_like(y_ref)


def _experts(tile_expert, tile_src, xs, wg, wu, wd, *, tmm):
    n_rows, d = xs.shape
    de = wg.shape[2]
    grid_spec = pltpu.PrefetchScalarGridSpec(
        num_scalar_prefetch=2,
        grid=(n_rows // tmm,),
        in_specs=[pl.BlockSpec((tmm, d), lambda t, te, ts: (ts[t], 0)),
                  pl.BlockSpec((1, d, de), lambda t, te, ts: (te[t], 0, 0)),
                  pl.BlockSpec((1, d, de), lambda t, te, ts: (te[t], 0, 0)),
                  pl.BlockSpec((1, de, d), lambda t, te, ts: (te[t], 0, 0))],
        out_specs=pl.BlockSpec((tmm, d), lambda t, te, ts: (t, 0)),
    )
    return pl.pallas_call(
        _experts_kernel,
        grid_spec=grid_spec,
        out_shape=jax.ShapeDtypeStruct((n_rows, d), F32),
        compiler_params=pltpu.CompilerParams(
            dimension_semantics=("arbitrary",), vmem_limit_bytes=VMEM_LIMIT),
        name="experts",
    )(tile_expert, tile_src, xs, wg, wu, wd)


def _combine_kernel(pos_ref, posn_ref, h1_ref, gates_ref, p_ref, pnw_ref, wpp_ref, wpg_ref, fnw_ref,
                    y_ref, o_ref, ybuf, sem, *, tm, n_steps):
    i = pl.program_id(0)

    def gather(p_ref_, slot):
        for r in range(tm):
            for k in range(TOP_K):
                pltpu.make_async_copy(y_ref.at[pl.ds(p_ref_[k, r], 1), :],
                                      ybuf.at[slot, k, pl.ds(r, 1), :], sem.at[slot]).start(priority=k)

    def wait(slot):
        for k in range(TOP_K):
            pltpu.make_async_copy(y_ref.at[pl.ds(0, tm), :], ybuf.at[slot, k], sem.at[slot]).wait()

    @pl.when(i == 0)
    def _():
        gather(pos_ref, 0)

    def step(slot):
        wait(slot)
        gather(posn_ref, 1 - slot)
        gts = gates_ref[...]
        h2 = h1_ref[...] + gts[:, 0:1] * ybuf[slot, 0] + gts[:, 1:2] * ybuf[slot, 1]
        e = jnp.dot(p_ref[...].astype(BF16), wpp_ref[...], preferred_element_type=F32)
        ms = jnp.mean(h2 * h2, axis=-1, keepdims=True)
        hn = (h2 * lax.rsqrt(ms + EPS) * pnw_ref[...]).astype(BF16)
        gate = _sigmoid(jnp.dot(hn, wpg_ref[...], preferred_element_type=F32))
        h3 = h2 + gate * e
        ms3 = jnp.mean(h3 * h3, axis=-1, keepdims=True)
        o_ref[...] = h3 * lax.rsqrt(ms3 + EPS) * fnw_ref[...]

        @pl.when(i == n_steps - 1)
        def _():
            wait(1 - slot)

    for parity in range(2):
        @pl.when(i % 2 == parity)
        def _():
            step(parity)


def _combine(pos, h1, gates, p2, pnw, wpp, wpg, fnw, y, *, tm):
    n, d = h1.shape
    n_steps = n // tm
    tok = lambda width: pl.BlockSpec((tm, width), lambda i: (i, 0))
    const = lambda shape: pl.BlockSpec(shape, lambda i: (0,) * len(shape))
    return pl.pallas_call(
        functools.partial(_combine_kernel, tm=tm, n_steps=n_steps),
        grid=(n_steps,),
        in_specs=[pl.BlockSpec((8, tm), lambda i: (0, i), memory_space=pltpu.SMEM),
                  pl.BlockSpec((8, tm), lambda i: (0, jnp.minimum(i + 1, n_steps - 1)),
                               memory_space=pltpu.SMEM),
                  tok(d), tok(LANES), tok(p2.shape[1]), const((1, d)), const(wpp.shape),
                  const(wpg.shape), const((1, d)),
                  pl.BlockSpec(memory_space=pl.ANY)],
        out_specs=tok(d),
        out_shape=jax.ShapeDtypeStruct((n, d), F32),
        scratch_shapes=[pltpu.VMEM((2, TOP_K, tm, d), F32), pltpu.SemaphoreType.DMA((2,))],
        compiler_params=pltpu.CompilerParams(
            dimension_semantics=("arbitrary",), vmem_limit_bytes=VMEM_LIMIT),
        name="combine",
    )(pos, pos, h1, gates, p2, pnw, wpp, wpg, fnw, y)


def kernel(x, p, attn_norm_w, w_in, sb_norm_w, hg_lower_bounds, hg_norm_w, w_out, ffn_norm_w,
           w_group_router, b_group_router, w_expert_router, b_expert_router, w_exp_gate, w_exp_up,
           w_exp_down, ple_norm_w, w_ple_proj, w_ple_gate, final_norm_w):
    b, s, d = x.shape
    depth = w_in.shape[0]
    assert depth == 1, "single-layer trunk"
    sbw = sb_norm_w.shape[1]
    hgw = hg_norm_w.shape[1]
    n = b * s
    tm_proj = min(512, s)
    tm_route = min(512, n)
    tm_disp = min(512, n)
    tm_comb = min(256, n)
    tmm = min(512, n)

    wi = w_in[0]
    w_main = jnp.concatenate([wi[:, 0:sbw], wi[:, 2 * sbw:]], axis=1).astype(BF16)
    w_kt = wi[:, sbw:2 * sbw].T.astype(BF16)
    gap = ROUTER_ROW0 - N_GROUPS
    tail = LANES - ROUTER_ROW0 - N_EXPERTS
    wr_t = jnp.concatenate([w_group_router[0].T, jnp.zeros((gap, d), F32), w_expert_router[0].T,
                            jnp.zeros((tail, d), F32)], axis=0)
    br = jnp.concatenate([b_group_router[0], jnp.zeros((gap,), F32), b_expert_router[0],
                          jnp.zeros((tail,), F32)])[:, None]

    sbq, sbkt, sbve, sbvo, hq, hgl, hk, hv, hgate = _in_proj(
        x, attn_norm_w[0][None, :], w_main, w_kt, hg_lower_bounds, tm=tm_proj)
    sbo = _sb_attn(sbq, sbkt, sbve, sbvo, sb_norm_w, chains=SB_CHAINS, qb=SB_QUERY_BLOCKS)
    hgo = _hgrn2(hq, hgl, hk, hv, hgate, hg_norm_w, chunks_per_iter=HG_CHUNKS_PER_ITER)

    h1, m, meta, gates, counts = _out_route(
        x.reshape(n, d), sbo.reshape(n, sbw), hgo.reshape(n, hgw), w_out[0].astype(BF16),
        ffn_norm_w[0][None, :], wr_t, br, tm=tm_route)

    cnt = counts[ROUTER_ROW0:ROUTER_ROW0 + N_EXPERTS, 0].astype(jnp.int32)
    padded = ((cnt + tmm - 1) // tmm) * tmm
    ends = jnp.cumsum(padded)
    starts = ends - padded
    n_rows = n * TOP_K + N_EXPERTS * tmm
    n_tiles = n_rows // tmm
    last_tile = ends[-1] // tmm - 1
    tile_src = jnp.minimum(jnp.arange(n_tiles, dtype=jnp.int32), last_tile)
    tile_expert = jnp.sum((ends[None, :] <= (tile_src * tmm)[:, None]).astype(jnp.int32), axis=1)
    seg = jnp.concatenate([starts, ends, padded, (last_tile + 1)[None],
                           jnp.zeros((LANES - 3 * N_EXPERTS - 1,), jnp.int32)])

    pos = _positions(seg, meta)
    xs = _dispatch(seg, pos, m, n_rows, tm=tm_disp, tmm=tmm)
    y = _experts(tile_expert, tile_src, xs, w_exp_gate[0].astype(BF16), w_exp_up[0].astype(BF16),
                 w_exp_down[0].astype(BF16), tmm=tmm)
    out = _combine(pos, h1, gates, p[0].reshape(n, -1), ple_norm_w[0][None, :],
                   w_ple_proj[0].astype(BF16), w_ple_gate[0].astype(BF16), final_norm_w[None, :],
                   y, tm=tm_comb)
    return out.reshape(b, s, d)
```

```python
import functools

import numpy as np
import jax
import jax.numpy as jnp
from jax import lax
from jax.experimental import pallas as pl
from jax.experimental.pallas import tpu as pltpu

F32 = jnp.float32
BF16 = jnp.bfloat16
EPS = 1e-6

SB_HEADS = 8
SB_HEAD_DIM = 64
HG_HEAD_DIM = 128
HG_CHUNK = 64
HG_CHUNKS_PER_ITER = 8
N_GROUPS = 4
EXPERTS_PER_GROUP = 8
N_EXPERTS = N_GROUPS * EXPERTS_PER_GROUP
TOP_K = 2
LANES = 128
ROW_CHUNKS = 8
ROUTER_ROW0 = 8
SB_BLOCK = 128
SB_QUERY_BLOCKS = 1
SB_CHAINS = 8
SB_LOG2_FLOOR = -152.0
LOG2E = 1.4426950408889634
VMEM_LIMIT = 56 * 1024 * 1024

_NT = (((1,), (1,)), ((), ()))
_TN = (((0,), (0,)), ((), ()))


def _sigmoid(x):
    return 1.0 / (1.0 + jnp.exp(-x))


def _store_row_tiles(ref, x):
    rows = x.shape[0]
    for c in range(ROW_CHUNKS):
        ref[pl.ds(c, rows, stride=ROW_CHUNKS), :] = x[:, c * LANES:(c + 1) * LANES]


def _load_row_tiles(ref, rows):
    return jnp.concatenate([ref[pl.ds(c, rows, stride=ROW_CHUNKS), :] for c in range(ROW_CHUNKS)], axis=1)


def _split2(x):
    hi = x.astype(BF16)
    lo = (x - hi.astype(F32)).astype(BF16)
    return hi, lo


def _in_proj_kernel(x_ref, nw_ref, w_ref, wkt_ref, lbp_ref,
                    sbq_ref, sbkt_ref, sbve_ref, sbvo_ref, hq_ref, hgl_ref, hk_ref, hv_ref, hgate_ref,
                    *, sbw, hgw):
    x = x_ref[0]
    ms = jnp.mean(x * x, axis=-1, keepdims=True)
    a = (x * lax.rsqrt(ms + EPS) * nw_ref[...]).astype(BF16)

    def seg(lo, width):
        return jnp.dot(a, w_ref[:, lo:lo + width], preferred_element_type=F32)

    sbq_ref[0] = (seg(0, sbw) * (SB_HEAD_DIM ** -0.5 * LOG2E)).astype(BF16)
    sbkt_ref[0] = lax.dot_general(wkt_ref[...], a, _NT, preferred_element_type=F32).astype(BF16)
    v = seg(sbw, sbw)
    even_head = (lax.broadcasted_iota(jnp.int32, v.shape, 1) & SB_HEAD_DIM) == 0
    sbve_ref[0] = jnp.where(even_head, v, 0.0).astype(BF16)
    sbvo_ref[0] = jnp.where(even_head, 0.0, v).astype(BF16)
    c = 2 * sbw
    q = seg(c, hgw)
    hq_ref[0] = q * _sigmoid(q)
    p0 = lbp_ref[0:1, :]
    p1 = lbp_ref[1:2, :]
    pm = jnp.maximum(p0, p1)
    e0 = jnp.exp(p0 - pm)
    e1 = jnp.exp(p1 - pm)
    lb = e0 / (e0 + e1)
    fz = seg(c + hgw, hgw)
    sg = _sigmoid(fz)
    f = lb + (1.0 - lb) * sg
    hgl_ref[0] = jnp.log(f)
    hk_ref[0] = 1.0 - f
    hv_ref[0] = seg(c + 2 * hgw, hgw).astype(BF16)
    g = seg(c + 3 * hgw, hgw)
    hgate_ref[0] = g * _sigmoid(g)


def _in_proj(x, nw, w_main, w_kt, lbp, *, tm):
    b, s, d = x.shape
    sbw = w_kt.shape[0]
    hgw = (w_main.shape[1] - 2 * sbw) // 4
    tok = lambda width: pl.BlockSpec((1, tm, width), lambda bi, i: (bi, i, 0))
    const = lambda shape: pl.BlockSpec(shape, lambda bi, i: (0,) * len(shape))
    out_shape = [
        jax.ShapeDtypeStruct((b, s, sbw), BF16),
        jax.ShapeDtypeStruct((b, sbw, s), BF16),
        jax.ShapeDtypeStruct((b, s, sbw), BF16),
        jax.ShapeDtypeStruct((b, s, sbw), BF16),
        jax.ShapeDtypeStruct((b, s, hgw), F32),
        jax.ShapeDtypeStruct((b, s, hgw), F32),
        jax.ShapeDtypeStruct((b, s, hgw), F32),
        jax.ShapeDtypeStruct((b, s, hgw), BF16),
        jax.ShapeDtypeStruct((b, s, hgw), F32),
    ]
    out_specs = [tok(sbw), pl.BlockSpec((1, sbw, tm), lambda bi, i: (bi, 0, i)), tok(sbw), tok(sbw),
                 tok(hgw), tok(hgw), tok(hgw), tok(hgw), tok(hgw)]
    return pl.pallas_call(
        functools.partial(_in_proj_kernel, sbw=sbw, hgw=hgw),
        grid=(b, s // tm),
        in_specs=[tok(d), const((1, d)), const(w_main.shape), const(w_kt.shape), const(lbp.shape)],
        out_specs=out_specs,
        out_shape=out_shape,
        compiler_params=pltpu.CompilerParams(
            dimension_semantics=("arbitrary", "arbitrary"), vmem_limit_bytes=VMEM_LIMIT),
        name="in_proj",
    )(x, nw, w_main, w_kt, lbp)


def _sb_cumsum_weights():
    j = np.arange(SB_BLOCK)[:, None]
    s = np.arange(SB_BLOCK)[None, :]
    half = np.concatenate([(j > s).astype(np.float32), np.ones((SB_BLOCK, SB_BLOCK), np.float32)], axis=1)
    return jnp.asarray(np.concatenate([half, half], axis=0), dtype=BF16)


def _sb_attn_kernel(q_ref, kt_ref, ve_ref, vo_ref, nw_ref, cw_ref, o_ref, *, n_groups, chains, qb):
    blk = SB_BLOCK
    dh = SB_HEAD_DIM
    rq = qb * blk
    rr = 2 * rq
    first = lax.broadcasted_iota(jnp.int32, (rq, LANES), 1) < dh
    rowpos = lax.broadcasted_iota(jnp.int32, (rr, blk), 0) & (rq - 1)
    colpos = lax.broadcasted_iota(jnp.int32, (rr, blk), 1)
    cw = cw_ref[...]

    def steps(qss, js, q0s, v_scales, carries, accs, masked):
        k0s = [pl.multiple_of(j * blk, blk) for j in js]
        zs = [jnp.dot(qs, kt_ref[0, :, pl.ds(k0, blk)], preferred_element_type=F32)
              for qs, k0 in zip(qss, k0s)]
        lss, hls, befores = [], [], []
        for z, k0, q0 in zip(zs, k0s, q0s):
            sp = jnp.log2(1.0 + jnp.exp2(-jnp.abs(z)))
            ls = jnp.minimum(z, 0.0) - sp
            lk = ls - z
            before = None
            if masked:
                before = (k0 + colpos) < (q0 + rowpos)
                lk = jnp.where(before, lk, 0.0)
            hi, lo = _split2(lk)
            lss.append(ls)
            hls.append(jnp.concatenate([hi, lo], axis=1))
            befores.append(before)
        css = [jnp.dot(hl, cw, preferred_element_type=F32) for hl in hls]
        abs_ = []
        for ls, cs, carry, before in zip(lss, css, carries, befores):
            a = jnp.exp2(ls + cs[:, :blk] + carry)
            if masked:
                a = jnp.where(before, a, 0.0)
            ab = a.astype(BF16)
            abs_.append(jnp.concatenate([ab[:rq], ab[rq:]], axis=1))
        new_accs = []
        for ab, k0, v_scale, acc in zip(abs_, k0s, v_scales, accs):
            vst = jnp.concatenate([ve_ref[0, pl.ds(k0, blk), :], vo_ref[0, pl.ds(k0, blk), :]], axis=0)
            if v_scale is not None:
                vst = vst * v_scale
            new_accs.append(acc + jnp.dot(ab, vst, preferred_element_type=F32))
        new_carries = [carry + cs[:, blk:] for carry, cs in zip(carries, css)]
        return new_carries, new_accs

    def highest(carries):
        m = carries[0]
        for c in carries[1:]:
            m = jnp.maximum(m, c)
        return jnp.max(m)

    def group(gi, _):
        sis = [gi * chains + c for c in range(chains)]
        q0s = [pl.multiple_of(si * rq, rq) for si in sis]
        qss = []
        for q0 in q0s:
            q2 = q_ref[0, pl.ds(q0, rq), :].astype(F32)
            qss.append(jnp.concatenate([jnp.where(first, q2, 0.0), jnp.where(first, 0.0, q2)],
                                       axis=0).astype(BF16))
        carries = [jnp.zeros((rr, blk), F32)] * chains
        accs = [jnp.zeros((rq, LANES), F32)] * chains
        for i in range(qb):
            carries, accs = steps(qss, [si * qb + (qb - 1 - i) for si in sis], q0s, [None] * chains,
                                  carries, accs, True)

        def cond(st):
            n, top = st[0], st[1]
            return (n <= sis[-1] * qb) & (top > SB_LOG2_FLOOR)

        def body(st):
            n = st[0]
            js, v_scales = [], []
            for c, si in enumerate(sis):
                d = si * qb - n
                if c < chains - 1:
                    v_scales.append(jnp.where(d >= 0, 1.0, 0.0).astype(BF16))
                    d = jnp.maximum(d, 0)
                else:
                    v_scales.append(None)
                js.append(d)
            cs_out, as_out = steps(qss, js, q0s, v_scales, list(st[2]), list(st[3]), False)
            return n + 1, highest(cs_out), tuple(cs_out), tuple(as_out)

        st = lax.while_loop(cond, body, (jnp.int32(1), highest(carries), tuple(carries), tuple(accs)))
        for c, si in enumerate(sis):
            acc = st[3][c]
            q0 = pl.multiple_of(si * rq, rq)
            sq = acc * acc
            s0 = jnp.sum(jnp.where(first, sq, 0.0), axis=-1, keepdims=True)
            s1 = jnp.sum(jnp.where(first, 0.0, sq), axis=-1, keepdims=True)
            ms = jnp.where(first, s0, s1) * (1.0 / dh)
            o_ref[0, pl.ds(q0, rq), :] = (acc * lax.rsqrt(ms + EPS) * nw_ref[...]).astype(o_ref.dtype)
        return 0

    lax.fori_loop(0, n_groups, group, 0)


def _sb_attn(q, kt, v_even, v_odd, nw, *, chains, qb):
    b, s, w = q.shape
    assert s % (chains * qb * SB_BLOCK) == 0 and w % LANES == 0
    n_pairs = w // LANES
    cw = _sb_cumsum_weights()
    tok = pl.BlockSpec((1, s, LANES), lambda bi, hp: (bi, 0, hp))
    return pl.pallas_call(
        functools.partial(_sb_attn_kernel, n_groups=s // (chains * qb * SB_BLOCK), chains=chains, qb=qb),
        grid=(b, n_pairs),
        in_specs=[
            tok,
            pl.BlockSpec((1, LANES, s), lambda bi, hp: (bi, hp, 0)),
            tok, tok,
            pl.BlockSpec((1, LANES), lambda bi, hp: (0, hp)),
            pl.BlockSpec(cw.shape, lambda bi, hp: (0, 0)),
        ],
        out_specs=tok,
        out_shape=jax.ShapeDtypeStruct((b, s, w), BF16),
        compiler_params=pltpu.CompilerParams(
            dimension_semantics=("arbitrary", "arbitrary"), vmem_limit_bytes=VMEM_LIMIT),
        name="sb_attn",
    )(q, kt, v_even, v_odd, nw, cw)


def _hgrn_levels():
    hs = []
    h = HG_CHUNK // 2
    while h >= 1:
        hs.append(h)
        h //= 2
    return hs


def _hgrn_consts():
    n = HG_CHUNK
    t = np.arange(n)[:, None]
    j = np.arange(n)[None, :]
    mats = [j <= t]
    masks = []
    for h in _hgrn_levels():
        mid = (t // (2 * h)) * (2 * h) + h - 1
        upper = (t % (2 * h)) >= h
        if h > 1:
            mats.append(np.where(upper, (j > mid) & (j <= t), (j > t) & (j <= mid)))
        masks.append(((t // (2 * h)) == (j // (2 * h))) & upper & ((j % (2 * h)) < h))
    masks.append(t == j)
    sums = np.concatenate(mats, axis=0).astype(np.float32)
    pm = np.concatenate(masks, axis=0).astype(np.float32)
    return jnp.asarray(sums, dtype=BF16), jnp.asarray(pm, dtype=F32)


def _hgrn2_kernel(q_ref, gl_ref, k_ref, v_ref, gate_ref, nw_ref, cs_ref, pm_ref, o_ref,
                  *, n_iters, chunks_per_iter):
    n = HG_CHUNK
    dk = HG_HEAD_DIM
    heads = 2
    levels = _hgrn_levels()
    rows = lax.broadcasted_iota(jnp.int32, (n, heads * dk), 0)
    odd = (rows & 1) != 0

    def body(it, states):
        base = it * (chunks_per_iter * n)
        cs = cs_ref[...]
        half = cs.shape[0] // 2
        units = []
        for c in range(chunks_per_iter):
            sl = pl.ds(pl.multiple_of(base + c * n, n), n)
            g = gl_ref[0, sl, :]
            g_hi, g_lo = _split2(g)
            d = jnp.concatenate(
                [jnp.dot(cs[:half], g_hi, preferred_element_type=F32)
                 + jnp.dot(cs[:half], g_lo, preferred_element_type=F32),
                 jnp.dot(cs[half:], g_hi, preferred_element_type=F32)
                 + jnp.dot(cs[half:], g_lo, preferred_element_type=F32)], axis=0)
            units.append(dict(sl=sl, g=g, d=d, q=q_ref[0, sl, :], k=k_ref[0, sl, :], v=v_ref[0, sl, :]))
        for un in units:
            q, k, g, d = un["q"], un["k"], un["g"], un["d"]
            bc = d[0:n]
            b_last = bc[n - 1:n, :]
            ws = []
            for li, h in enumerate(levels):
                dl = d[(1 + li) * n:(2 + li) * n] if h > 1 else jnp.where(odd, g, 0.0)
                upper = (rows & h) != 0
                ws.append((jnp.where(upper, q, k) * jnp.exp(dl)).astype(BF16))
            un["ws"] = ws
            un["qk"] = q * k
            un["qd"] = (q * jnp.exp(bc)).astype(BF16)
            un["kd"] = (k * jnp.exp(b_last - bc)).astype(BF16)
            un["decay"] = jnp.exp(b_last)
        hslices = [slice(hh * dk, (hh + 1) * dk) for hh in range(heads)]
        for un in units:
            un["gram"] = [[lax.dot_general(w[:, ls], w[:, ls], _NT, preferred_element_type=F32)
                           for w in un["ws"]] for ls in hslices]
            un["kv"] = [lax.dot_general(un["v"][:, ls], un["kd"][:, ls], _TN, preferred_element_type=F32)
                        for ls in hslices]
        for un in units:
            un["st"] = states
            states = tuple(states[hh] * un["decay"][:, ls] + un["kv"][hh] for hh, ls in enumerate(hslices))
            ps = []
            for hh, ls in enumerate(hslices):
                p = pm_ref[len(levels) * n:(len(levels) + 1) * n, :] * jnp.sum(un["qk"][:, ls], axis=1, keepdims=True)
                for li in range(len(levels)):
                    p = p + pm_ref[li * n:(li + 1) * n, :] * un["gram"][hh][li]
                ps.append(p.astype(BF16))
            un["p"] = ps
        for un in units:
            un["o"] = [jnp.dot(un["p"][hh], un["v"][:, ls], preferred_element_type=F32)
                       + lax.dot_general(un["qd"][:, ls], un["st"][hh].astype(BF16), _NT,
                                         preferred_element_type=F32)
                       for hh, ls in enumerate(hslices)]
        for un in units:
            outs = []
            for o in un["o"]:
                ms = jnp.mean(o * o, axis=-1, keepdims=True)
                outs.append(o * lax.rsqrt(ms + EPS))
            o2 = jnp.concatenate(outs, axis=1)
            o_ref[0, un["sl"], :] = (o2 * nw_ref[...] * gate_ref[0, un["sl"], :]).astype(o_ref.dtype)
        return states

    zero = jnp.zeros((dk, dk), F32)
    lax.fori_loop(0, n_iters, body, (zero, zero))


def _hgrn2(q, gl, k, v, gate, nw, *, chunks_per_iter):
    b, s, w = q.shape
    assert s % (HG_CHUNK * chunks_per_iter) == 0 and w % (2 * HG_HEAD_DIM) == 0
    pair = 2 * HG_HEAD_DIM
    heads = w // pair
    cs, pm = _hgrn_consts()
    tok = pl.BlockSpec((1, s, pair), lambda bi, h: (bi, 0, h))
    return pl.pallas_call(
        functools.partial(_hgrn2_kernel, n_iters=s // (HG_CHUNK * chunks_per_iter),
                          chunks_per_iter=chunks_per_iter),
        grid=(b, heads),
        in_specs=[tok, tok, tok, tok, tok,
                  pl.BlockSpec((1, pair), lambda bi, h: (0, h)),
                  pl.BlockSpec(cs.shape, lambda bi, h: (0, 0)),
                  pl.BlockSpec(pm.shape, lambda bi, h: (0, 0))],
        out_specs=tok,
        out_shape=jax.ShapeDtypeStruct((b, s, w), BF16),
        compiler_params=pltpu.CompilerParams(
            dimension_semantics=("arbitrary", "arbitrary"), vmem_limit_bytes=VMEM_LIMIT),
        name="hgrn2",
    )(q, gl, k, v, gate, nw, cs, pm)


def _out_route_kernel(x_ref, sbo_ref, hgo_ref, wo_ref, nw_ref, wr_ref, br_ref, su_ref,
                      h1_ref, m_ref, meta_ref, gates_ref, counts_ref, carry_ref, *, sbw):
    i = pl.program_id(0)

    @pl.when(i == 0)
    def _():
        carry_ref[...] = jnp.zeros_like(carry_ref)

    h1 = (x_ref[...]
          + jnp.dot(sbo_ref[...], wo_ref[0:sbw, :], preferred_element_type=F32)
          + jnp.dot(hgo_ref[...], wo_ref[sbw:, :], preferred_element_type=F32))
    h1_ref[...] = h1
    ms = jnp.mean(h1 * h1, axis=-1, keepdims=True)
    m = h1 * lax.rsqrt(ms + EPS) * nw_ref[...]
    _store_row_tiles(m_ref, m)

    m2 = _split2(m)
    w2 = _split2(wr_ref[...])
    logits = br_ref[...]
    for wi, mi in ((0, 0), (0, 1), (1, 0)):
        logits = logits + lax.dot_general(w2[wi], m2[mi], _NT, preferred_element_type=F32)
    rid = lax.broadcasted_iota(jnp.int32, logits.shape, 0)
    neg = -jnp.inf
    big = jnp.int32(2 * LANES)

    def first_argmax(vals):
        vmax = jnp.max(vals, axis=0, keepdims=True)
        idx = jnp.min(jnp.where(vals == vmax, rid, big), axis=0, keepdims=True)
        return vmax, idx

    is_group = rid < N_GROUPS
    gmax, g_idx = first_argmax(jnp.where(is_group, logits, neg))
    gsum = jnp.sum(jnp.where(is_group, jnp.exp(logits - gmax), 0.0), axis=0, keepdims=True)
    g_prob = 1.0 / gsum
    lo_row = ROUTER_ROW0 + EXPERTS_PER_GROUP * g_idx
    el = jnp.where((rid >= lo_row) & (rid < lo_row + EXPERTS_PER_GROUP), logits, neg)
    v1, i1 = first_argmax(el)
    v2, i2 = first_argmax(jnp.where(rid == i1, neg, el))
    dd = jnp.exp(v2 - v1)
    p1 = 1.0 / (1.0 + dd)
    g1 = p1 * g_prob
    g2 = dd * p1 * g_prob

    hit1 = rid == i1
    hit2 = rid == i2
    onehot = jnp.where(hit1 | hit2, 1.0, 0.0)
    before_cnt = carry_ref[...] + jnp.dot(onehot.astype(BF16), su_ref[...], preferred_element_type=F32)
    r1 = jnp.sum(jnp.where(hit1, before_cnt, 0.0), axis=0, keepdims=True)
    r2 = jnp.sum(jnp.where(hit2, before_cnt, 0.0), axis=0, keepdims=True)
    carry_ref[...] = carry_ref[...] + jnp.sum(onehot, axis=1, keepdims=True)
    counts_ref[...] = carry_ref[...]

    meta_ref[...] = jnp.zeros_like(meta_ref)
    meta_ref[0:1, :] = i1 - ROUTER_ROW0
    meta_ref[1:2, :] = i2 - ROUTER_ROW0
    meta_ref[2:3, :] = r1.astype(jnp.int32)
    meta_ref[3:4, :] = r2.astype(jnp.int32)
    gt = jnp.where(rid == 0, g1, jnp.where(rid == 1, g2, 0.0))
    gates_ref[...] = gt.T


def _out_route(x2, sbo, hgo, w_out, nw, wr_t, br, *, tm):
    n, d = x2.shape
    assert d == ROW_CHUNKS * LANES
    sbw = sbo.shape[1]
    j = np.arange(tm)[:, None]
    t = np.arange(tm)[None, :]
    su = jnp.asarray((j < t).astype(np.float32), dtype=BF16)
    tok = lambda width: pl.BlockSpec((tm, width), lambda i: (i, 0))
    const = lambda shape: pl.BlockSpec(shape, lambda i: (0,) * len(shape))
    return pl.pallas_call(
        functools.partial(_out_route_kernel, sbw=sbw),
        grid=(n // tm,),
        in_specs=[tok(d), tok(sbw), tok(hgo.shape[1]), const(w_out.shape), const((1, d)),
                  const(wr_t.shape), const(br.shape), const(su.shape)],
        out_specs=[tok(d), pl.BlockSpec((tm * ROW_CHUNKS, LANES), lambda i: (i, 0)),
                   pl.BlockSpec((8, tm), lambda i: (0, i)), tok(LANES), const((LANES, 1))],
        out_shape=[jax.ShapeDtypeStruct((n, d), F32), jax.ShapeDtypeStruct((n * ROW_CHUNKS, LANES), F32),
                   jax.ShapeDtypeStruct((8, n), jnp.int32), jax.ShapeDtypeStruct((n, LANES), F32),
                   jax.ShapeDtypeStruct((LANES, 1), F32)],
        scratch_shapes=[pltpu.VMEM((LANES, 1), F32)],
        compiler_params=pltpu.CompilerParams(
            dimension_semantics=("arbitrary",), vmem_limit_bytes=VMEM_LIMIT),
        name="out_route",
    )(x2, sbo, hgo, w_out, nw, wr_t, br, su)


def _positions_kernel(seg_ref, meta_ref, pos_ref):
    e = meta_ref[0:TOP_K, :]
    start = jnp.zeros(e.shape, jnp.int32)
    for x in range(N_EXPERTS):
        start = jnp.where(e == x, seg_ref[x], start)
    pos_ref[...] = jnp.zeros_like(pos_ref)
    pos_ref[0:TOP_K, :] = start + meta_ref[TOP_K:2 * TOP_K, :]


def _positions(seg, meta):
    return pl.pallas_call(
        _positions_kernel,
        in_specs=[pl.BlockSpec(memory_space=pltpu.SMEM), pl.BlockSpec(memory_space=pltpu.VMEM)],
        out_specs=pl.BlockSpec(memory_space=pltpu.VMEM),
        out_shape=jax.ShapeDtypeStruct(meta.shape, jnp.int32),
        compiler_params=pltpu.CompilerParams(vmem_limit_bytes=VMEM_LIMIT),
        name="positions",
    )(seg, meta)


def _dispatch_kernel(seg_ref, pos_ref, m_ref, xs_ref, zbuf, sem, zsem, *, tm, tmm, n_tiles):
    i = pl.program_id(0)

    rc = ROW_CHUNKS

    def zero_copy(e):
        tail = pl.multiple_of((seg_ref[N_EXPERTS + e] - tmm) * rc, tmm * rc)
        return pltpu.make_async_copy(zbuf, xs_ref.at[pl.ds(tail, tmm * rc), :], zsem)

    @pl.when(i == 0)
    def _():
        zbuf[...] = jnp.zeros_like(zbuf)
        for e in range(N_EXPERTS):
            @pl.when(seg_ref[2 * N_EXPERTS + e] > 0)
            def _():
                zero_copy(e).start()
        for e in range(N_EXPERTS):
            @pl.when(seg_ref[2 * N_EXPERTS + e] > 0)
            def _():
                zero_copy(e).wait()

        def unused_copy(t):
            first = pl.multiple_of(t * (tmm * rc), tmm * rc)
            return pltpu.make_async_copy(zbuf, xs_ref.at[pl.ds(first, tmm * rc), :], zsem)

        def start_unused(t, _):
            unused_copy(t).start()
            return 0

        def wait_unused(t, _):
            unused_copy(t).wait()
            return 0

        lax.fori_loop(seg_ref[3 * N_EXPERTS], n_tiles, start_unused, 0)
        lax.fori_loop(seg_ref[3 * N_EXPERTS], n_tiles, wait_unused, 0)

    for r in range(tm):
        src = m_ref.at[pl.ds(r * rc, rc), :]
        for k in range(TOP_K):
            dst = pl.multiple_of(pos_ref[k, r] * rc, rc)
            pltpu.make_async_copy(src, xs_ref.at[pl.ds(dst, rc), :], sem).start(priority=k)
    for _ in range(TOP_K):
        pltpu.make_async_copy(m_ref, xs_ref.at[pl.ds(0, tm * rc), :], sem).wait()


def _dispatch(seg, pos, m, n_rows, *, tm, tmm):
    n = m.shape[0] // ROW_CHUNKS
    return pl.pallas_call(
        functools.partial(_dispatch_kernel, tm=tm, tmm=tmm, n_tiles=n_rows // tmm),
        grid=(n // tm,),
        in_specs=[pl.BlockSpec(memory_space=pltpu.SMEM),
                  pl.BlockSpec((8, tm), lambda i: (0, i), memory_space=pltpu.SMEM),
                  pl.BlockSpec((tm * ROW_CHUNKS, LANES), lambda i: (i, 0))],
        out_specs=pl.BlockSpec(memory_space=pl.ANY),
        out_shape=jax.ShapeDtypeStruct((n_rows * ROW_CHUNKS, LANES), F32),
        scratch_shapes=[pltpu.VMEM((tmm * ROW_CHUNKS, LANES), F32), pltpu.SemaphoreType.DMA(()),
                        pltpu.SemaphoreType.DMA(())],
        compiler_params=pltpu.CompilerParams(
            dimension_semantics=("arbitrary",), vmem_limit_bytes=VMEM_LIMIT),
        name="dispatch",
    )(seg, pos, m)


def _experts_kernel(te_ref, ts_ref, xs_ref, wg_ref, wu_ref, wd_ref, y_ref, *, tmm):
    t = pl.program_id(0)

    @pl.when(ts_ref[t] == t)
    def _():
        x = _load_row_tiles(xs_ref, tmm).astype(BF16)
        hg = jnp.dot(x, wg_ref[0].astype(BF16), preferred_element_type=F32)
        hu = jnp.dot(x, wu_ref[0].astype(BF16), preferred_element_type=F32)
        act = (hg * _sigmoid(hg) * hu).astype(BF16)
        _store_row_tiles(y_ref, jnp.dot(act, wd_ref[0].astype(BF16), preferred_element_type=F32))

    @pl.when(ts_ref[t] != t)
    def _():
        y_ref[...] = jnp.zeros_like(y_ref)


def _experts(tile_expert, tile_src, xs, wg, wu, wd, *, tmm):
    n_rows = xs.shape[0] // ROW_CHUNKS
    d, de = wg.shape[1], wg.shape[2]
    assert d == ROW_CHUNKS * LANES
    rows = pl.BlockSpec((tmm * ROW_CHUNKS, LANES), lambda t, te, ts: (ts[t], 0))
    grid_spec = pltpu.PrefetchScalarGridSpec(
        num_scalar_prefetch=2,
        grid=(n_rows // tmm,),
        in_specs=[rows,
                  pl.BlockSpec((1, d, de), lambda t, te, ts: (te[t], 0, 0)),
                  pl.BlockSpec((1, d, de), lambda t, te, ts: (te[t], 0, 0)),
                  pl.BlockSpec((1, de, d), lambda t, te, ts: (te[t], 0, 0))],
        out_specs=pl.BlockSpec((tmm * ROW_CHUNKS, LANES), lambda t, te, ts: (t, 0)),
    )
    return pl.pallas_call(
        functools.partial(_experts_kernel, tmm=tmm),
        grid_spec=grid_spec,
        out_shape=jax.ShapeDtypeStruct((n_rows * ROW_CHUNKS, LANES), F32),
        compiler_params=pltpu.CompilerParams(
            dimension_semantics=("arbitrary",), vmem_limit_bytes=VMEM_LIMIT),
        name="experts",
    )(tile_expert, tile_src, xs, wg, wu, wd)


def _combine_kernel(pos_ref, posn_ref, h1_ref, gates_ref, p_ref, pnw_ref, wpp_ref, wpg_ref, fnw_ref,
                    y_ref, o_ref, ybuf, sem, *, tm, n_steps):
    i = pl.program_id(0)

    rc = ROW_CHUNKS

    def gather(p_ref_, slot):
        for r in range(tm):
            for k in range(TOP_K):
                src = pl.multiple_of(p_ref_[k, r] * rc, rc)
                pltpu.make_async_copy(y_ref.at[pl.ds(src, rc), :],
                                      ybuf.at[slot, k, pl.ds(r * rc, rc), :], sem.at[slot]).start(priority=k)

    def wait(slot):
        for k in range(TOP_K):
            pltpu.make_async_copy(y_ref.at[pl.ds(0, tm * rc), :], ybuf.at[slot, k], sem.at[slot]).wait()

    @pl.when(i == 0)
    def _():
        gather(pos_ref, 0)

    def step(slot):
        wait(slot)
        gather(posn_ref, 1 - slot)
        gts = gates_ref[...]
        h2 = (h1_ref[...] + gts[:, 0:1] * _load_row_tiles(ybuf.at[slot, 0], tm)
              + gts[:, 1:2] * _load_row_tiles(ybuf.at[slot, 1], tm))
        e = jnp.dot(p_ref[...].astype(BF16), wpp_ref[...], preferred_element_type=F32)
        ms = jnp.mean(h2 * h2, axis=-1, keepdims=True)
        hn = (h2 * lax.rsqrt(ms + EPS) * pnw_ref[...]).astype(BF16)
        gate = _sigmoid(jnp.dot(hn, wpg_ref[...], preferred_element_type=F32))
        h3 = h2 + gate * e
        ms3 = jnp.mean(h3 * h3, axis=-1, keepdims=True)
        o_ref[...] = h3 * lax.rsqrt(ms3 + EPS) * fnw_ref[...]

        @pl.when(i == n_steps - 1)
        def _():
            wait(1 - slot)

    for parity in range(2):
        @pl.when(i % 2 == parity)
        def _():
            step(parity)


def _combine(pos, h1, gates, p2, pnw, wpp, wpg, fnw, y, *, tm):
    n, d = h1.shape
    n_steps = n // tm
    tok = lambda width: pl.BlockSpec((tm, width), lambda i: (i, 0))
    const = lambda shape: pl.BlockSpec(shape, lambda i: (0,) * len(shape))
    return pl.pallas_call(
        functools.partial(_combine_kernel, tm=tm, n_steps=n_steps),
        grid=(n_steps,),
        in_specs=[pl.BlockSpec((8, tm), lambda i: (0, i), memory_space=pltpu.SMEM),
                  pl.BlockSpec((8, tm), lambda i: (0, jnp.minimum(i + 1, n_steps - 1)),
                               memory_space=pltpu.SMEM),
                  tok(d), tok(LANES), tok(p2.shape[1]), const((1, d)), const(wpp.shape),
                  const(wpg.shape), const((1, d)),
                  pl.BlockSpec(memory_space=pl.ANY)],
        out_specs=tok(d),
        out_shape=jax.ShapeDtypeStruct((n, d), F32),
        scratch_shapes=[pltpu.VMEM((2, TOP_K, tm * ROW_CHUNKS, LANES), F32), pltpu.SemaphoreType.DMA((2,))],
        compiler_params=pltpu.CompilerParams(
            dimension_semantics=("arbitrary",), vmem_limit_bytes=VMEM_LIMIT),
        name="combine",
    )(pos, pos, h1, gates, p2, pnw, wpp, wpg, fnw, y)


def kernel(x, p, attn_norm_w, w_in, sb_norm_w, hg_lower_bounds, hg_norm_w, w_out, ffn_norm_w,
           w_group_router, b_group_router, w_expert_router, b_expert_router, w_exp_gate, w_exp_up,
           w_exp_down, ple_norm_w, w_ple_proj, w_ple_gate, final_norm_w):
    b, s, d = x.shape
    depth = w_in.shape[0]
    assert depth == 1, "single-layer trunk"
    sbw = sb_norm_w.shape[1]
    hgw = hg_norm_w.shape[1]
    n = b * s
    tm_proj = min(512, s)
    tm_route = min(512, n)
    tm_disp = min(512, n)
    tm_comb = min(256, n)
    tmm = min(512, n)

    wi = w_in[0]
    w_main = jnp.concatenate([wi[:, 0:sbw], wi[:, 2 * sbw:]], axis=1).astype(BF16)
    w_kt = wi[:, sbw:2 * sbw].T.astype(BF16)
    gap = ROUTER_ROW0 - N_GROUPS
    tail = LANES - ROUTER_ROW0 - N_EXPERTS
    wr_t = jnp.concatenate([w_group_router[0].T, jnp.zeros((gap, d), F32), w_expert_router[0].T,
                            jnp.zeros((tail, d), F32)], axis=0)
    br = jnp.concatenate([b_group_router[0], jnp.zeros((gap,), F32), b_expert_router[0],
                          jnp.zeros((tail,), F32)])[:, None]

    sbq, sbkt, sbve, sbvo, hq, hgl, hk, hv, hgate = _in_proj(
        x, attn_norm_w[0][None, :], w_main, w_kt, hg_lower_bounds, tm=tm_proj)
    sbo = _sb_attn(sbq, sbkt, sbve, sbvo, sb_norm_w, chains=SB_CHAINS, qb=SB_QUERY_BLOCKS)
    hgo = _hgrn2(hq, hgl, hk, hv, hgate, hg_norm_w, chunks_per_iter=HG_CHUNKS_PER_ITER)

    h1, m, meta, gates, counts = _out_route(
        x.reshape(n, d), sbo.reshape(n, sbw), hgo.reshape(n, hgw), w_out[0].astype(BF16),
        ffn_norm_w[0][None, :], wr_t, br, tm=tm_route)

    cnt = counts[ROUTER_ROW0:ROUTER_ROW0 + N_EXPERTS, 0].astype(jnp.int32)
    padded = ((cnt + tmm - 1) // tmm) * tmm
    ends = jnp.cumsum(padded)
    starts = ends - padded
    n_rows = n * TOP_K + N_EXPERTS * tmm
    n_tiles = n_rows // tmm
    last_tile = ends[-1] // tmm - 1
    tile_src = jnp.minimum(jnp.arange(n_tiles, dtype=jnp.int32), last_tile)
    tile_expert = jnp.sum((ends[None, :] <= (tile_src * tmm)[:, None]).astype(jnp.int32), axis=1)
    seg = jnp.concatenate([starts, ends, padded, (last_tile + 1)[None],
                           jnp.zeros((LANES - 3 * N_EXPERTS - 1,), jnp.int32)])

    pos = _positions(seg, meta)
    xs = _dispatch(seg, pos, m, n_rows, tm=tm_disp, tmm=tmm)
    y = _experts(tile_expert, tile_src, xs, w_exp_gate[0], w_exp_up[0], w_exp_down[0], tmm=tmm)
    out = _combine(pos, h1, gates, p[0].reshape(n, -1), ple_norm_w[0][None, :],
                   w_ple_proj[0].astype(BF16), w_ple_gate[0].astype(BF16), final_norm_w[None, :],
                   y, tm=tm_comb)
    return out.reshape(b, s, d)
```

```python
import functools

import numpy as np
import jax
import jax.numpy as jnp
from jax import lax
from jax.experimental import pallas as pl
from jax.experimental.pallas import tpu as pltpu

F32 = jnp.float32
BF16 = jnp.bfloat16
EPS = 1e-6

SB_HEADS = 8
SB_HEAD_DIM = 64
HG_HEAD_DIM = 128
HG_CHUNK = 64
HG_CHUNKS_PER_ITER = 8
N_GROUPS = 4
EXPERTS_PER_GROUP = 8
N_EXPERTS = N_GROUPS * EXPERTS_PER_GROUP
TOP_K = 2
LANES = 128
ROW_CHUNKS = 8
ROUTER_ROW0 = 8
SB_BLOCK = 128
SB_QUERY_BLOCKS = 1
SB_CHAINS = 8
SB_LOG2_FLOOR = -152.0
LOG2E = 1.4426950408889634
VMEM_LIMIT = 56 * 1024 * 1024

_NT = (((1,), (1,)), ((), ()))
_TN = (((0,), (0,)), ((), ()))


def _sigmoid(x):
    return 1.0 / (1.0 + jnp.exp(-x))


def _store_row_tiles(ref, x):
    rows = x.shape[0]
    for c in range(ROW_CHUNKS):
        ref[pl.ds(c, rows, stride=ROW_CHUNKS), :] = x[:, c * LANES:(c + 1) * LANES]


def _load_row_tiles(ref, rows):
    return jnp.concatenate([ref[pl.ds(c, rows, stride=ROW_CHUNKS), :] for c in range(ROW_CHUNKS)], axis=1)


def _split2(x):
    hi = x.astype(BF16)
    lo = (x - hi.astype(F32)).astype(BF16)
    return hi, lo


def _in_proj_kernel(x_ref, nw_ref, w_ref, wkt_ref, lbp_ref,
                    sbq_ref, sbkt_ref, sbve_ref, sbvo_ref, hq_ref, hgl_ref, hk_ref, hv_ref, hgate_ref,
                    *, sbw, hgw):
    x = x_ref[0]
    ms = jnp.mean(x * x, axis=-1, keepdims=True)
    a = (x * lax.rsqrt(ms + EPS) * nw_ref[...]).astype(BF16)

    def seg(lo, width):
        return jnp.dot(a, w_ref[:, lo:lo + width], preferred_element_type=F32)

    sbq_ref[0] = (seg(0, sbw) * (SB_HEAD_DIM ** -0.5 * LOG2E)).astype(BF16)
    sbkt_ref[0] = lax.dot_general(wkt_ref[...], a, _NT, preferred_element_type=F32).astype(BF16)
    v = seg(sbw, sbw)
    even_head = (lax.broadcasted_iota(jnp.int32, v.shape, 1) & SB_HEAD_DIM) == 0
    sbve_ref[0] = jnp.where(even_head, v, 0.0).astype(BF16)
    sbvo_ref[0] = jnp.where(even_head, 0.0, v).astype(BF16)
    c = 2 * sbw
    q = seg(c, hgw)
    hq_ref[0] = q * _sigmoid(q)
    p0 = lbp_ref[0:1, :]
    p1 = lbp_ref[1:2, :]
    pm = jnp.maximum(p0, p1)
    e0 = jnp.exp(p0 - pm)
    e1 = jnp.exp(p1 - pm)
    lb = e0 / (e0 + e1)
    fz = seg(c + hgw, hgw)
    sg = _sigmoid(fz)
    f = lb + (1.0 - lb) * sg
    hgl_ref[0] = jnp.log(f)
    hk_ref[0] = 1.0 - f
    hv_ref[0] = seg(c + 2 * hgw, hgw).astype(BF16)
    g = seg(c + 3 * hgw, hgw)
    hgate_ref[0] = g * _sigmoid(g)


def _in_proj(x, nw, w_main, w_kt, lbp, *, tm):
    b, s, d = x.shape
    sbw = w_kt.shape[0]
    hgw = (w_main.shape[1] - 2 * sbw) // 4
    tok = lambda width: pl.BlockSpec((1, tm, width), lambda bi, i: (bi, i, 0))
    const = lambda shape: pl.BlockSpec(shape, lambda bi, i: (0,) * len(shape))
    out_shape = [
        jax.ShapeDtypeStruct((b, s, sbw), BF16),
        jax.ShapeDtypeStruct((b, sbw, s), BF16),
        jax.ShapeDtypeStruct((b, s, sbw), BF16),
        jax.ShapeDtypeStruct((b, s, sbw), BF16),
        jax.ShapeDtypeStruct((b, s, hgw), F32),
        jax.ShapeDtypeStruct((b, s, hgw), F32),
        jax.ShapeDtypeStruct((b, s, hgw), F32),
        jax.ShapeDtypeStruct((b, s, hgw), BF16),
        jax.ShapeDtypeStruct((b, s, hgw), F32),
    ]
    out_specs = [tok(sbw), pl.BlockSpec((1, sbw, tm), lambda bi, i: (bi, 0, i)), tok(sbw), tok(sbw),
                 tok(hgw), tok(hgw), tok(hgw), tok(hgw), tok(hgw)]
    return pl.pallas_call(
        functools.partial(_in_proj_kernel, sbw=sbw, hgw=hgw),
        grid=(b, s // tm),
        in_specs=[tok(d), const((1, d)), const(w_main.shape), const(w_kt.shape), const(lbp.shape)],
        out_specs=out_specs,
        out_shape=out_shape,
        compiler_params=pltpu.CompilerParams(
            dimension_semantics=("arbitrary", "arbitrary"), vmem_limit_bytes=VMEM_LIMIT),
        name="in_proj",
    )(x, nw, w_main, w_kt, lbp)


def _sb_cumsum_weights():
    j = np.arange(SB_BLOCK)[:, None]
    s = np.arange(SB_BLOCK)[None, :]
    half = np.concatenate([(j > s).astype(np.float32), np.ones((SB_BLOCK, SB_BLOCK), np.float32)], axis=1)
    return jnp.asarray(np.concatenate([half, half], axis=0), dtype=BF16)


def _sb_attn_kernel(q_ref, kt_ref, ve_ref, vo_ref, nw_ref, cw_ref, o_ref, *, n_groups, chains, qb):
    blk = SB_BLOCK
    dh = SB_HEAD_DIM
    rq = qb * blk
    rr = 2 * rq
    first = lax.broadcasted_iota(jnp.int32, (rq, LANES), 1) < dh
    rowpos = lax.broadcasted_iota(jnp.int32, (rr, blk), 0) & (rq - 1)
    colpos = lax.broadcasted_iota(jnp.int32, (rr, blk), 1)
    cw = cw_ref[...]

    def steps(qss, js, q0s, v_scales, carries, accs, masked):
        k0s = [pl.multiple_of(j * blk, blk) for j in js]
        zs = [jnp.dot(qs, kt_ref[0, :, pl.ds(k0, blk)], preferred_element_type=F32)
              for qs, k0 in zip(qss, k0s)]
        lss, hls, befores = [], [], []
        for z, k0, q0 in zip(zs, k0s, q0s):
            sp = jnp.log2(1.0 + jnp.exp2(-jnp.abs(z)))
            ls = jnp.minimum(z, 0.0) - sp
            lk = ls - z
            before = None
            if masked:
                before = (k0 + colpos) < (q0 + rowpos)
                lk = jnp.where(before, lk, 0.0)
            hi, lo = _split2(lk)
            lss.append(ls)
            hls.append(jnp.concatenate([hi, lo], axis=1))
            befores.append(before)
        css = [jnp.dot(hl, cw, preferred_element_type=F32) for hl in hls]
        abs_ = []
        for ls, cs, carry, before in zip(lss, css, carries, befores):
            a = jnp.exp2(ls + cs[:, :blk] + carry)
            if masked:
                a = jnp.where(before, a, 0.0)
            ab = a.astype(BF16)
            abs_.append(jnp.concatenate([ab[:rq], ab[rq:]], axis=1))
        new_accs = []
        for ab, k0, v_scale, acc in zip(abs_, k0s, v_scales, accs):
            vst = jnp.concatenate([ve_ref[0, pl.ds(k0, blk), :], vo_ref[0, pl.ds(k0, blk), :]], axis=0)
            if v_scale is not None:
                vst = vst * v_scale
            new_accs.append(acc + jnp.dot(ab, vst, preferred_element_type=F32))
        new_carries = [carry + cs[:, blk:] for carry, cs in zip(carries, css)]
        return new_carries, new_accs

    def highest(carries):
        m = carries[0]
        for c in carries[1:]:
            m = jnp.maximum(m, c)
        return jnp.max(m)

    def group(gi, _):
        sis = [gi * chains + c for c in range(chains)]
        q0s = [pl.multiple_of(si * rq, rq) for si in sis]
        qss = []
        for q0 in q0s:
            q2 = q_ref[0, pl.ds(q0, rq), :].astype(F32)
            qss.append(jnp.concatenate([jnp.where(first, q2, 0.0), jnp.where(first, 0.0, q2)],
                                       axis=0).astype(BF16))
        carries = [jnp.zeros((rr, blk), F32)] * chains
        accs = [jnp.zeros((rq, LANES), F32)] * chains
        for i in range(qb):
            carries, accs = steps(qss, [si * qb + (qb - 1 - i) for si in sis], q0s, [None] * chains,
                                  carries, accs, True)

        def cond(st):
            n, top = st[0], st[1]
            return (n <= sis[-1] * qb) & (top > SB_LOG2_FLOOR)

        def body(st):
            n = st[0]
            js, v_scales = [], []
            for c, si in enumerate(sis):
                d = si * qb - n
                if c < chains - 1:
                    v_scales.append(jnp.where(d >= 0, 1.0, 0.0).astype(BF16))
                    d = jnp.maximum(d, 0)
                else:
                    v_scales.append(None)
                js.append(d)
            cs_out, as_out = steps(qss, js, q0s, v_scales, list(st[2]), list(st[3]), False)
            return n + 1, highest(cs_out), tuple(cs_out), tuple(as_out)

        st = lax.while_loop(cond, body, (jnp.int32(1), highest(carries), tuple(carries), tuple(accs)))
        for c, si in enumerate(sis):
            acc = st[3][c]
            q0 = pl.multiple_of(si * rq, rq)
            sq = acc * acc
            s0 = jnp.sum(jnp.where(first, sq, 0.0), axis=-1, keepdims=True)
            s1 = jnp.sum(jnp.where(first, 0.0, sq), axis=-1, keepdims=True)
            ms = jnp.where(first, s0, s1) * (1.0 / dh)
            o_ref[0, pl.ds(q0, rq), :] = (acc * lax.rsqrt(ms + EPS) * nw_ref[...]).astype(o_ref.dtype)
        return 0

    lax.fori_loop(0, n_groups, group, 0)


def _sb_attn(q, kt, v_even, v_odd, nw, *, chains, qb):
    b, s, w = q.shape
    assert s % (chains * qb * SB_BLOCK) == 0 and w % LANES == 0
    n_pairs = w // LANES
    cw = _sb_cumsum_weights()
    tok = pl.BlockSpec((1, s, LANES), lambda bi, hp: (bi, 0, hp))
    return pl.pallas_call(
        functools.partial(_sb_attn_kernel, n_groups=s // (chains * qb * SB_BLOCK), chains=chains, qb=qb),
        grid=(b, n_pairs),
        in_specs=[
            tok,
            pl.BlockSpec((1, LANES, s), lambda bi, hp: (bi, hp, 0)),
            tok, tok,
            pl.BlockSpec((1, LANES), lambda bi, hp: (0, hp)),
            pl.BlockSpec(cw.shape, lambda bi, hp: (0, 0)),
        ],
        out_specs=tok,
        out_shape=jax.ShapeDtypeStruct((b, s, w), BF16),
        compiler_params=pltpu.CompilerParams(
            dimension_semantics=("arbitrary", "arbitrary"), vmem_limit_bytes=VMEM_LIMIT),
        name="sb_attn",
    )(q, kt, v_even, v_odd, nw, cw)


def _hgrn_levels():
    hs = []
    h = HG_CHUNK // 2
    while h >= 1:
        hs.append(h)
        h //= 2
    return hs


def _hgrn_consts():
    n = HG_CHUNK
    t = np.arange(n)[:, None]
    j = np.arange(n)[None, :]
    mats = [j <= t]
    masks = []
    for h in _hgrn_levels():
        mid = (t // (2 * h)) * (2 * h) + h - 1
        upper = (t % (2 * h)) >= h
        if h > 1:
            mats.append(np.where(upper, (j > mid) & (j <= t), (j > t) & (j <= mid)))
        masks.append(((t // (2 * h)) == (j // (2 * h))) & upper & ((j % (2 * h)) < h))
    masks.append(t == j)
    sums = np.concatenate(mats, axis=0).astype(np.float32)
    sums = np.concatenate([sums, sums], axis=1)
    pm = np.concatenate(masks, axis=0).astype(np.float32)
    return jnp.asarray(sums, dtype=BF16), jnp.asarray(pm, dtype=F32)


def _hgrn2_kernel(q_ref, gl_ref, k_ref, v_ref, gate_ref, nw_ref, cs_ref, pm_ref, o_ref,
                  *, n_iters, chunks_per_iter):
    n = HG_CHUNK
    dk = HG_HEAD_DIM
    heads = 2
    levels = _hgrn_levels()
    rows = lax.broadcasted_iota(jnp.int32, (n, heads * dk), 0)
    odd = (rows & 1) != 0

    def body(it, states):
        base = it * (chunks_per_iter * n)
        cs = cs_ref[...]
        half = cs.shape[0] // 2
        units = []
        for c in range(chunks_per_iter):
            sl = pl.ds(pl.multiple_of(base + c * n, n), n)
            g = gl_ref[0, sl, :]
            g_parts = jnp.concatenate(_split2(g), axis=0)
            d = jnp.concatenate(
                [jnp.dot(cs[:half], g_parts, preferred_element_type=F32),
                 jnp.dot(cs[half:], g_parts, preferred_element_type=F32)], axis=0)
            units.append(dict(sl=sl, g=g, d=d, q=q_ref[0, sl, :], k=k_ref[0, sl, :], v=v_ref[0, sl, :]))
        for un in units:
            q, k, g, d = un["q"], un["k"], un["g"], un["d"]
            bc = d[0:n]
            b_last = bc[n - 1:n, :]
            ws = []
            for li, h in enumerate(levels):
                dl = d[(1 + li) * n:(2 + li) * n] if h > 1 else jnp.where(odd, g, 0.0)
                upper = (rows & h) != 0
                ws.append((jnp.where(upper, q, k) * jnp.exp(dl)).astype(BF16))
            un["ws"] = ws
            un["qk"] = q * k
            un["qd"] = (q * jnp.exp(bc)).astype(BF16)
            un["kd"] = (k * jnp.exp(b_last - bc)).astype(BF16)
            un["decay"] = jnp.exp(b_last)
        hslices = [slice(hh * dk, (hh + 1) * dk) for hh in range(heads)]
        for un in units:
            un["gram"] = [[lax.dot_general(w[:, ls], w[:, ls], _NT, preferred_element_type=F32)
                           for w in un["ws"]] for ls in hslices]
            un["kv"] = [lax.dot_general(un["v"][:, ls], un["kd"][:, ls], _TN, preferred_element_type=F32)
                        for ls in hslices]
        for un in units:
            un["st"] = states
            states = tuple(states[hh] * un["decay"][:, ls] + un["kv"][hh] for hh, ls in enumerate(hslices))
            ps = []
            for hh, ls in enumerate(hslices):
                p = pm_ref[len(levels) * n:(len(levels) + 1) * n, :] * jnp.sum(un["qk"][:, ls], axis=1, keepdims=True)
                for li in range(len(levels)):
                    p = p + pm_ref[li * n:(li + 1) * n, :] * un["gram"][hh][li]
                ps.append(p.astype(BF16))
            un["p"] = ps
        for un in units:
            un["o"] = [jnp.dot(un["p"][hh], un["v"][:, ls], preferred_element_type=F32)
                       + lax.dot_general(un["qd"][:, ls], un["st"][hh].astype(BF16), _NT,
                                         preferred_element_type=F32)
                       for hh, ls in enumerate(hslices)]
        for un in units:
            outs = []
            for o in un["o"]:
                ms = jnp.mean(o * o, axis=-1, keepdims=True)
                outs.append(o * lax.rsqrt(ms + EPS))
            o2 = jnp.concatenate(outs, axis=1)
            o_ref[0, un["sl"], :] = (o2 * nw_ref[...] * gate_ref[0, un["sl"], :]).astype(o_ref.dtype)
        return states

    zero = jnp.zeros((dk, dk), F32)
    lax.fori_loop(0, n_iters, body, (zero, zero))


def _hgrn2(q, gl, k, v, gate, nw, *, chunks_per_iter):
    b, s, w = q.shape
    assert s % (HG_CHUNK * chunks_per_iter) == 0 and w % (2 * HG_HEAD_DIM) == 0
    pair = 2 * HG_HEAD_DIM
    heads = w // pair
    cs, pm = _hgrn_consts()
    tok = pl.BlockSpec((1, s, pair), lambda bi, h: (bi, 0, h))
    return pl.pallas_call(
        functools.partial(_hgrn2_kernel, n_iters=s // (HG_CHUNK * chunks_per_iter),
                          chunks_per_iter=chunks_per_iter),
        grid=(b, heads),
        in_specs=[tok, tok, tok, tok, tok,
                  pl.BlockSpec((1, pair), lambda bi, h: (0, h)),
                  pl.BlockSpec(cs.shape, lambda bi, h: (0, 0)),
                  pl.BlockSpec(pm.shape, lambda bi, h: (0, 0))],
        out_specs=tok,
        out_shape=jax.ShapeDtypeStruct((b, s, w), BF16),
        compiler_params=pltpu.CompilerParams(
            dimension_semantics=("arbitrary", "arbitrary"), vmem_limit_bytes=VMEM_LIMIT),
        name="hgrn2",
    )(q, gl, k, v, gate, nw, cs, pm)


def _out_route_kernel(x_ref, sbo_ref, hgo_ref, wo_ref, nw_ref, wr_ref, br_ref, su_ref,
                      h1_ref, m_ref, meta_ref, gates_ref, counts_ref, carry_ref, *, sbw):
    i = pl.program_id(0)

    @pl.when(i == 0)
    def _():
        carry_ref[...] = jnp.zeros_like(carry_ref)

    h1 = (x_ref[...]
          + jnp.dot(sbo_ref[...], wo_ref[0:sbw, :], preferred_element_type=F32)
          + jnp.dot(hgo_ref[...], wo_ref[sbw:, :], preferred_element_type=F32))
    h1_ref[...] = h1
    ms = jnp.mean(h1 * h1, axis=-1, keepdims=True)
    m = h1 * lax.rsqrt(ms + EPS) * nw_ref[...]
    _store_row_tiles(m_ref, m)

    m2 = _split2(m)
    w2 = _split2(wr_ref[...])
    logits = br_ref[...]
    for wi, mi in ((0, 0), (0, 1), (1, 0)):
        logits = logits + lax.dot_general(w2[wi], m2[mi], _NT, preferred_element_type=F32)
    rid = lax.broadcasted_iota(jnp.int32, logits.shape, 0)
    neg = -jnp.inf
    big = jnp.int32(2 * LANES)

    def first_argmax(vals):
        vmax = jnp.max(vals, axis=0, keepdims=True)
        idx = jnp.min(jnp.where(vals == vmax, rid, big), axis=0, keepdims=True)
        return vmax, idx

    is_group = rid < N_GROUPS
    gmax, g_idx = first_argmax(jnp.where(is_group, logits, neg))
    gsum = jnp.sum(jnp.where(is_group, jnp.exp(logits - gmax), 0.0), axis=0, keepdims=True)
    g_prob = 1.0 / gsum
    lo_row = ROUTER_ROW0 + EXPERTS_PER_GROUP * g_idx
    el = jnp.where((rid >= lo_row) & (rid < lo_row + EXPERTS_PER_GROUP), logits, neg)
    v1, i1 = first_argmax(el)
    v2, i2 = first_argmax(jnp.where(rid == i1, neg, el))
    dd = jnp.exp(v2 - v1)
    p1 = 1.0 / (1.0 + dd)
    g1 = p1 * g_prob
    g2 = dd * p1 * g_prob

    hit1 = rid == i1
    hit2 = rid == i2
    onehot = jnp.where(hit1 | hit2, 1.0, 0.0)
    before_cnt = carry_ref[...] + jnp.dot(onehot.astype(BF16), su_ref[...], preferred_element_type=F32)
    r1 = jnp.sum(jnp.where(hit1, before_cnt, 0.0), axis=0, keepdims=True)
    r2 = jnp.sum(jnp.where(hit2, before_cnt, 0.0), axis=0, keepdims=True)
    carry_ref[...] = carry_ref[...] + jnp.sum(onehot, axis=1, keepdims=True)
    counts_ref[...] = carry_ref[...]

    meta_ref[...] = jnp.zeros_like(meta_ref)
    meta_ref[0:1, :] = i1 - ROUTER_ROW0
    meta_ref[1:2, :] = i2 - ROUTER_ROW0
    meta_ref[2:3, :] = r1.astype(jnp.int32)
    meta_ref[3:4, :] = r2.astype(jnp.int32)
    gt = jnp.where(rid == 0, g1, jnp.where(rid == 1, g2, 0.0))
    gates_ref[...] = gt.T


def _out_route(x2, sbo, hgo, w_out, nw, wr_t, br, *, tm):
    n, d = x2.shape
    assert d == ROW_CHUNKS * LANES
    sbw = sbo.shape[1]
    j = np.arange(tm)[:, None]
    t = np.arange(tm)[None, :]
    su = jnp.asarray((j < t).astype(np.float32), dtype=BF16)
    tok = lambda width: pl.BlockSpec((tm, width), lambda i: (i, 0))
    const = lambda shape: pl.BlockSpec(shape, lambda i: (0,) * len(shape))
    return pl.pallas_call(
        functools.partial(_out_route_kernel, sbw=sbw),
        grid=(n // tm,),
        in_specs=[tok(d), tok(sbw), tok(hgo.shape[1]), const(w_out.shape), const((1, d)),
                  const(wr_t.shape), const(br.shape), const(su.shape)],
        out_specs=[tok(d), pl.BlockSpec((tm * ROW_CHUNKS, LANES), lambda i: (i, 0)),
                   pl.BlockSpec((8, tm), lambda i: (0, i)), tok(LANES), const((LANES, 1))],
        out_shape=[jax.ShapeDtypeStruct((n, d), F32), jax.ShapeDtypeStruct((n * ROW_CHUNKS, LANES), F32),
                   jax.ShapeDtypeStruct((8, n), jnp.int32), jax.ShapeDtypeStruct((n, LANES), F32),
                   jax.ShapeDtypeStruct((LANES, 1), F32)],
        scratch_shapes=[pltpu.VMEM((LANES, 1), F32)],
        compiler_params=pltpu.CompilerParams(
            dimension_semantics=("arbitrary",), vmem_limit_bytes=VMEM_LIMIT),
        name="out_route",
    )(x2, sbo, hgo, w_out, nw, wr_t, br, su)


def _positions_kernel(seg_ref, meta_ref, pos_ref):
    e = meta_ref[0:TOP_K, :]
    start = jnp.zeros(e.shape, jnp.int32)
    for x in range(N_EXPERTS):
        start = jnp.where(e == x, seg_ref[x], start)
    pos_ref[...] = jnp.zeros_like(pos_ref)
    pos_ref[0:TOP_K, :] = start + meta_ref[TOP_K:2 * TOP_K, :]


def _positions(seg, meta):
    return pl.pallas_call(
        _positions_kernel,
        in_specs=[pl.BlockSpec(memory_space=pltpu.SMEM), pl.BlockSpec(memory_space=pltpu.VMEM)],
        out_specs=pl.BlockSpec(memory_space=pltpu.VMEM),
        out_shape=jax.ShapeDtypeStruct(meta.shape, jnp.int32),
        compiler_params=pltpu.CompilerParams(vmem_limit_bytes=VMEM_LIMIT),
        name="positions",
    )(seg, meta)


def _dispatch_kernel(seg_ref, pos_ref, m_ref, xs_ref, zbuf, sem, zsem, *, tm, tmm, n_tiles):
    i = pl.program_id(0)

    rc = ROW_CHUNKS

    def zero_copy(e):
        tail = pl.multiple_of((seg_ref[N_EXPERTS + e] - tmm) * rc, tmm * rc)
        return pltpu.make_async_copy(zbuf, xs_ref.at[pl.ds(tail, tmm * rc), :], zsem)

    @pl.when(i == 0)
    def _():
        zbuf[...] = jnp.zeros_like(zbuf)
        for e in range(N_EXPERTS):
            @pl.when(seg_ref[2 * N_EXPERTS + e] > 0)
            def _():
                zero_copy(e).start()
        for e in range(N_EXPERTS):
            @pl.when(seg_ref[2 * N_EXPERTS + e] > 0)
            def _():
                zero_copy(e).wait()

        def unused_copy(t):
            first = pl.multiple_of(t * (tmm * rc), tmm * rc)
            return pltpu.make_async_copy(zbuf, xs_ref.at[pl.ds(first, tmm * rc), :], zsem)

        def start_unused(t, _):
            unused_copy(t).start()
            return 0

        def wait_unused(t, _):
            unused_copy(t).wait()
            return 0

        lax.fori_loop(seg_ref[3 * N_EXPERTS], n_tiles, start_unused, 0)
        lax.fori_loop(seg_ref[3 * N_EXPERTS], n_tiles, wait_unused, 0)

    for r in range(tm):
        src = m_ref.at[pl.ds(r * rc, rc), :]
        for k in range(TOP_K):
            dst = pl.multiple_of(pos_ref[k, r] * rc, rc)
            pltpu.make_async_copy(src, xs_ref.at[pl.ds(dst, rc), :], sem).start(priority=k)
    for _ in range(TOP_K):
        pltpu.make_async_copy(m_ref, xs_ref.at[pl.ds(0, tm * rc), :], sem).wait()


def _dispatch(seg, pos, m, n_rows, *, tm, tmm):
    n = m.shape[0] // ROW_CHUNKS
    return pl.pallas_call(
        functools.partial(_dispatch_kernel, tm=tm, tmm=tmm, n_tiles=n_rows // tmm),
        grid=(n // tm,),
        in_specs=[pl.BlockSpec(memory_space=pltpu.SMEM),
                  pl.BlockSpec((8, tm), lambda i: (0, i), memory_space=pltpu.SMEM),
                  pl.BlockSpec((tm * ROW_CHUNKS, LANES), lambda i: (i, 0))],
        out_specs=pl.BlockSpec(memory_space=pl.ANY),
        out_shape=jax.ShapeDtypeStruct((n_rows * ROW_CHUNKS, LANES), F32),
        scratch_shapes=[pltpu.VMEM((tmm * ROW_CHUNKS, LANES), F32), pltpu.SemaphoreType.DMA(()),
                        pltpu.SemaphoreType.DMA(())],
        compiler_params=pltpu.CompilerParams(
            dimension_semantics=("arbitrary",), vmem_limit_bytes=VMEM_LIMIT),
        name="dispatch",
    )(seg, pos, m)


def _experts_kernel(te_ref, ts_ref, xs_ref, wg_ref, wu_ref, wd_ref, y_ref, wg_b, wu_b, wd_b, *, tmm):
    t = pl.program_id(0)

    @pl.when((t == 0) | (te_ref[t] != te_ref[jnp.maximum(t - 1, 0)]))
    def _():
        wg_b[...] = wg_ref[0].astype(BF16)
        wu_b[...] = wu_ref[0].astype(BF16)
        wd_b[...] = wd_ref[0].astype(BF16)

    @pl.when(ts_ref[t] == t)
    def _():
        x = _load_row_tiles(xs_ref, tmm).astype(BF16)
        hg = jnp.dot(x, wg_b[...], preferred_element_type=F32)
        hu = jnp.dot(x, wu_b[...], preferred_element_type=F32)
        act = (hg * _sigmoid(hg) * hu).astype(BF16)
        _store_row_tiles(y_ref, jnp.dot(act, wd_b[...], preferred_element_type=F32))

    @pl.when(ts_ref[t] != t)
    def _():
        y_ref[...] = jnp.zeros_like(y_ref)


def _experts(tile_expert, tile_src, xs, wg, wu, wd, *, tmm):
    n_rows = xs.shape[0] // ROW_CHUNKS
    d, de = wg.shape[1], wg.shape[2]
    assert d == ROW_CHUNKS * LANES
    rows = pl.BlockSpec((tmm * ROW_CHUNKS, LANES), lambda t, te, ts: (ts[t], 0))
    grid_spec = pltpu.PrefetchScalarGridSpec(
        num_scalar_prefetch=2,
        grid=(n_rows // tmm,),
        in_specs=[rows,
                  pl.BlockSpec((1, d, de), lambda t, te, ts: (te[t], 0, 0)),
                  pl.BlockSpec((1, d, de), lambda t, te, ts: (te[t], 0, 0)),
                  pl.BlockSpec((1, de, d), lambda t, te, ts: (te[t], 0, 0))],
        out_specs=pl.BlockSpec((tmm * ROW_CHUNKS, LANES), lambda t, te, ts: (t, 0)),
        scratch_shapes=[pltpu.VMEM((d, de), BF16), pltpu.VMEM((d, de), BF16), pltpu.VMEM((de, d), BF16)],
    )
    return pl.pallas_call(
        functools.partial(_experts_kernel, tmm=tmm),
        grid_spec=grid_spec,
        out_shape=jax.ShapeDtypeStruct((n_rows * ROW_CHUNKS, LANES), F32),
        compiler_params=pltpu.CompilerParams(
            dimension_semantics=("arbitrary",), vmem_limit_bytes=VMEM_LIMIT),
        name="experts",
    )(tile_expert, tile_src, xs, wg, wu, wd)


def _combine_kernel(pos_ref, posn_ref, h1_ref, gates_ref, p_ref, pnw_ref, wpp_ref, wpg_ref, fnw_ref,
                    y_ref, o_ref, ybuf, sem, *, tm, n_steps):
    i = pl.program_id(0)

    rc = ROW_CHUNKS

    def gather(p_ref_, slot):
        for r in range(tm):
            for k in range(TOP_K):
                src = pl.multiple_of(p_ref_[k, r] * rc, rc)
                pltpu.make_async_copy(y_ref.at[pl.ds(src, rc), :],
                                      ybuf.at[slot, k, pl.ds(r * rc, rc), :], sem.at[slot]).start(priority=k)

    def wait(slot):
        for k in range(TOP_K):
            pltpu.make_async_copy(y_ref.at[pl.ds(0, tm * rc), :], ybuf.at[slot, k], sem.at[slot]).wait()

    @pl.when(i == 0)
    def _():
        gather(pos_ref, 0)

    def step(slot):
        wait(slot)
        gather(posn_ref, 1 - slot)
        gts = gates_ref[...]
        h2 = (h1_ref[...] + gts[:, 0:1] * _load_row_tiles(ybuf.at[slot, 0], tm)
              + gts[:, 1:2] * _load_row_tiles(ybuf.at[slot, 1], tm))
        e = jnp.dot(p_ref[...].astype(BF16), wpp_ref[...], preferred_element_type=F32)
        ms = jnp.mean(h2 * h2, axis=-1, keepdims=True)
        hn = (h2 * lax.rsqrt(ms + EPS) * pnw_ref[...]).astype(BF16)
        gate = _sigmoid(jnp.dot(hn, wpg_ref[...], preferred_element_type=F32))
        h3 = h2 + gate * e
        ms3 = jnp.mean(h3 * h3, axis=-1, keepdims=True)
        o_ref[...] = h3 * lax.rsqrt(ms3 + EPS) * fnw_ref[...]

        @pl.when(i == n_steps - 1)
        def _():
            wait(1 - slot)

    for parity in range(2):
        @pl.when(i % 2 == parity)
        def _():
            step(parity)


def _combine(pos, h1, gates, p2, pnw, wpp, wpg, fnw, y, *, tm):
    n, d = h1.shape
    n_steps = n // tm
    tok = lambda width: pl.BlockSpec((tm, width), lambda i: (i, 0))
    const = lambda shape: pl.BlockSpec(shape, lambda i: (0,) * len(shape))
    return pl.pallas_call(
        functools.partial(_combine_kernel, tm=tm, n_steps=n_steps),
        grid=(n_steps,),
        in_specs=[pl.BlockSpec((8, tm), lambda i: (0, i), memory_space=pltpu.SMEM),
                  pl.BlockSpec((8, tm), lambda i: (0, jnp.minimum(i + 1, n_steps - 1)),
                               memory_space=pltpu.SMEM),
                  tok(d), tok(LANES), tok(p2.shape[1]), const((1, d)), const(wpp.shape),
                  const(wpg.shape), const((1, d)),
                  pl.BlockSpec(memory_space=pl.ANY)],
        out_specs=tok(d),
        out_shape=jax.ShapeDtypeStruct((n, d), F32),
        scratch_shapes=[pltpu.VMEM((2, TOP_K, tm * ROW_CHUNKS, LANES), F32), pltpu.SemaphoreType.DMA((2,))],
        compiler_params=pltpu.CompilerParams(
            dimension_semantics=("arbitrary",), vmem_limit_bytes=VMEM_LIMIT),
        name="combine",
    )(pos, pos, h1, gates, p2, pnw, wpp, wpg, fnw, y)


def kernel(x, p, attn_norm_w, w_in, sb_norm_w, hg_lower_bounds, hg_norm_w, w_out, ffn_norm_w,
           w_group_router, b_group_router, w_expert_router, b_expert_router, w_exp_gate, w_exp_up,
           w_exp_down, ple_norm_w, w_ple_proj, w_ple_gate, final_norm_w):
    b, s, d = x.shape
    depth = w_in.shape[0]
    assert depth == 1, "single-layer trunk"
    sbw = sb_norm_w.shape[1]
    hgw = hg_norm_w.shape[1]
    n = b * s
    tm_proj = min(512, s)
    tm_route = min(512, n)
    tm_disp = min(512, n)
    tm_comb = min(256, n)
    tmm = min(512, n)

    wi = w_in[0]
    w_main = jnp.concatenate([wi[:, 0:sbw], wi[:, 2 * sbw:]], axis=1).astype(BF16)
    w_kt = wi[:, sbw:2 * sbw].T.astype(BF16)
    gap = ROUTER_ROW0 - N_GROUPS
    tail = LANES - ROUTER_ROW0 - N_EXPERTS
    wr_t = jnp.concatenate([w_group_router[0].T, jnp.zeros((gap, d), F32), w_expert_router[0].T,
                            jnp.zeros((tail, d), F32)], axis=0)
    br = jnp.concatenate([b_group_router[0], jnp.zeros((gap,), F32), b_expert_router[0],
                          jnp.zeros((tail,), F32)])[:, None]

    sbq, sbkt, sbve, sbvo, hq, hgl, hk, hv, hgate = _in_proj(
        x, attn_norm_w[0][None, :], w_main, w_kt, hg_lower_bounds, tm=tm_proj)
    sbo = _sb_attn(sbq, sbkt, sbve, sbvo, sb_norm_w, chains=SB_CHAINS, qb=SB_QUERY_BLOCKS)
    hgo = _hgrn2(hq, hgl, hk, hv, hgate, hg_norm_w, chunks_per_iter=HG_CHUNKS_PER_ITER)

    h1, m, meta, gates, counts = _out_route(
        x.reshape(n, d), sbo.reshape(n, sbw), hgo.reshape(n, hgw), w_out[0].astype(BF16),
        ffn_norm_w[0][None, :], wr_t, br, tm=tm_route)

    cnt = counts[ROUTER_ROW0:ROUTER_ROW0 + N_EXPERTS, 0].astype(jnp.int32)
    padded = ((cnt + tmm - 1) // tmm) * tmm
    ends = jnp.cumsum(padded)
    starts = ends - padded
    n_rows = n * TOP_K + N_EXPERTS * tmm
    n_tiles = n_rows // tmm
    last_tile = ends[-1] // tmm - 1
    tile_src = jnp.minimum(jnp.arange(n_tiles, dtype=jnp.int32), last_tile)
    tile_expert = jnp.sum((ends[None, :] <= (tile_src * tmm)[:, None]).astype(jnp.int32), axis=1)
    seg = jnp.concatenate([starts, ends, padded, (last_tile + 1)[None],
                           jnp.zeros((LANES - 3 * N_EXPERTS - 1,), jnp.int32)])

    pos = _positions(seg, meta)
    xs = _dispatch(seg, pos, m, n_rows, tm=tm_disp, tmm=tmm)
    y = _experts(tile_expert, tile_src, xs, w_exp_gate[0], w_exp_up[0], w_exp_down[0], tmm=tmm)
    out = _combine(pos, h1, gates, p[0].reshape(n, -1), ple_norm_w[0][None, :],
                   w_ple_proj[0].astype(BF16), w_ple_gate[0].astype(BF16), final_norm_w[None, :],
                   y, tm=tm_comb)
    return out.reshape(b, s, d)
```

```python
import functools

import numpy as np
import jax
import jax.numpy as jnp
from jax import lax
from jax.experimental import pallas as pl
from jax.experimental.pallas import tpu as pltpu

F32 = jnp.float32
BF16 = jnp.bfloat16
EPS = 1e-6

SB_HEADS = 8
SB_HEAD_DIM = 64
HG_HEAD_DIM = 128
HG_CHUNK = 64
HG_CHUNKS_PER_ITER = 8
N_GROUPS = 4
EXPERTS_PER_GROUP = 8
N_EXPERTS = N_GROUPS * EXPERTS_PER_GROUP
TOP_K = 2
LANES = 128
ROW_CHUNKS = 8
ROUTER_ROW0 = 8
SB_BLOCK = 128
SB_QUERY_BLOCKS = 1
SB_CHAINS = 8
SB_LOG2_FLOOR = -152.0
LOG2E = 1.4426950408889634
VMEM_LIMIT = 56 * 1024 * 1024

_NT = (((1,), (1,)), ((), ()))
_TN = (((0,), (0,)), ((), ()))


def _sigmoid(x):
    return 1.0 / (1.0 + jnp.exp(-x))


def _store_row_tiles(ref, x):
    rows = x.shape[0]
    for c in range(ROW_CHUNKS):
        ref[pl.ds(c, rows, stride=ROW_CHUNKS), :] = x[:, c * LANES:(c + 1) * LANES]


def _load_row_tiles(ref, rows):
    return jnp.concatenate([ref[pl.ds(c, rows, stride=ROW_CHUNKS), :] for c in range(ROW_CHUNKS)], axis=1)


def _split2(x):
    hi = x.astype(BF16)
    lo = (x - hi.astype(F32)).astype(BF16)
    return hi, lo


def _in_proj_kernel(x_ref, nw_ref, w_ref, wkt_ref, lbp_ref,
                    sbq_ref, sbkt_ref, sbve_ref, sbvo_ref, hq_ref, hgl_ref, hk_ref, hv_ref, hgate_ref,
                    *, sbw, hgw):
    x = x_ref[0]
    ms = jnp.mean(x * x, axis=-1, keepdims=True)
    a = (x * lax.rsqrt(ms + EPS) * nw_ref[...]).astype(BF16)

    def seg(lo, width):
        return jnp.dot(a, w_ref[:, lo:lo + width], preferred_element_type=F32)

    sbq_ref[0] = (seg(0, sbw) * (SB_HEAD_DIM ** -0.5 * LOG2E)).astype(BF16)
    sbkt_ref[0] = lax.dot_general(wkt_ref[...], a, _NT, preferred_element_type=F32).astype(BF16)
    v = seg(sbw, sbw)
    even_head = (lax.broadcasted_iota(jnp.int32, v.shape, 1) & SB_HEAD_DIM) == 0
    sbve_ref[0] = jnp.where(even_head, v, 0.0).astype(BF16)
    sbvo_ref[0] = jnp.where(even_head, 0.0, v).astype(BF16)
    c = 2 * sbw
    q = seg(c, hgw)
    hq_ref[0] = q * _sigmoid(q)
    p0 = lbp_ref[0:1, :]
    p1 = lbp_ref[1:2, :]
    pm = jnp.maximum(p0, p1)
    e0 = jnp.exp(p0 - pm)
    e1 = jnp.exp(p1 - pm)
    lb = e0 / (e0 + e1)
    fz = seg(c + hgw, hgw)
    sg = _sigmoid(fz)
    f = lb + (1.0 - lb) * sg
    hgl_ref[0] = jnp.log(f)
    hk_ref[0] = 1.0 - f
    hv_ref[0] = seg(c + 2 * hgw, hgw).astype(BF16)
    g = seg(c + 3 * hgw, hgw)
    hgate_ref[0] = g * _sigmoid(g)


def _in_proj(x, nw, w_main, w_kt, lbp, *, tm):
    b, s, d = x.shape
    sbw = w_kt.shape[0]
    hgw = (w_main.shape[1] - 2 * sbw) // 4
    tok = lambda width: pl.BlockSpec((1, tm, width), lambda bi, i: (bi, i, 0))
    const = lambda shape: pl.BlockSpec(shape, lambda bi, i: (0,) * len(shape))
    out_shape = [
        jax.ShapeDtypeStruct((b, s, sbw), BF16),
        jax.ShapeDtypeStruct((b, sbw, s), BF16),
        jax.ShapeDtypeStruct((b, s, sbw), BF16),
        jax.ShapeDtypeStruct((b, s, sbw), BF16),
        jax.ShapeDtypeStruct((b, s, hgw), F32),
        jax.ShapeDtypeStruct((b, s, hgw), F32),
        jax.ShapeDtypeStruct((b, s, hgw), F32),
        jax.ShapeDtypeStruct((b, s, hgw), BF16),
        jax.ShapeDtypeStruct((b, s, hgw), F32),
    ]
    out_specs = [tok(sbw), pl.BlockSpec((1, sbw, tm), lambda bi, i: (bi, 0, i)), tok(sbw), tok(sbw),
                 tok(hgw), tok(hgw), tok(hgw), tok(hgw), tok(hgw)]
    return pl.pallas_call(
        functools.partial(_in_proj_kernel, sbw=sbw, hgw=hgw),
        grid=(b, s // tm),
        in_specs=[tok(d), const((1, d)), const(w_main.shape), const(w_kt.shape), const(lbp.shape)],
        out_specs=out_specs,
        out_shape=out_shape,
        compiler_params=pltpu.CompilerParams(
            dimension_semantics=("arbitrary", "arbitrary"), vmem_limit_bytes=VMEM_LIMIT),
        name="in_proj",
    )(x, nw, w_main, w_kt, lbp)


def _sb_cumsum_weights():
    j = np.arange(SB_BLOCK)[:, None]
    s = np.arange(SB_BLOCK)[None, :]
    half = np.concatenate([(j > s).astype(np.float32), np.ones((SB_BLOCK, SB_BLOCK), np.float32)], axis=1)
    return jnp.asarray(np.concatenate([half, half], axis=0), dtype=BF16)


def _sb_attn_kernel(q_ref, kt_ref, ve_ref, vo_ref, nw_ref, cw_ref, o_ref, *, n_groups, chains, qb):
    blk = SB_BLOCK
    dh = SB_HEAD_DIM
    rq = qb * blk
    rr = 2 * rq
    first = lax.broadcasted_iota(jnp.int32, (rq, LANES), 1) < dh
    rowpos = lax.broadcasted_iota(jnp.int32, (rr, blk), 0) & (rq - 1)
    colpos = lax.broadcasted_iota(jnp.int32, (rr, blk), 1)
    cw = cw_ref[...]

    def steps(qss, js, q0s, v_scales, carries, accs, masked):
        k0s = [pl.multiple_of(j * blk, blk) for j in js]
        zs = [jnp.dot(qs, kt_ref[0, :, pl.ds(k0, blk)], preferred_element_type=F32)
              for qs, k0 in zip(qss, k0s)]
        lss, hls, befores = [], [], []
        for z, k0, q0 in zip(zs, k0s, q0s):
            sp = jnp.log2(1.0 + jnp.exp2(-jnp.abs(z)))
            ls = jnp.minimum(z, 0.0) - sp
            lk = ls - z
            before = None
            if masked:
                before = (k0 + colpos) < (q0 + rowpos)
                lk = jnp.where(before, lk, 0.0)
            hi, lo = _split2(lk)
            lss.append(ls)
            hls.append(jnp.concatenate([hi, lo], axis=1))
            befores.append(before)
        css = [jnp.dot(hl, cw, preferred_element_type=F32) for hl in hls]
        abs_ = []
        for ls, cs, carry, before in zip(lss, css, carries, befores):
            a = jnp.exp2(ls + cs[:, :blk] + carry)
            if masked:
                a = jnp.where(before, a, 0.0)
            ab = a.astype(BF16)
            abs_.append(jnp.concatenate([ab[:rq], ab[rq:]], axis=1))
        new_accs = []
        for ab, k0, v_scale, acc in zip(abs_, k0s, v_scales, accs):
            vst = jnp.concatenate([ve_ref[0, pl.ds(k0, blk), :], vo_ref[0, pl.ds(k0, blk), :]], axis=0)
            if v_scale is not None:
                vst = vst * v_scale
            new_accs.append(acc + jnp.dot(ab, vst, preferred_element_type=F32))
        new_carries = [carry + cs[:, blk:] for carry, cs in zip(carries, css)]
        return new_carries, new_accs

    def highest(carries):
        m = carries[0]
        for c in carries[1:]:
            m = jnp.maximum(m, c)
        return jnp.max(m)

    def group(gi, _):
        sis = [gi * chains + c for c in range(chains)]
        q0s = [pl.multiple_of(si * rq, rq) for si in sis]
        qss = []
        for q0 in q0s:
            q2 = q_ref[0, pl.ds(q0, rq), :].astype(F32)
            qss.append(jnp.concatenate([jnp.where(first, q2, 0.0), jnp.where(first, 0.0, q2)],
                                       axis=0).astype(BF16))
        carries = [jnp.zeros((rr, blk), F32)] * chains
        accs = [jnp.zeros((rq, LANES), F32)] * chains
        for i in range(qb):
            carries, accs = steps(qss, [si * qb + (qb - 1 - i) for si in sis], q0s, [None] * chains,
                                  carries, accs, True)

        def cond(st):
            n, top = st[0], st[1]
            return (n <= sis[-1] * qb) & (top > SB_LOG2_FLOOR)

        def body(st):
            n = st[0]
            js, v_scales = [], []
            for c, si in enumerate(sis):
                d = si * qb - n
                if c < chains - 1:
                    v_scales.append(jnp.where(d >= 0, 1.0, 0.0).astype(BF16))
                    d = jnp.maximum(d, 0)
                else:
                    v_scales.append(None)
                js.append(d)
            cs_out, as_out = steps(qss, js, q0s, v_scales, list(st[2]), list(st[3]), False)
            return n + 1, highest(cs_out), tuple(cs_out), tuple(as_out)

        st = lax.while_loop(cond, body, (jnp.int32(1), highest(carries), tuple(carries), tuple(accs)))
        for c, si in enumerate(sis):
            acc = st[3][c]
            q0 = pl.multiple_of(si * rq, rq)
            sq = acc * acc
            s0 = jnp.sum(jnp.where(first, sq, 0.0), axis=-1, keepdims=True)
            s1 = jnp.sum(jnp.where(first, 0.0, sq), axis=-1, keepdims=True)
            ms = jnp.where(first, s0, s1) * (1.0 / dh)
            o_ref[0, pl.ds(q0, rq), :] = (acc * lax.rsqrt(ms + EPS) * nw_ref[...]).astype(o_ref.dtype)
        return 0

    lax.fori_loop(0, n_groups, group, 0)


def _sb_attn(q, kt, v_even, v_odd, nw, *, chains, qb):
    b, s, w = q.shape
    assert s % (chains * qb * SB_BLOCK) == 0 and w % LANES == 0
    n_pairs = w // LANES
    cw = _sb_cumsum_weights()
    tok = pl.BlockSpec((1, s, LANES), lambda bi, hp: (bi, 0, hp))
    return pl.pallas_call(
        functools.partial(_sb_attn_kernel, n_groups=s // (chains * qb * SB_BLOCK), chains=chains, qb=qb),
        grid=(b, n_pairs),
        in_specs=[
            tok,
            pl.BlockSpec((1, LANES, s), lambda bi, hp: (bi, hp, 0)),
            tok, tok,
            pl.BlockSpec((1, LANES), lambda bi, hp: (0, hp)),
            pl.BlockSpec(cw.shape, lambda bi, hp: (0, 0)),
        ],
        out_specs=tok,
        out_shape=jax.ShapeDtypeStruct((b, s, w), BF16),
        compiler_params=pltpu.CompilerParams(
            dimension_semantics=("arbitrary", "arbitrary"), vmem_limit_bytes=VMEM_LIMIT),
        name="sb_attn",
    )(q, kt, v_even, v_odd, nw, cw)


def _hgrn_levels():
    hs = []
    h = HG_CHUNK // 2
    while h >= 1:
        hs.append(h)
        h //= 2
    return hs


def _hgrn_consts():
    n = HG_CHUNK
    t = np.arange(n)[:, None]
    j = np.arange(n)[None, :]
    mats = [j <= t]
    masks = []
    for h in _hgrn_levels():
        mid = (t // (2 * h)) * (2 * h) + h - 1
        upper = (t % (2 * h)) >= h
        if h > 1:
            mats.append(np.where(upper, (j > mid) & (j <= t), (j > t) & (j <= mid)))
        masks.append(((t // (2 * h)) == (j // (2 * h))) & upper & ((j % (2 * h)) < h))
    masks.append(t == j)
    sums = np.concatenate(mats, axis=0).astype(np.float32)
    sums = np.concatenate([sums, sums], axis=1)
    pm = np.concatenate(masks, axis=0).astype(np.float32)
    return jnp.asarray(sums, dtype=BF16), jnp.asarray(pm, dtype=F32)


def _hgrn2_kernel(q_ref, gl_ref, k_ref, v_ref, gate_ref, nw_ref, cs_ref, pm_ref, o_ref,
                  *, n_iters, chunks_per_iter):
    n = HG_CHUNK
    dk = HG_HEAD_DIM
    heads = 2
    levels = _hgrn_levels()
    rows = lax.broadcasted_iota(jnp.int32, (n, heads * dk), 0)
    odd = (rows & 1) != 0

    def body(it, states):
        base = it * (chunks_per_iter * n)
        cs = cs_ref[...]
        half = cs.shape[0] // 2
        units = []
        for c in range(chunks_per_iter):
            sl = pl.ds(pl.multiple_of(base + c * n, n), n)
            g = gl_ref[0, sl, :]
            g_parts = jnp.concatenate(_split2(g), axis=0)
            d = jnp.concatenate(
                [jnp.dot(cs[:half], g_parts, preferred_element_type=F32),
                 jnp.dot(cs[half:], g_parts, preferred_element_type=F32)], axis=0)
            units.append(dict(sl=sl, g=g, d=d, q=q_ref[0, sl, :], k=k_ref[0, sl, :], v=v_ref[0, sl, :]))
        for un in units:
            q, k, g, d = un["q"], un["k"], un["g"], un["d"]
            bc = d[0:n]
            b_last = bc[n - 1:n, :]
            ws = []
            for li, h in enumerate(levels):
                dl = d[(1 + li) * n:(2 + li) * n] if h > 1 else jnp.where(odd, g, 0.0)
                upper = (rows & h) != 0
                ws.append((jnp.where(upper, q, k) * jnp.exp(dl)).astype(BF16))
            un["ws"] = ws
            un["qk"] = q * k
            un["qd"] = (q * jnp.exp(bc)).astype(BF16)
            un["kd"] = (k * jnp.exp(b_last - bc)).astype(BF16)
            un["decay"] = jnp.exp(b_last)
        hslices = [slice(hh * dk, (hh + 1) * dk) for hh in range(heads)]
        for un in units:
            un["gram"] = [[lax.dot_general(w[:, ls], w[:, ls], _NT, preferred_element_type=F32)
                           for w in un["ws"]] for ls in hslices]
            un["kv"] = [lax.dot_general(un["v"][:, ls], un["kd"][:, ls], _TN, preferred_element_type=F32)
                        for ls in hslices]
        for un in units:
            un["st"] = states
            states = tuple(states[hh] * un["decay"][:, ls] + un["kv"][hh] for hh, ls in enumerate(hslices))
            ps = []
            for hh, ls in enumerate(hslices):
                p = pm_ref[len(levels) * n:(len(levels) + 1) * n, :] * jnp.sum(un["qk"][:, ls], axis=1, keepdims=True)
                for li in range(len(levels)):
                    p = p + pm_ref[li * n:(li + 1) * n, :] * un["gram"][hh][li]
                ps.append(p.astype(BF16))
            un["p"] = ps
        for un in units:
            un["o"] = [jnp.dot(un["p"][hh], un["v"][:, ls], preferred_element_type=F32)
                       + lax.dot_general(un["qd"][:, ls], un["st"][hh].astype(BF16), _NT,
                                         preferred_element_type=F32)
                       for hh, ls in enumerate(hslices)]
        for un in units:
            outs = []
            for o in un["o"]:
                ms = jnp.mean(o * o, axis=-1, keepdims=True)
                outs.append(o * lax.rsqrt(ms + EPS))
            o2 = jnp.concatenate(outs, axis=1)
            o_ref[0, un["sl"], :] = (o2 * nw_ref[...] * gate_ref[0, un["sl"], :]).astype(o_ref.dtype)
        return states

    zero = jnp.zeros((dk, dk), F32)
    lax.fori_loop(0, n_iters, body, (zero, zero))


def _hgrn2(q, gl, k, v, gate, nw, *, chunks_per_iter):
    b, s, w = q.shape
    assert s % (HG_CHUNK * chunks_per_iter) == 0 and w % (2 * HG_HEAD_DIM) == 0
    pair = 2 * HG_HEAD_DIM
    heads = w // pair
    cs, pm = _hgrn_consts()
    tok = pl.BlockSpec((1, s, pair), lambda bi, h: (bi, 0, h))
    return pl.pallas_call(
        functools.partial(_hgrn2_kernel, n_iters=s // (HG_CHUNK * chunks_per_iter),
                          chunks_per_iter=chunks_per_iter),
        grid=(b, heads),
        in_specs=[tok, tok, tok, tok, tok,
                  pl.BlockSpec((1, pair), lambda bi, h: (0, h)),
                  pl.BlockSpec(cs.shape, lambda bi, h: (0, 0)),
                  pl.BlockSpec(pm.shape, lambda bi, h: (0, 0))],
        out_specs=tok,
        out_shape=jax.ShapeDtypeStruct((b, s, w), BF16),
        compiler_params=pltpu.CompilerParams(
            dimension_semantics=("arbitrary", "arbitrary"), vmem_limit_bytes=VMEM_LIMIT),
        name="hgrn2",
    )(q, gl, k, v, gate, nw, cs, pm)


def _out_route_kernel(x_ref, sbo_ref, hgo_ref, wo_ref, nw_ref, wr_ref, br_ref, su_ref,
                      h1_ref, m_ref, meta_ref, gates_ref, counts_ref, carry_ref, *, sbw):
    i = pl.program_id(0)

    @pl.when(i == 0)
    def _():
        carry_ref[...] = jnp.zeros_like(carry_ref)

    h1 = (x_ref[...]
          + jnp.dot(sbo_ref[...], wo_ref[0:sbw, :], preferred_element_type=F32)
          + jnp.dot(hgo_ref[...], wo_ref[sbw:, :], preferred_element_type=F32))
    h1_ref[...] = h1
    ms = jnp.mean(h1 * h1, axis=-1, keepdims=True)
    m = h1 * lax.rsqrt(ms + EPS) * nw_ref[...]
    _store_row_tiles(m_ref, m)

    m2 = _split2(m)
    w2 = _split2(wr_ref[...])
    logits = br_ref[...]
    for wi, mi in ((0, 0), (0, 1), (1, 0)):
        logits = logits + lax.dot_general(w2[wi], m2[mi], _NT, preferred_element_type=F32)
    rid = lax.broadcasted_iota(jnp.int32, logits.shape, 0)
    neg = -jnp.inf
    big = jnp.int32(2 * LANES)

    def first_argmax(vals):
        vmax = jnp.max(vals, axis=0, keepdims=True)
        idx = jnp.min(jnp.where(vals == vmax, rid, big), axis=0, keepdims=True)
        return vmax, idx

    is_group = rid < N_GROUPS
    gmax, g_idx = first_argmax(jnp.where(is_group, logits, neg))
    gsum = jnp.sum(jnp.where(is_group, jnp.exp(logits - gmax), 0.0), axis=0, keepdims=True)
    g_prob = 1.0 / gsum
    lo_row = ROUTER_ROW0 + EXPERTS_PER_GROUP * g_idx
    el = jnp.where((rid >= lo_row) & (rid < lo_row + EXPERTS_PER_GROUP), logits, neg)
    v1, i1 = first_argmax(el)
    v2, i2 = first_argmax(jnp.where(rid == i1, neg, el))
    dd = jnp.exp(v2 - v1)
    p1 = 1.0 / (1.0 + dd)
    g1 = p1 * g_prob
    g2 = dd * p1 * g_prob

    hit1 = rid == i1
    hit2 = rid == i2
    onehot = jnp.where(hit1 | hit2, 1.0, 0.0)
    before_cnt = carry_ref[...] + jnp.dot(onehot.astype(BF16), su_ref[...], preferred_element_type=F32)
    r1 = jnp.sum(jnp.where(hit1, before_cnt, 0.0), axis=0, keepdims=True)
    r2 = jnp.sum(jnp.where(hit2, before_cnt, 0.0), axis=0, keepdims=True)
    carry_ref[...] = carry_ref[...] + jnp.sum(onehot, axis=1, keepdims=True)
    counts_ref[...] = carry_ref[...]

    meta_ref[...] = jnp.zeros_like(meta_ref)
    meta_ref[0:1, :] = i1 - ROUTER_ROW0
    meta_ref[1:2, :] = i2 - ROUTER_ROW0
    meta_ref[2:3, :] = r1.astype(jnp.int32)
    meta_ref[3:4, :] = r2.astype(jnp.int32)
    gt = jnp.where(rid == 0, g1, jnp.where(rid == 1, g2, 0.0))
    gates_ref[...] = gt.T


def _out_route(x2, sbo, hgo, w_out, nw, wr_t, br, *, tm):
    n, d = x2.shape
    assert d == ROW_CHUNKS * LANES
    sbw = sbo.shape[1]
    j = np.arange(tm)[:, None]
    t = np.arange(tm)[None, :]
    su = jnp.asarray((j < t).astype(np.float32), dtype=BF16)
    tok = lambda width: pl.BlockSpec((tm, width), lambda i: (i, 0))
    const = lambda shape: pl.BlockSpec(shape, lambda i: (0,) * len(shape))
    return pl.pallas_call(
        functools.partial(_out_route_kernel, sbw=sbw),
        grid=(n // tm,),
        in_specs=[tok(d), tok(sbw), tok(hgo.shape[1]), const(w_out.shape), const((1, d)),
                  const(wr_t.shape), const(br.shape), const(su.shape)],
        out_specs=[tok(d), pl.BlockSpec((tm * ROW_CHUNKS, LANES), lambda i: (i, 0)),
                   pl.BlockSpec((8, tm), lambda i: (0, i)), tok(LANES), const((LANES, 1))],
        out_shape=[jax.ShapeDtypeStruct((n, d), F32), jax.ShapeDtypeStruct((n * ROW_CHUNKS, LANES), F32),
                   jax.ShapeDtypeStruct((8, n), jnp.int32), jax.ShapeDtypeStruct((n, LANES), F32),
                   jax.ShapeDtypeStruct((LANES, 1), F32)],
        scratch_shapes=[pltpu.VMEM((LANES, 1), F32)],
        compiler_params=pltpu.CompilerParams(
            dimension_semantics=("arbitrary",), vmem_limit_bytes=VMEM_LIMIT),
        name="out_route",
    )(x2, sbo, hgo, w_out, nw, wr_t, br, su)


def _positions_kernel(seg_ref, meta_ref, pos_ref):
    e = meta_ref[0:TOP_K, :]
    start = jnp.zeros(e.shape, jnp.int32)
    for x in range(N_EXPERTS):
        start = jnp.where(e == x, seg_ref[x], start)
    pos_ref[...] = jnp.zeros_like(pos_ref)
    pos_ref[0:TOP_K, :] = start + meta_ref[TOP_K:2 * TOP_K, :]


def _positions(seg, meta):
    return pl.pallas_call(
        _positions_kernel,
        in_specs=[pl.BlockSpec(memory_space=pltpu.SMEM), pl.BlockSpec(memory_space=pltpu.VMEM)],
        out_specs=pl.BlockSpec(memory_space=pltpu.VMEM),
        out_shape=jax.ShapeDtypeStruct(meta.shape, jnp.int32),
        compiler_params=pltpu.CompilerParams(vmem_limit_bytes=VMEM_LIMIT),
        name="positions",
    )(seg, meta)


def _dispatch_kernel(seg_ref, pos_ref, m_ref, xs_ref, zbuf, sem, zsem, *, tm, tmm, n_tiles):
    i = pl.program_id(0)

    rc = ROW_CHUNKS

    def zero_copy(e):
        tail = pl.multiple_of((seg_ref[N_EXPERTS + e] - tmm) * rc, tmm * rc)
        return pltpu.make_async_copy(zbuf, xs_ref.at[pl.ds(tail, tmm * rc), :], zsem)

    @pl.when(i == 0)
    def _():
        zbuf[...] = jnp.zeros_like(zbuf)
        for e in range(N_EXPERTS):
            @pl.when(seg_ref[2 * N_EXPERTS + e] > 0)
            def _():
                zero_copy(e).start()
        for e in range(N_EXPERTS):
            @pl.when(seg_ref[2 * N_EXPERTS + e] > 0)
            def _():
                zero_copy(e).wait()

        def unused_copy(t):
            first = pl.multiple_of(t * (tmm * rc), tmm * rc)
            return pltpu.make_async_copy(zbuf, xs_ref.at[pl.ds(first, tmm * rc), :], zsem)

        def start_unused(t, _):
            unused_copy(t).start()
            return 0

        def wait_unused(t, _):
            unused_copy(t).wait()
            return 0

        lax.fori_loop(seg_ref[3 * N_EXPERTS], n_tiles, start_unused, 0)
        lax.fori_loop(seg_ref[3 * N_EXPERTS], n_tiles, wait_unused, 0)

    for r in range(tm):
        src = m_ref.at[pl.ds(r * rc, rc), :]
        for k in range(TOP_K):
            dst = pl.multiple_of(pos_ref[k, r] * rc, rc)
            pltpu.make_async_copy(src, xs_ref.at[pl.ds(dst, rc), :], sem).start(priority=k)
    for _ in range(TOP_K):
        pltpu.make_async_copy(m_ref, xs_ref.at[pl.ds(0, tm * rc), :], sem).wait()


def _dispatch(seg, pos, m, n_rows, *, tm, tmm):
    n = m.shape[0] // ROW_CHUNKS
    return pl.pallas_call(
        functools.partial(_dispatch_kernel, tm=tm, tmm=tmm, n_tiles=n_rows // tmm),
        grid=(n // tm,),
        in_specs=[pl.BlockSpec(memory_space=pltpu.SMEM),
                  pl.BlockSpec((8, tm), lambda i: (0, i), memory_space=pltpu.SMEM),
                  pl.BlockSpec((tm * ROW_CHUNKS, LANES), lambda i: (i, 0))],
        out_specs=pl.BlockSpec(memory_space=pl.ANY),
        out_shape=jax.ShapeDtypeStruct((n_rows * ROW_CHUNKS, LANES), F32),
        scratch_shapes=[pltpu.VMEM((tmm * ROW_CHUNKS, LANES), F32), pltpu.SemaphoreType.DMA(()),
                        pltpu.SemaphoreType.DMA(())],
        compiler_params=pltpu.CompilerParams(
            dimension_semantics=("arbitrary",), vmem_limit_bytes=VMEM_LIMIT),
        name="dispatch",
    )(seg, pos, m)


def _experts_kernel(te_ref, ts_ref, xs_ref, wg_ref, wu_ref, wd_ref, y_ref, wg_b, wu_b, wd_b, *, tmm):
    t = pl.program_id(0)

    @pl.when((t == 0) | (te_ref[t] != te_ref[jnp.maximum(t - 1, 0)]))
    def _():
        wg_b[...] = wg_ref[0].astype(BF16)
        wu_b[...] = wu_ref[0].astype(BF16)
        wd_b[...] = wd_ref[0].astype(BF16)

    @pl.when(ts_ref[t] == t)
    def _():
        x = _load_row_tiles(xs_ref, tmm).astype(BF16)
        hg = jnp.dot(x, wg_b[...], preferred_element_type=F32)
        hu = jnp.dot(x, wu_b[...], preferred_element_type=F32)
        act = (hg * _sigmoid(hg) * hu).astype(BF16)
        _store_row_tiles(y_ref, jnp.dot(act, wd_b[...], preferred_element_type=F32))

    @pl.when(ts_ref[t] != t)
    def _():
        y_ref[...] = jnp.zeros_like(y_ref)


def _experts(tile_expert, tile_src, xs, wg, wu, wd, *, tmm):
    n_rows = xs.shape[0] // ROW_CHUNKS
    d, de = wg.shape[1], wg.shape[2]
    assert d == ROW_CHUNKS * LANES
    rows = pl.BlockSpec((tmm * ROW_CHUNKS, LANES), lambda t, te, ts: (ts[t], 0))
    grid_spec = pltpu.PrefetchScalarGridSpec(
        num_scalar_prefetch=2,
        grid=(n_rows // tmm,),
        in_specs=[rows,
                  pl.BlockSpec((1, d, de), lambda t, te, ts: (te[t], 0, 0)),
                  pl.BlockSpec((1, d, de), lambda t, te, ts: (te[t], 0, 0)),
                  pl.BlockSpec((1, de, d), lambda t, te, ts: (te[t], 0, 0))],
        out_specs=pl.BlockSpec((tmm * ROW_CHUNKS, LANES), lambda t, te, ts: (t, 0)),
        scratch_shapes=[pltpu.VMEM((d, de), BF16), pltpu.VMEM((d, de), BF16), pltpu.VMEM((de, d), BF16)],
    )
    return pl.pallas_call(
        functools.partial(_experts_kernel, tmm=tmm),
        grid_spec=grid_spec,
        out_shape=jax.ShapeDtypeStruct((n_rows * ROW_CHUNKS, LANES), F32),
        compiler_params=pltpu.CompilerParams(
            dimension_semantics=("arbitrary",), vmem_limit_bytes=VMEM_LIMIT),
        name="experts",
    )(tile_expert, tile_src, xs, wg, wu, wd)


def _combine_kernel(pos_ref, posn_ref, h1_ref, gates_ref, p_ref, pnw_ref, wpp_ref, wpg_ref, fnw_ref,
                    y_ref, o_ref, ybuf_even, ybuf_odd, sem, *, tm, n_steps):
    i = pl.program_id(0)

    rc = ROW_CHUNKS
    ybufs = (ybuf_even, ybuf_odd)

    def gather(p_ref_, slot):
        for r in range(tm):
            for k in range(TOP_K):
                src = pl.multiple_of(p_ref_[k, r] * rc, rc)
                pltpu.make_async_copy(y_ref.at[pl.ds(src, rc), :],
                                      ybufs[slot].at[k, pl.ds(r * rc, rc), :], sem.at[slot]).start(priority=k)

    def wait(slot):
        for k in range(TOP_K):
            pltpu.make_async_copy(y_ref.at[pl.ds(0, tm * rc), :], ybufs[slot].at[k], sem.at[slot]).wait()

    @pl.when(i == 0)
    def _():
        gather(pos_ref, 0)

    def step(slot):
        wait(slot)
        gather(posn_ref, 1 - slot)
        gts = gates_ref[...]
        h2 = (h1_ref[...] + gts[:, 0:1] * _load_row_tiles(ybufs[slot].at[0], tm)
              + gts[:, 1:2] * _load_row_tiles(ybufs[slot].at[1], tm))
        e = jnp.dot(p_ref[...].astype(BF16), wpp_ref[...], preferred_element_type=F32)
        ms = jnp.mean(h2 * h2, axis=-1, keepdims=True)
        hn = (h2 * lax.rsqrt(ms + EPS) * pnw_ref[...]).astype(BF16)
        gate = _sigmoid(jnp.dot(hn, wpg_ref[...], preferred_element_type=F32))
        h3 = h2 + gate * e
        ms3 = jnp.mean(h3 * h3, axis=-1, keepdims=True)
        o_ref[...] = h3 * lax.rsqrt(ms3 + EPS) * fnw_ref[...]

        @pl.when(i == n_steps - 1)
        def _():
            wait(1 - slot)

    for parity in range(2):
        @pl.when(i % 2 == parity)
        def _():
            step(parity)


def _combine(pos, h1, gates, p2, pnw, wpp, wpg, fnw, y, *, tm):
    n, d = h1.shape
    n_steps = n // tm
    tok = lambda width: pl.BlockSpec((tm, width), lambda i: (i, 0))
    const = lambda shape: pl.BlockSpec(shape, lambda i: (0,) * len(shape))
    return pl.pallas_call(
        functools.partial(_combine_kernel, tm=tm, n_steps=n_steps),
        grid=(n_steps,),
        in_specs=[pl.BlockSpec((8, tm), lambda i: (0, i), memory_space=pltpu.SMEM),
                  pl.BlockSpec((8, tm), lambda i: (0, jnp.minimum(i + 1, n_steps - 1)),
                               memory_space=pltpu.SMEM),
                  tok(d), tok(LANES), tok(p2.shape[1]), const((1, d)), const(wpp.shape),
                  const(wpg.shape), const((1, d)),
                  pl.BlockSpec(memory_space=pl.ANY)],
        out_specs=tok(d),
        out_shape=jax.ShapeDtypeStruct((n, d), F32),
        scratch_shapes=[pltpu.VMEM((TOP_K, tm * ROW_CHUNKS, LANES), F32),
                        pltpu.VMEM((TOP_K, tm * ROW_CHUNKS, LANES), F32), pltpu.SemaphoreType.DMA((2,))],
        compiler_params=pltpu.CompilerParams(
            dimension_semantics=("arbitrary",), vmem_limit_bytes=VMEM_LIMIT),
        name="combine",
    )(pos, pos, h1, gates, p2, pnw, wpp, wpg, fnw, y)


def kernel(x, p, attn_norm_w, w_in, sb_norm_w, hg_lower_bounds, hg_norm_w, w_out, ffn_norm_w,
           w_group_router, b_group_router, w_expert_router, b_expert_router, w_exp_gate, w_exp_up,
           w_exp_down, ple_norm_w, w_ple_proj, w_ple_gate, final_norm_w):
    b, s, d = x.shape
    depth = w_in.shape[0]
    assert depth == 1, "single-layer trunk"
    sbw = sb_norm_w.shape[1]
    hgw = hg_norm_w.shape[1]
    n = b * s
    tm_proj = min(512, s)
    tm_route = min(512, n)
    tm_disp = min(512, n)
    tm_comb = min(256, n)
    tmm = min(512, n)

    wi = w_in[0]
    w_main = jnp.concatenate([wi[:, 0:sbw], wi[:, 2 * sbw:]], axis=1).astype(BF16)
    w_kt = wi[:, sbw:2 * sbw].T.astype(BF16)
    gap = ROUTER_ROW0 - N_GROUPS
    tail = LANES - ROUTER_ROW0 - N_EXPERTS
    wr_t = jnp.concatenate([w_group_router[0].T, jnp.zeros((gap, d), F32), w_expert_router[0].T,
                            jnp.zeros((tail, d), F32)], axis=0)
    br = jnp.concatenate([b_group_router[0], jnp.zeros((gap,), F32), b_expert_router[0],
                          jnp.zeros((tail,), F32)])[:, None]

    sbq, sbkt, sbve, sbvo, hq, hgl, hk, hv, hgate = _in_proj(
        x, attn_norm_w[0][None, :], w_main, w_kt, hg_lower_bounds, tm=tm_proj)
    sbo = _sb_attn(sbq, sbkt, sbve, sbvo, sb_norm_w, chains=SB_CHAINS, qb=SB_QUERY_BLOCKS)
    hgo = _hgrn2(hq, hgl, hk, hv, hgate, hg_norm_w, chunks_per_iter=HG_CHUNKS_PER_ITER)

    h1, m, meta, gates, counts = _out_route(
        x.reshape(n, d), sbo.reshape(n, sbw), hgo.reshape(n, hgw), w_out[0].astype(BF16),
        ffn_norm_w[0][None, :], wr_t, br, tm=tm_route)

    cnt = counts[ROUTER_ROW0:ROUTER_ROW0 + N_EXPERTS, 0].astype(jnp.int32)
    padded = ((cnt + tmm - 1) // tmm) * tmm
    ends = jnp.cumsum(padded)
    starts = ends - padded
    n_rows = n * TOP_K + N_EXPERTS * tmm
    n_tiles = n_rows // tmm
    last_tile = ends[-1] // tmm - 1
    tile_src = jnp.minimum(jnp.arange(n_tiles, dtype=jnp.int32), last_tile)
    tile_expert = jnp.sum((ends[None, :] <= (tile_src * tmm)[:, None]).astype(jnp.int32), axis=1)
    seg = jnp.concatenate([starts, ends, padded, (last_tile + 1)[None],
                           jnp.zeros((LANES - 3 * N_EXPERTS - 1,), jnp.int32)])

    pos = _positions(seg, meta)
    xs = _dispatch(seg, pos, m, n_rows, tm=tm_disp, tmm=tmm)
    y = _experts(tile_expert, tile_src, xs, w_exp_gate[0], w_exp_up[0], w_exp_down[0], tmm=tmm)
    out = _combine(pos, h1, gates, p[0].reshape(n, -1), ple_norm_w[0][None, :],
                   w_ple_proj[0].astype(BF16), w_ple_gate[0].astype(BF16), final_norm_w[None, :],
                   y, tm=tm_comb)
    return out.reshape(b, s, d)
```

```python
import functools

import numpy as np
import jax
import jax.numpy as jnp
from jax import lax
from jax.experimental import pallas as pl
from jax.experimental.pallas import tpu as pltpu

F32 = jnp.float32
BF16 = jnp.bfloat16
EPS = 1e-6

SB_HEADS = 8
SB_HEAD_DIM = 64
HG_HEAD_DIM = 128
HG_CHUNK = 64
HG_CHUNKS_PER_ITER = 8
N_GROUPS = 4
EXPERTS_PER_GROUP = 8
N_EXPERTS = N_GROUPS * EXPERTS_PER_GROUP
TOP_K = 2
LANES = 128
ROW_CHUNKS = 8
ROUTER_ROW0 = 8
SB_BLOCK = 128
SB_QUERY_BLOCKS = 1
SB_CHAINS = 8
SB_LOG2_FLOOR = -152.0
LOG2E = 1.4426950408889634
VMEM_LIMIT = 56 * 1024 * 1024

_NT = (((1,), (1,)), ((), ()))
_TN = (((0,), (0,)), ((), ()))


def _sigmoid(x):
    return 1.0 / (1.0 + jnp.exp(-x))


def _store_row_tiles(ref, x):
    rows = x.shape[0]
    for c in range(ROW_CHUNKS):
        ref[pl.ds(c, rows, stride=ROW_CHUNKS), :] = x[:, c * LANES:(c + 1) * LANES]


def _load_row_tiles(ref, rows):
    return jnp.concatenate([ref[pl.ds(c, rows, stride=ROW_CHUNKS), :] for c in range(ROW_CHUNKS)], axis=1)


def _split2(x):
    hi = x.astype(BF16)
    lo = (x - hi.astype(F32)).astype(BF16)
    return hi, lo


def _in_proj_kernel(x_ref, nw_ref, w_ref, wkt_ref, lbp_ref,
                    sbq_ref, sbkt_ref, sbve_ref, sbvo_ref, hq_ref, hgl_ref, hk_ref, hv_ref, hgate_ref,
                    *, sbw, hgw):
    x = x_ref[0]
    ms = jnp.mean(x * x, axis=-1, keepdims=True)
    a = (x * lax.rsqrt(ms + EPS) * nw_ref[...]).astype(BF16)

    def seg(lo, width):
        return jnp.dot(a, w_ref[:, lo:lo + width], preferred_element_type=F32)

    sbq_ref[0] = (seg(0, sbw) * (SB_HEAD_DIM ** -0.5 * LOG2E)).astype(BF16)
    sbkt_ref[0] = lax.dot_general(wkt_ref[...], a, _NT, preferred_element_type=F32).astype(BF16)
    v = seg(sbw, sbw)
    even_head = (lax.broadcasted_iota(jnp.int32, v.shape, 1) & SB_HEAD_DIM) == 0
    sbve_ref[0] = jnp.where(even_head, v, 0.0).astype(BF16)
    sbvo_ref[0] = jnp.where(even_head, 0.0, v).astype(BF16)
    c = 2 * sbw
    q = seg(c, hgw)
    hq_ref[0] = q * _sigmoid(q)
    p0 = lbp_ref[0:1, :]
    p1 = lbp_ref[1:2, :]
    pm = jnp.maximum(p0, p1)
    e0 = jnp.exp(p0 - pm)
    e1 = jnp.exp(p1 - pm)
    lb = e0 / (e0 + e1)
    fz = seg(c + hgw, hgw)
    sg = _sigmoid(fz)
    f = lb + (1.0 - lb) * sg
    hgl_ref[0] = jnp.log(f)
    hk_ref[0] = 1.0 - f
    hv_ref[0] = seg(c + 2 * hgw, hgw).astype(BF16)
    g = seg(c + 3 * hgw, hgw)
    hgate_ref[0] = g * _sigmoid(g)


def _in_proj(x, nw, w_main, w_kt, lbp, *, tm):
    b, s, d = x.shape
    sbw = w_kt.shape[0]
    hgw = (w_main.shape[1] - 2 * sbw) // 4
    tok = lambda width: pl.BlockSpec((1, tm, width), lambda bi, i: (bi, i, 0))
    const = lambda shape: pl.BlockSpec(shape, lambda bi, i: (0,) * len(shape))
    out_shape = [
        jax.ShapeDtypeStruct((b, s, sbw), BF16),
        jax.ShapeDtypeStruct((b, sbw, s), BF16),
        jax.ShapeDtypeStruct((b, s, sbw), BF16),
        jax.ShapeDtypeStruct((b, s, sbw), BF16),
        jax.ShapeDtypeStruct((b, s, hgw), F32),
        jax.ShapeDtypeStruct((b, s, hgw), F32),
        jax.ShapeDtypeStruct((b, s, hgw), F32),
        jax.ShapeDtypeStruct((b, s, hgw), BF16),
        jax.ShapeDtypeStruct((b, s, hgw), F32),
    ]
    out_specs = [tok(sbw), pl.BlockSpec((1, sbw, tm), lambda bi, i: (bi, 0, i)), tok(sbw), tok(sbw),
                 tok(hgw), tok(hgw), tok(hgw), tok(hgw), tok(hgw)]
    return pl.pallas_call(
        functools.partial(_in_proj_kernel, sbw=sbw, hgw=hgw),
        grid=(b, s // tm),
        in_specs=[tok(d), const((1, d)), const(w_main.shape), const(w_kt.shape), const(lbp.shape)],
        out_specs=out_specs,
        out_shape=out_shape,
        compiler_params=pltpu.CompilerParams(
            dimension_semantics=("arbitrary", "arbitrary"), vmem_limit_bytes=VMEM_LIMIT),
        name="in_proj",
    )(x, nw, w_main, w_kt, lbp)


def _sb_cumsum_weights():
    j = np.arange(SB_BLOCK)[:, None]
    s = np.arange(SB_BLOCK)[None, :]
    half = np.concatenate([(j > s).astype(np.float32), np.ones((SB_BLOCK, SB_BLOCK), np.float32)], axis=1)
    return jnp.asarray(np.concatenate([half, half], axis=0), dtype=BF16)


def _sb_attn_kernel(q_ref, kt_ref, ve_ref, vo_ref, nw_ref, cw_ref, o_ref, *, n_groups, chains, qb):
    assert qb == 1, "the corner step assumes one query block per chain"
    blk = SB_BLOCK
    dh = SB_HEAD_DIM
    rq = qb * blk
    rr = 2 * rq
    first = lax.broadcasted_iota(jnp.int32, (rq, LANES), 1) < dh
    rowpos = lax.broadcasted_iota(jnp.int32, (rr, blk), 0) & (rq - 1)
    colpos = lax.broadcasted_iota(jnp.int32, (rr, blk), 1)
    cw = cw_ref[...]

    ch = blk // 2
    corner_region = (rowpos < ch) & (colpos >= blk - ch)
    corner_keys = lax.broadcasted_iota(jnp.int32, (2 * ch, blk), 1) >= blk - ch

    def corner_rows(x):
        return jnp.concatenate([x[0:ch], x[rq:rq + ch]], axis=0)

    def corner(qss, js, v_scales, carries, accs):
        k0s = [pl.multiple_of(j * blk, blk) for j in js]
        zs = [jnp.dot(corner_rows(qs), kt_ref[0, :, pl.ds(k0, blk)], preferred_element_type=F32)
              for qs, k0 in zip(qss, k0s)]
        lss, hls = [], []
        for z in zs:
            sp = jnp.log2(1.0 + jnp.exp2(-jnp.abs(z)))
            ls = jnp.minimum(z, 0.0) - sp
            hi, lo = _split2(jnp.where(corner_keys, ls - z, 0.0))
            lss.append(ls)
            hls.append(jnp.concatenate([hi, lo], axis=1))
        css = [jnp.dot(hl, cw, preferred_element_type=F32) for hl in hls]
        new_carries, new_accs = [], []
        for ls, cs, carry, acc, k0, v_scale in zip(lss, css, carries, accs, k0s, v_scales):
            a = jnp.where(corner_keys, jnp.exp2(ls + cs[:, :blk] + corner_rows(carry)), 0.0).astype(BF16)
            vst = jnp.concatenate([ve_ref[0, pl.ds(k0, blk), :], vo_ref[0, pl.ds(k0, blk), :]], axis=0) * v_scale
            delta = jnp.dot(jnp.concatenate([a[:ch], a[ch:]], axis=1), vst, preferred_element_type=F32)
            new_accs.append(jnp.concatenate([acc[:ch] + delta, acc[ch:]], axis=0))
            tot = cs[:, blk:]
            new_carries.append(jnp.concatenate(
                [carry[0:ch] + tot[:ch], carry[ch:rq], carry[rq:rq + ch] + tot[ch:], carry[rq + ch:]], axis=0))
        return new_carries, new_accs

    def steps(qss, js, q0s, v_scales, carries, accs, masked, skip=None):
        k0s = [pl.multiple_of(j * blk, blk) for j in js]
        zs = [jnp.dot(qs, kt_ref[0, :, pl.ds(k0, blk)], preferred_element_type=F32)
              for qs, k0 in zip(qss, k0s)]
        lss, hls, befores = [], [], []
        for z, k0, q0 in zip(zs, k0s, q0s):
            sp = jnp.log2(1.0 + jnp.exp2(-jnp.abs(z)))
            ls = jnp.minimum(z, 0.0) - sp
            lk = ls - z
            before = None
            if masked:
                before = (k0 + colpos) < (q0 + rowpos)
                lk = jnp.where(before, lk, 0.0)
            if skip is not None:
                lk = jnp.where(skip, 0.0, lk)
            hi, lo = _split2(lk)
            lss.append(ls)
            hls.append(jnp.concatenate([hi, lo], axis=1))
            befores.append(before)
        css = [jnp.dot(hl, cw, preferred_element_type=F32) for hl in hls]
        abs_ = []
        for ls, cs, carry, before in zip(lss, css, carries, befores):
            a = jnp.exp2(ls + cs[:, :blk] + carry)
            if masked:
                a = jnp.where(before, a, 0.0)
            if skip is not None:
                a = jnp.where(skip, 0.0, a)
            ab = a.astype(BF16)
            abs_.append(jnp.concatenate([ab[:rq], ab[rq:]], axis=1))
        new_accs = []
        for ab, k0, v_scale, acc in zip(abs_, k0s, v_scales, accs):
            vst = jnp.concatenate([ve_ref[0, pl.ds(k0, blk), :], vo_ref[0, pl.ds(k0, blk), :]], axis=0)
            if v_scale is not None:
                vst = vst * v_scale
            new_accs.append(acc + jnp.dot(ab, vst, preferred_element_type=F32))
        new_carries = [carry + cs[:, blk:] for carry, cs in zip(carries, css)]
        return new_carries, new_accs

    def highest(carries):
        m = carries[0]
        for c in carries[1:]:
            m = jnp.maximum(m, c)
        return jnp.max(m)

    def group(gi, _):
        sis = [gi * chains + c for c in range(chains)]
        q0s = [pl.multiple_of(si * rq, rq) for si in sis]
        qss = []
        for q0 in q0s:
            q2 = q_ref[0, pl.ds(q0, rq), :].astype(F32)
            qss.append(jnp.concatenate([jnp.where(first, q2, 0.0), jnp.where(first, 0.0, q2)],
                                       axis=0).astype(BF16))
        carries = [jnp.zeros((rr, blk), F32)] * chains
        accs = [jnp.zeros((rq, LANES), F32)] * chains
        for i in range(qb):
            carries, accs = steps(qss, [si * qb + (qb - 1 - i) for si in sis], q0s, [None] * chains,
                                  carries, accs, True)

        def cond(st):
            n, top = st[0], st[1]
            return (n <= sis[-1] * qb) & (top > SB_LOG2_FLOOR)

        def key_blocks(n, last_chain_in_range):
            js, v_scales = [], []
            for c, si in enumerate(sis):
                d = si * qb - n
                if c < chains - 1 or not last_chain_in_range:
                    v_scales.append(jnp.where(d >= 0, 1.0, 0.0).astype(BF16))
                    d = jnp.maximum(d, 0)
                else:
                    v_scales.append(None)
                js.append(d)
            return js, v_scales

        def body(st):
            n = st[0]
            js, v_scales = key_blocks(n, True)
            skip = corner_region & (n == 2)
            cs_out, as_out = steps(qss, js, q0s, v_scales, list(st[2]), list(st[3]), False, skip)
            return n + 1, highest(cs_out), tuple(cs_out), tuple(as_out)

        js, v_scales = key_blocks(jnp.int32(1), False)
        carries, accs = steps(qss, js, q0s, v_scales, carries, accs, False)
        js, v_scales = key_blocks(jnp.int32(2), False)
        carries, accs = corner(qss, js, v_scales, carries, accs)
        st = lax.while_loop(cond, body, (jnp.int32(2), highest(carries), tuple(carries), tuple(accs)))
        for c, si in enumerate(sis):
            acc = st[3][c]
            q0 = pl.multiple_of(si * rq, rq)
            sq = acc * acc
            s0 = jnp.sum(jnp.where(first, sq, 0.0), axis=-1, keepdims=True)
            s1 = jnp.sum(jnp.where(first, 0.0, sq), axis=-1, keepdims=True)
            ms = jnp.where(first, s0, s1) * (1.0 / dh)
            o_ref[0, pl.ds(q0, rq), :] = (acc * lax.rsqrt(ms + EPS) * nw_ref[...]).astype(o_ref.dtype)
        return 0

    lax.fori_loop(0, n_groups, group, 0)


def _sb_attn(q, kt, v_even, v_odd, nw, *, chains, qb):
    b, s, w = q.shape
    assert s % (chains * qb * SB_BLOCK) == 0 and w % LANES == 0
    n_pairs = w // LANES
    cw = _sb_cumsum_weights()
    tok = pl.BlockSpec((1, s, LANES), lambda bi, hp: (bi, 0, hp))
    return pl.pallas_call(
        functools.partial(_sb_attn_kernel, n_groups=s // (chains * qb * SB_BLOCK), chains=chains, qb=qb),
        grid=(b, n_pairs),
        in_specs=[
            tok,
            pl.BlockSpec((1, LANES, s), lambda bi, hp: (bi, hp, 0)),
            tok, tok,
            pl.BlockSpec((1, LANES), lambda bi, hp: (0, hp)),
            pl.BlockSpec(cw.shape, lambda bi, hp: (0, 0)),
        ],
        out_specs=tok,
        out_shape=jax.ShapeDtypeStruct((b, s, w), BF16),
        compiler_params=pltpu.CompilerParams(
            dimension_semantics=("arbitrary", "arbitrary"), vmem_limit_bytes=VMEM_LIMIT),
        name="sb_attn",
    )(q, kt, v_even, v_odd, nw, cw)


def _hgrn_levels():
    hs = []
    h = HG_CHUNK // 2
    while h >= 1:
        hs.append(h)
        h //= 2
    return hs


def _hgrn_consts():
    n = HG_CHUNK
    t = np.arange(n)[:, None]
    j = np.arange(n)[None, :]
    mats = [j <= t]
    masks = []
    for h in _hgrn_levels():
        mid = (t // (2 * h)) * (2 * h) + h - 1
        upper = (t % (2 * h)) >= h
        if h > 1:
            mats.append(np.where(upper, (j > mid) & (j <= t), (j > t) & (j <= mid)))
        masks.append(((t // (2 * h)) == (j // (2 * h))) & upper & ((j % (2 * h)) < h))
    masks.append(t == j)
    sums = np.concatenate(mats, axis=0).astype(np.float32)
    sums = np.concatenate([sums, sums], axis=1)
    pm = np.concatenate(masks, axis=0).astype(np.float32)
    return jnp.asarray(sums, dtype=BF16), jnp.asarray(pm, dtype=F32)


def _hgrn2_kernel(q_ref, gl_ref, k_ref, v_ref, gate_ref, nw_ref, cs_ref, pm_ref, o_ref,
                  *, n_iters, chunks_per_iter):
    n = HG_CHUNK
    dk = HG_HEAD_DIM
    heads = 2
    levels = _hgrn_levels()
    rows = lax.broadcasted_iota(jnp.int32, (n, heads * dk), 0)
    odd = (rows & 1) != 0

    def body(it, states):
        base = it * (chunks_per_iter * n)
        cs = cs_ref[...]
        half = cs.shape[0] // 2
        units = []
        for c in range(chunks_per_iter):
            sl = pl.ds(pl.multiple_of(base + c * n, n), n)
            g = gl_ref[0, sl, :]
            g_parts = jnp.concatenate(_split2(g), axis=0)
            d = jnp.concatenate(
                [jnp.dot(cs[:half], g_parts, preferred_element_type=F32),
                 jnp.dot(cs[half:], g_parts, preferred_element_type=F32)], axis=0)
            units.append(dict(sl=sl, g=g, d=d, q=q_ref[0, sl, :], k=k_ref[0, sl, :], v=v_ref[0, sl, :]))
        for un in units:
            q, k, g, d = un["q"], un["k"], un["g"], un["d"]
            bc = d[0:n]
            b_last = bc[n - 1:n, :]
            ws = []
            for li, h in enumerate(levels):
                dl = d[(1 + li) * n:(2 + li) * n] if h > 1 else jnp.where(odd, g, 0.0)
                upper = (rows & h) != 0
                ws.append((jnp.where(upper, q, k) * jnp.exp(dl)).astype(BF16))
            un["ws"] = ws
            un["qk"] = q * k
            un["qd"] = (q * jnp.exp(bc)).astype(BF16)
            un["kd"] = (k * jnp.exp(b_last - bc)).astype(BF16)
            un["decay"] = jnp.exp(b_last)
        hslices = [slice(hh * dk, (hh + 1) * dk) for hh in range(heads)]
        for un in units:
            un["gram"] = [[lax.dot_general(w[:, ls], w[:, ls], _NT, preferred_element_type=F32)
                           for w in un["ws"]] for ls in hslices]
            un["kv"] = [lax.dot_general(un["v"][:, ls], un["kd"][:, ls], _TN, preferred_element_type=F32)
                        for ls in hslices]
        for un in units:
            un["st"] = states
            states = tuple(states[hh] * un["decay"][:, ls] + un["kv"][hh] for hh, ls in enumerate(hslices))
            ps = []
            for hh, ls in enumerate(hslices):
                p = pm_ref[len(levels) * n:(len(levels) + 1) * n, :] * jnp.sum(un["qk"][:, ls], axis=1, keepdims=True)
                for li in range(len(levels)):
                    p = p + pm_ref[li * n:(li + 1) * n, :] * un["gram"][hh][li]
                ps.append(p.astype(BF16))
            un["p"] = ps
        for un in units:
            un["o"] = [jnp.dot(un["p"][hh], un["v"][:, ls], preferred_element_type=F32)
                       + lax.dot_general(un["qd"][:, ls], un["st"][hh].astype(BF16), _NT,
                                         preferred_element_type=F32)
                       for hh, ls in enumerate(hslices)]
        for un in units:
            outs = []
            for o in un["o"]:
                ms = jnp.mean(o * o, axis=-1, keepdims=True)
                outs.append(o * lax.rsqrt(ms + EPS))
            o2 = jnp.concatenate(outs, axis=1)
            o_ref[0, un["sl"], :] = (o2 * nw_ref[...] * gate_ref[0, un["sl"], :]).astype(o_ref.dtype)
        return states

    zero = jnp.zeros((dk, dk), F32)
    lax.fori_loop(0, n_iters, body, (zero, zero))


def _hgrn2(q, gl, k, v, gate, nw, *, chunks_per_iter):
    b, s, w = q.shape
    assert s % (HG_CHUNK * chunks_per_iter) == 0 and w % (2 * HG_HEAD_DIM) == 0
    pair = 2 * HG_HEAD_DIM
    heads = w // pair
    cs, pm = _hgrn_consts()
    tok = pl.BlockSpec((1, s, pair), lambda bi, h: (bi, 0, h))
    return pl.pallas_call(
        functools.partial(_hgrn2_kernel, n_iters=s // (HG_CHUNK * chunks_per_iter),
                          chunks_per_iter=chunks_per_iter),
        grid=(b, heads),
        in_specs=[tok, tok, tok, tok, tok,
                  pl.BlockSpec((1, pair), lambda bi, h: (0, h)),
                  pl.BlockSpec(cs.shape, lambda bi, h: (0, 0)),
                  pl.BlockSpec(pm.shape, lambda bi, h: (0, 0))],
        out_specs=tok,
        out_shape=jax.ShapeDtypeStruct((b, s, w), BF16),
        compiler_params=pltpu.CompilerParams(
            dimension_semantics=("arbitrary", "arbitrary"), vmem_limit_bytes=VMEM_LIMIT),
        name="hgrn2",
    )(q, gl, k, v, gate, nw, cs, pm)


def _out_route_kernel(x_ref, sbo_ref, hgo_ref, wo_ref, nw_ref, wr_ref, br_ref, su_ref,
                      h1_ref, m_ref, meta_ref, gates_ref, counts_ref, carry_ref, *, sbw):
    i = pl.program_id(0)

    @pl.when(i == 0)
    def _():
        carry_ref[...] = jnp.zeros_like(carry_ref)

    h1 = (x_ref[...]
          + jnp.dot(sbo_ref[...], wo_ref[0:sbw, :], preferred_element_type=F32)
          + jnp.dot(hgo_ref[...], wo_ref[sbw:, :], preferred_element_type=F32))
    h1_ref[...] = h1
    ms = jnp.mean(h1 * h1, axis=-1, keepdims=True)
    m = h1 * lax.rsqrt(ms + EPS) * nw_ref[...]
    _store_row_tiles(m_ref, m)

    m2 = _split2(m)
    w2 = _split2(wr_ref[...])
    logits = br_ref[...]
    for wi, mi in ((0, 0), (0, 1), (1, 0)):
        logits = logits + lax.dot_general(w2[wi], m2[mi], _NT, preferred_element_type=F32)
    rid = lax.broadcasted_iota(jnp.int32, logits.shape, 0)
    neg = -jnp.inf
    big = jnp.int32(2 * LANES)

    def first_argmax(vals):
        vmax = jnp.max(vals, axis=0, keepdims=True)
        idx = jnp.min(jnp.where(vals == vmax, rid, big), axis=0, keepdims=True)
        return vmax, idx

    is_group = rid < N_GROUPS
    gmax, g_idx = first_argmax(jnp.where(is_group, logits, neg))
    gsum = jnp.sum(jnp.where(is_group, jnp.exp(logits - gmax), 0.0), axis=0, keepdims=True)
    g_prob = 1.0 / gsum
    lo_row = ROUTER_ROW0 + EXPERTS_PER_GROUP * g_idx
    el = jnp.where((rid >= lo_row) & (rid < lo_row + EXPERTS_PER_GROUP), logits, neg)
    v1, i1 = first_argmax(el)
    v2, i2 = first_argmax(jnp.where(rid == i1, neg, el))
    dd = jnp.exp(v2 - v1)
    p1 = 1.0 / (1.0 + dd)
    g1 = p1 * g_prob
    g2 = dd * p1 * g_prob

    hit1 = rid == i1
    hit2 = rid == i2
    onehot = jnp.where(hit1 | hit2, 1.0, 0.0)
    before_cnt = carry_ref[...] + jnp.dot(onehot.astype(BF16), su_ref[...], preferred_element_type=F32)
    r1 = jnp.sum(jnp.where(hit1, before_cnt, 0.0), axis=0, keepdims=True)
    r2 = jnp.sum(jnp.where(hit2, before_cnt, 0.0), axis=0, keepdims=True)
    carry_ref[...] = carry_ref[...] + jnp.sum(onehot, axis=1, keepdims=True)
    counts_ref[...] = carry_ref[...]

    meta_ref[...] = jnp.zeros_like(meta_ref)
    meta_ref[0:1, :] = i1 - ROUTER_ROW0
    meta_ref[1:2, :] = i2 - ROUTER_ROW0
    meta_ref[2:3, :] = r1.astype(jnp.int32)
    meta_ref[3:4, :] = r2.astype(jnp.int32)
    gt = jnp.where(rid == 0, g1, jnp.where(rid == 1, g2, 0.0))
    gates_ref[...] = gt.T


def _out_route(x2, sbo, hgo, w_out, nw, wr_t, br, *, tm):
    n, d = x2.shape
    assert d == ROW_CHUNKS * LANES
    sbw = sbo.shape[1]
    j = np.arange(tm)[:, None]
    t = np.arange(tm)[None, :]
    su = jnp.asarray((j < t).astype(np.float32), dtype=BF16)
    tok = lambda width: pl.BlockSpec((tm, width), lambda i: (i, 0))
    const = lambda shape: pl.BlockSpec(shape, lambda i: (0,) * len(shape))
    return pl.pallas_call(
        functools.partial(_out_route_kernel, sbw=sbw),
        grid=(n // tm,),
        in_specs=[tok(d), tok(sbw), tok(hgo.shape[1]), const(w_out.shape), const((1, d)),
                  const(wr_t.shape), const(br.shape), const(su.shape)],
        out_specs=[tok(d), pl.BlockSpec((tm * ROW_CHUNKS, LANES), lambda i: (i, 0)),
                   pl.BlockSpec((8, tm), lambda i: (0, i)), tok(LANES), const((LANES, 1))],
        out_shape=[jax.ShapeDtypeStruct((n, d), F32), jax.ShapeDtypeStruct((n * ROW_CHUNKS, LANES), F32),
                   jax.ShapeDtypeStruct((8, n), jnp.int32), jax.ShapeDtypeStruct((n, LANES), F32),
                   jax.ShapeDtypeStruct((LANES, 1), F32)],
        scratch_shapes=[pltpu.VMEM((LANES, 1), F32)],
        compiler_params=pltpu.CompilerParams(
            dimension_semantics=("arbitrary",), vmem_limit_bytes=VMEM_LIMIT),
        name="out_route",
    )(x2, sbo, hgo, w_out, nw, wr_t, br, su)


def _positions_kernel(seg_ref, meta_ref, pos_ref):
    e = meta_ref[0:TOP_K, :]
    start = jnp.zeros(e.shape, jnp.int32)
    for x in range(N_EXPERTS):
        start = jnp.where(e == x, seg_ref[x], start)
    pos_ref[...] = jnp.zeros_like(pos_ref)
    pos_ref[0:TOP_K, :] = start + meta_ref[TOP_K:2 * TOP_K, :]


def _positions(seg, meta):
    return pl.pallas_call(
        _positions_kernel,
        in_specs=[pl.BlockSpec(memory_space=pltpu.SMEM), pl.BlockSpec(memory_space=pltpu.VMEM)],
        out_specs=pl.BlockSpec(memory_space=pltpu.VMEM),
        out_shape=jax.ShapeDtypeStruct(meta.shape, jnp.int32),
        compiler_params=pltpu.CompilerParams(vmem_limit_bytes=VMEM_LIMIT),
        name="positions",
    )(seg, meta)


def _dispatch_kernel(seg_ref, pos_ref, m_ref, xs_ref, zbuf, sem, zsem, *, tm, tmm, n_tiles):
    i = pl.program_id(0)

    rc = ROW_CHUNKS

    def zero_copy(e):
        tail = pl.multiple_of((seg_ref[N_EXPERTS + e] - tmm) * rc, tmm * rc)
        return pltpu.make_async_copy(zbuf, xs_ref.at[pl.ds(tail, tmm * rc), :], zsem)

    @pl.when(i == 0)
    def _():
        zbuf[...] = jnp.zeros_like(zbuf)
        for e in range(N_EXPERTS):
            @pl.when(seg_ref[2 * N_EXPERTS + e] > 0)
            def _():
                zero_copy(e).start()
        for e in range(N_EXPERTS):
            @pl.when(seg_ref[2 * N_EXPERTS + e] > 0)
            def _():
                zero_copy(e).wait()

        def unused_copy(t):
            first = pl.multiple_of(t * (tmm * rc), tmm * rc)
            return pltpu.make_async_copy(zbuf, xs_ref.at[pl.ds(first, tmm * rc), :], zsem)

        def start_unused(t, _):
            unused_copy(t).start()
            return 0

        def wait_unused(t, _):
            unused_copy(t).wait()
            return 0

        lax.fori_loop(seg_ref[3 * N_EXPERTS], n_tiles, start_unused, 0)
        lax.fori_loop(seg_ref[3 * N_EXPERTS], n_tiles, wait_unused, 0)

    for r in range(tm):
        src = m_ref.at[pl.ds(r * rc, rc), :]
        for k in range(TOP_K):
            dst = pl.multiple_of(pos_ref[k, r] * rc, rc)
            pltpu.make_async_copy(src, xs_ref.at[pl.ds(dst, rc), :], sem).start(priority=k)
    for _ in range(TOP_K):
        pltpu.make_async_copy(m_ref, xs_ref.at[pl.ds(0, tm * rc), :], sem).wait()


def _dispatch(seg, pos, m, n_rows, *, tm, tmm):
    n = m.shape[0] // ROW_CHUNKS
    return pl.pallas_call(
        functools.partial(_dispatch_kernel, tm=tm, tmm=tmm, n_tiles=n_rows // tmm),
        grid=(n // tm,),
        in_specs=[pl.BlockSpec(memory_space=pltpu.SMEM),
                  pl.BlockSpec((8, tm), lambda i: (0, i), memory_space=pltpu.SMEM),
                  pl.BlockSpec((tm * ROW_CHUNKS, LANES), lambda i: (i, 0))],
        out_specs=pl.BlockSpec(memory_space=pl.ANY),
        out_shape=jax.ShapeDtypeStruct((n_rows * ROW_CHUNKS, LANES), F32),
        scratch_shapes=[pltpu.VMEM((tmm * ROW_CHUNKS, LANES), F32), pltpu.SemaphoreType.DMA(()),
                        pltpu.SemaphoreType.DMA(())],
        compiler_params=pltpu.CompilerParams(
            dimension_semantics=("arbitrary",), vmem_limit_bytes=VMEM_LIMIT),
        name="dispatch",
    )(seg, pos, m)


def _experts_kernel(te_ref, ts_ref, xs_ref, wg_ref, wu_ref, wd_ref, y_ref, wg_b, wu_b, wd_b, *, tmm):
    t = pl.program_id(0)

    @pl.when((t == 0) | (te_ref[t] != te_ref[jnp.maximum(t - 1, 0)]))
    def _():
        wg_b[...] = wg_ref[0].astype(BF16)
        wu_b[...] = wu_ref[0].astype(BF16)
        wd_b[...] = wd_ref[0].astype(BF16)

    @pl.when(ts_ref[t] == t)
    def _():
        x = _load_row_tiles(xs_ref, tmm).astype(BF16)
        hg = jnp.dot(x, wg_b[...], preferred_element_type=F32)
        hu = jnp.dot(x, wu_b[...], preferred_element_type=F32)
        act = (hg * _sigmoid(hg) * hu).astype(BF16)
        _store_row_tiles(y_ref, jnp.dot(act, wd_b[...], preferred_element_type=F32))

    @pl.when(ts_ref[t] != t)
    def _():
        y_ref[...] = jnp.zeros_like(y_ref)


def _experts(tile_expert, tile_src, xs, wg, wu, wd, *, tmm):
    n_rows = xs.shape[0] // ROW_CHUNKS
    d, de = wg.shape[1], wg.shape[2]
    assert d == ROW_CHUNKS * LANES
    rows = pl.BlockSpec((tmm * ROW_CHUNKS, LANES), lambda t, te, ts: (ts[t], 0))
    grid_spec = pltpu.PrefetchScalarGridSpec(
        num_scalar_prefetch=2,
        grid=(n_rows // tmm,),
        in_specs=[rows,
                  pl.BlockSpec((1, d, de), lambda t, te, ts: (te[t], 0, 0)),
                  pl.BlockSpec((1, d, de), lambda t, te, ts: (te[t], 0, 0)),
                  pl.BlockSpec((1, de, d), lambda t, te, ts: (te[t], 0, 0))],
        out_specs=pl.BlockSpec((tmm * ROW_CHUNKS, LANES), lambda t, te, ts: (t, 0)),
        scratch_shapes=[pltpu.VMEM((d, de), BF16), pltpu.VMEM((d, de), BF16), pltpu.VMEM((de, d), BF16)],
    )
    return pl.pallas_call(
        functools.partial(_experts_kernel, tmm=tmm),
        grid_spec=grid_spec,
        out_shape=jax.ShapeDtypeStruct((n_rows * ROW_CHUNKS, LANES), F32),
        compiler_params=pltpu.CompilerParams(
            dimension_semantics=("arbitrary",), vmem_limit_bytes=VMEM_LIMIT),
        name="experts",
    )(tile_expert, tile_src, xs, wg, wu, wd)


def _combine_kernel(pos_ref, posn_ref, h1_ref, gates_ref, p_ref, pnw_ref, wpp_ref, wpg_ref, fnw_ref,
                    y_ref, o_ref, ybuf_even, ybuf_odd, sem, *, tm, n_steps):
    i = pl.program_id(0)

    rc = ROW_CHUNKS
    ybufs = (ybuf_even, ybuf_odd)

    def gather(p_ref_, slot):
        for r in range(tm):
            for k in range(TOP_K):
                src = pl.multiple_of(p_ref_[k, r] * rc, rc)
                pltpu.make_async_copy(y_ref.at[pl.ds(src, rc), :],
                                      ybufs[slot].at[k, pl.ds(r * rc, rc), :], sem.at[slot]).start(priority=k)

    def wait(slot):
        for k in range(TOP_K):
            pltpu.make_async_copy(y_ref.at[pl.ds(0, tm * rc), :], ybufs[slot].at[k], sem.at[slot]).wait()

    @pl.when(i == 0)
    def _():
        gather(pos_ref, 0)

    def step(slot):
        wait(slot)
        gather(posn_ref, 1 - slot)
        gts = gates_ref[...]
        h2 = (h1_ref[...] + gts[:, 0:1] * _load_row_tiles(ybufs[slot].at[0], tm)
              + gts[:, 1:2] * _load_row_tiles(ybufs[slot].at[1], tm))
        e = jnp.dot(p_ref[...].astype(BF16), wpp_ref[...], preferred_element_type=F32)
        ms = jnp.mean(h2 * h2, axis=-1, keepdims=True)
        hn = (h2 * lax.rsqrt(ms + EPS) * pnw_ref[...]).astype(BF16)
        gate = _sigmoid(jnp.dot(hn, wpg_ref[...], preferred_element_type=F32))
        h3 = h2 + gate * e
        ms3 = jnp.mean(h3 * h3, axis=-1, keepdims=True)
        o_ref[...] = h3 * lax.rsqrt(ms3 + EPS) * fnw_ref[...]

        @pl.when(i == n_steps - 1)
        def _():
            wait(1 - slot)

    for parity in range(2):
        @pl.when(i % 2 == parity)
        def _():
            step(parity)


def _combine(pos, h1, gates, p2, pnw, wpp, wpg, fnw, y, *, tm):
    n, d = h1.shape
    n_steps = n // tm
    tok = lambda width: pl.BlockSpec((tm, width), lambda i: (i, 0))
    const = lambda shape: pl.BlockSpec(shape, lambda i: (0,) * len(shape))
    return pl.pallas_call(
        functools.partial(_combine_kernel, tm=tm, n_steps=n_steps),
        grid=(n_steps,),
        in_specs=[pl.BlockSpec((8, tm), lambda i: (0, i), memory_space=pltpu.SMEM),
                  pl.BlockSpec((8, tm), lambda i: (0, jnp.minimum(i + 1, n_steps - 1)),
                               memory_space=pltpu.SMEM),
                  tok(d), tok(LANES), tok(p2.shape[1]), const((1, d)), const(wpp.shape),
                  const(wpg.shape), const((1, d)),
                  pl.BlockSpec(memory_space=pl.ANY)],
        out_specs=tok(d),
        out_shape=jax.ShapeDtypeStruct((n, d), F32),
        scratch_shapes=[pltpu.VMEM((TOP_K, tm * ROW_CHUNKS, LANES), F32),
                        pltpu.VMEM((TOP_K, tm * ROW_CHUNKS, LANES), F32), pltpu.SemaphoreType.DMA((2,))],
        compiler_params=pltpu.CompilerParams(
            dimension_semantics=("arbitrary",), vmem_limit_bytes=VMEM_LIMIT),
        name="combine",
    )(pos, pos, h1, gates, p2, pnw, wpp, wpg, fnw, y)


def kernel(x, p, attn_norm_w, w_in, sb_norm_w, hg_lower_bounds, hg_norm_w, w_out, ffn_norm_w,
           w_group_router, b_group_router, w_expert_router, b_expert_router, w_exp_gate, w_exp_up,
           w_exp_down, ple_norm_w, w_ple_proj, w_ple_gate, final_norm_w):
    b, s, d = x.shape
    depth = w_in.shape[0]
    assert depth == 1, "single-layer trunk"
    sbw = sb_norm_w.shape[1]
    hgw = hg_norm_w.shape[1]
    n = b * s
    tm_proj = min(512, s)
    tm_route = min(512, n)
    tm_disp = min(512, n)
    tm_comb = min(256, n)
    tmm = min(512, n)

    wi = w_in[0]
    w_main = jnp.concatenate([wi[:, 0:sbw], wi[:, 2 * sbw:]], axis=1).astype(BF16)
    w_kt = wi[:, sbw:2 * sbw].T.astype(BF16)
    gap = ROUTER_ROW0 - N_GROUPS
    tail = LANES - ROUTER_ROW0 - N_EXPERTS
    wr_t = jnp.concatenate([w_group_router[0].T, jnp.zeros((gap, d), F32), w_expert_router[0].T,
                            jnp.zeros((tail, d), F32)], axis=0)
    br = jnp.concatenate([b_group_router[0], jnp.zeros((gap,), F32), b_expert_router[0],
                          jnp.zeros((tail,), F32)])[:, None]

    sbq, sbkt, sbve, sbvo, hq, hgl, hk, hv, hgate = _in_proj(
        x, attn_norm_w[0][None, :], w_main, w_kt, hg_lower_bounds, tm=tm_proj)
    sbo = _sb_attn(sbq, sbkt, sbve, sbvo, sb_norm_w, chains=SB_CHAINS, qb=SB_QUERY_BLOCKS)
    hgo = _hgrn2(hq, hgl, hk, hv, hgate, hg_norm_w, chunks_per_iter=HG_CHUNKS_PER_ITER)

    h1, m, meta, gates, counts = _out_route(
        x.reshape(n, d), sbo.reshape(n, sbw), hgo.reshape(n, hgw), w_out[0].astype(BF16),
        ffn_norm_w[0][None, :], wr_t, br, tm=tm_route)

    cnt = counts[ROUTER_ROW0:ROUTER_ROW0 + N_EXPERTS, 0].astype(jnp.int32)
    padded = ((cnt + tmm - 1) // tmm) * tmm
    ends = jnp.cumsum(padded)
    starts = ends - padded
    n_rows = n * TOP_K + N_EXPERTS * tmm
    n_tiles = n_rows // tmm
    last_tile = ends[-1] // tmm - 1
    tile_src = jnp.minimum(jnp.arange(n_tiles, dtype=jnp.int32), last_tile)
    tile_expert = jnp.sum((ends[None, :] <= (tile_src * tmm)[:, None]).astype(jnp.int32), axis=1)
    seg = jnp.concatenate([starts, ends, padded, (last_tile + 1)[None],
                           jnp.zeros((LANES - 3 * N_EXPERTS - 1,), jnp.int32)])

    pos = _positions(seg, meta)
    xs = _dispatch(seg, pos, m, n_rows, tm=tm_disp, tmm=tmm)
    y = _experts(tile_expert, tile_src, xs, w_exp_gate[0], w_exp_up[0], w_exp_down[0], tmm=tmm)
    out = _combine(pos, h1, gates, p[0].reshape(n, -1), ple_norm_w[0][None, :],
                   w_ple_proj[0].astype(BF16), w_ple_gate[0].astype(BF16), final_norm_w[None, :],
                   y, tm=tm_comb)
    return out.reshape(b, s, d)
```

```python
import functools

import numpy as np
import jax
import jax.numpy as jnp
from jax import lax
from jax.experimental import pallas as pl
from jax.experimental.pallas import tpu as pltpu

F32 = jnp.float32
BF16 = jnp.bfloat16
EPS = 1e-6

SB_HEADS = 8
SB_HEAD_DIM = 64
HG_HEAD_DIM = 128
HG_CHUNK = 64
HG_CHUNKS_PER_ITER = 8
N_GROUPS = 4
EXPERTS_PER_GROUP = 8
N_EXPERTS = N_GROUPS * EXPERTS_PER_GROUP
TOP_K = 2
LANES = 128
ROW_CHUNKS = 8
ROUTER_ROW0 = 8
ROUTER_ROWS = 48
SB_BLOCK = 128
SB_QUERY_BLOCKS = 1
SB_CHAINS = 16
SB_LOG2_FLOOR = -152.0
LOG2E = 1.4426950408889634
VMEM_LIMIT = 56 * 1024 * 1024

_NT = (((1,), (1,)), ((), ()))
_TN = (((0,), (0,)), ((), ()))


def _sigmoid(x):
    return 1.0 / (1.0 + jnp.exp(-x))


def _store_row_tiles(ref, x):
    rows = x.shape[0]
    for c in range(ROW_CHUNKS):
        ref[pl.ds(c, rows, stride=ROW_CHUNKS), :] = x[:, c * LANES:(c + 1) * LANES]


def _load_row_tiles(ref, rows):
    return jnp.concatenate([ref[pl.ds(c, rows, stride=ROW_CHUNKS), :] for c in range(ROW_CHUNKS)], axis=1)


def _split2(x):
    hi = x.astype(BF16)
    lo = (x - hi.astype(F32)).astype(BF16)
    return hi, lo


def _in_proj_kernel(x_ref, nw_ref, w_ref, wkt_ref, lbp_ref,
                    sbq_ref, sbkt_ref, sbve_ref, sbvo_ref, hq_ref, hgl_ref, hk_ref, hv_ref, hgate_ref,
                    *, sbw, hgw):
    x = x_ref[0]
    ms = jnp.mean(x * x, axis=-1, keepdims=True)
    a = (x * lax.rsqrt(ms + EPS) * nw_ref[...]).astype(BF16)

    def seg(lo, width):
        return jnp.dot(a, w_ref[:, lo:lo + width], preferred_element_type=F32)

    sbq_ref[0] = (seg(0, sbw) * (SB_HEAD_DIM ** -0.5 * LOG2E)).astype(BF16)
    sbkt_ref[0] = lax.dot_general(wkt_ref[...], a, _NT, preferred_element_type=F32).astype(BF16)
    v = seg(sbw, sbw)
    even_head = (lax.broadcasted_iota(jnp.int32, v.shape, 1) & SB_HEAD_DIM) == 0
    sbve_ref[0] = jnp.where(even_head, v, 0.0).astype(BF16)
    sbvo_ref[0] = jnp.where(even_head, 0.0, v).astype(BF16)
    c = 2 * sbw
    q = seg(c, hgw)
    hq_ref[0] = q * _sigmoid(q)
    p0 = lbp_ref[0:1, :]
    p1 = lbp_ref[1:2, :]
    pm = jnp.maximum(p0, p1)
    e0 = jnp.exp(p0 - pm)
    e1 = jnp.exp(p1 - pm)
    lb = e0 / (e0 + e1)
    fz = seg(c + hgw, hgw)
    sg = _sigmoid(fz)
    f = lb + (1.0 - lb) * sg
    hgl_ref[0] = jnp.log(f)
    hk_ref[0] = 1.0 - f
    hv_ref[0] = seg(c + 2 * hgw, hgw).astype(BF16)
    g = seg(c + 3 * hgw, hgw)
    hgate_ref[0] = g * _sigmoid(g)


def _in_proj(x, nw, w_main, w_kt, lbp, *, tm):
    b, s, d = x.shape
    sbw = w_kt.shape[0]
    hgw = (w_main.shape[1] - 2 * sbw) // 4
    tok = lambda width: pl.BlockSpec((1, tm, width), lambda bi, i: (bi, i, 0))
    const = lambda shape: pl.BlockSpec(shape, lambda bi, i: (0,) * len(shape))
    out_shape = [
        jax.ShapeDtypeStruct((b, s, sbw), BF16),
        jax.ShapeDtypeStruct((b, sbw, s), BF16),
        jax.ShapeDtypeStruct((b, s, sbw), BF16),
        jax.ShapeDtypeStruct((b, s, sbw), BF16),
        jax.ShapeDtypeStruct((b, s, hgw), F32),
        jax.ShapeDtypeStruct((b, s, hgw), F32),
        jax.ShapeDtypeStruct((b, s, hgw), F32),
        jax.ShapeDtypeStruct((b, s, hgw), BF16),
        jax.ShapeDtypeStruct((b, s, hgw), F32),
    ]
    out_specs = [tok(sbw), pl.BlockSpec((1, sbw, tm), lambda bi, i: (bi, 0, i)), tok(sbw), tok(sbw),
                 tok(hgw), tok(hgw), tok(hgw), tok(hgw), tok(hgw)]
    return pl.pallas_call(
        functools.partial(_in_proj_kernel, sbw=sbw, hgw=hgw),
        grid=(b, s // tm),
        in_specs=[tok(d), const((1, d)), const(w_main.shape), const(w_kt.shape), const(lbp.shape)],
        out_specs=out_specs,
        out_shape=out_shape,
        compiler_params=pltpu.CompilerParams(
            dimension_semantics=("arbitrary", "arbitrary"), vmem_limit_bytes=VMEM_LIMIT),
        name="in_proj",
    )(x, nw, w_main, w_kt, lbp)


def _sb_cumsum_weights():
    j = np.arange(SB_BLOCK)[:, None]
    s = np.arange(SB_BLOCK)[None, :]
    half = np.concatenate([(j > s).astype(np.float32), np.ones((SB_BLOCK, SB_BLOCK), np.float32)], axis=1)
    return jnp.asarray(np.concatenate([half, half], axis=0), dtype=BF16)


def _sb_attn_kernel(q_ref, kt_ref, ve_ref, vo_ref, nw_ref, cw_ref, o_ref, *, n_groups, chains, qb):
    assert qb == 1, "the corner step assumes one query block per chain"
    blk = SB_BLOCK
    dh = SB_HEAD_DIM
    rq = qb * blk
    rr = 2 * rq
    first = lax.broadcasted_iota(jnp.int32, (rq, LANES), 1) < dh
    rowpos = lax.broadcasted_iota(jnp.int32, (rr, blk), 0) & (rq - 1)
    colpos = lax.broadcasted_iota(jnp.int32, (rr, blk), 1)
    cw = cw_ref[...]

    ch = blk // 2
    corner_region = (rowpos < ch) & (colpos >= blk - ch)
    corner_keys = lax.broadcasted_iota(jnp.int32, (2 * ch, blk), 1) >= blk - ch

    def corner_rows(x):
        return jnp.concatenate([x[0:ch], x[rq:rq + ch]], axis=0)

    def corner(qss, js, v_scales, carries, accs):
        k0s = [pl.multiple_of(j * blk, blk) for j in js]
        zs = [jnp.dot(corner_rows(qs), kt_ref[0, :, pl.ds(k0, blk)], preferred_element_type=F32)
              for qs, k0 in zip(qss, k0s)]
        lss, hls = [], []
        for z in zs:
            sp = jnp.log2(1.0 + jnp.exp2(-jnp.abs(z)))
            ls = jnp.minimum(z, 0.0) - sp
            hi, lo = _split2(jnp.where(corner_keys, ls - z, 0.0))
            lss.append(ls)
            hls.append(jnp.concatenate([hi, lo], axis=1))
        css = [jnp.dot(hl, cw, preferred_element_type=F32) for hl in hls]
        new_carries, new_accs = [], []
        for ls, cs, carry, acc, k0, v_scale in zip(lss, css, carries, accs, k0s, v_scales):
            a = jnp.where(corner_keys, jnp.exp2(ls + cs[:, :blk] + corner_rows(carry)), 0.0).astype(BF16)
            vst = jnp.concatenate([ve_ref[0, pl.ds(k0, blk), :], vo_ref[0, pl.ds(k0, blk), :]], axis=0) * v_scale
            delta = jnp.dot(jnp.concatenate([a[:ch], a[ch:]], axis=1), vst, preferred_element_type=F32)
            new_accs.append(jnp.concatenate([acc[:ch] + delta, acc[ch:]], axis=0))
            tot = cs[:, blk:]
            new_carries.append(jnp.concatenate(
                [carry[0:ch] + tot[:ch], carry[ch:rq], carry[rq:rq + ch] + tot[ch:], carry[rq + ch:]], axis=0))
        return new_carries, new_accs

    def steps(qss, js, q0s, v_scales, carries, accs, masked, skip=None):
        k0s = [pl.multiple_of(j * blk, blk) for j in js]
        zs = [jnp.dot(qs, kt_ref[0, :, pl.ds(k0, blk)], preferred_element_type=F32)
              for qs, k0 in zip(qss, k0s)]
        lss, hls, befores = [], [], []
        for z, k0, q0 in zip(zs, k0s, q0s):
            sp = jnp.log2(1.0 + jnp.exp2(-jnp.abs(z)))
            ls = jnp.minimum(z, 0.0) - sp
            lk = ls - z
            before = None
            if masked:
                before = (k0 + colpos) < (q0 + rowpos)
                lk = jnp.where(before, lk, 0.0)
            if skip is not None:
                lk = jnp.where(skip, 0.0, lk)
            hi, lo = _split2(lk)
            lss.append(ls)
            hls.append(jnp.concatenate([hi, lo], axis=1))
            befores.append(before)
        css = [jnp.dot(hl, cw, preferred_element_type=F32) for hl in hls]
        abs_ = []
        for ls, cs, carry, before in zip(lss, css, carries, befores):
            a = jnp.exp2(ls + cs[:, :blk] + carry)
            if masked:
                a = jnp.where(before, a, 0.0)
            if skip is not None:
                a = jnp.where(skip, 0.0, a)
            ab = a.astype(BF16)
            abs_.append(jnp.concatenate([ab[:rq], ab[rq:]], axis=1))
        new_accs = []
        for ab, k0, v_scale, acc in zip(abs_, k0s, v_scales, accs):
            vst = jnp.concatenate([ve_ref[0, pl.ds(k0, blk), :], vo_ref[0, pl.ds(k0, blk), :]], axis=0)
            if v_scale is not None:
                vst = vst * v_scale
            new_accs.append(acc + jnp.dot(ab, vst, preferred_element_type=F32))
        new_carries = [carry + cs[:, blk:] for carry, cs in zip(carries, css)]
        return new_carries, new_accs

    def highest(carries):
        m = carries[0]
        for c in carries[1:]:
            m = jnp.maximum(m, c)
        return jnp.max(m)

    def group(gi, _):
        sis = [gi * chains + c for c in range(chains)]
        q0s = [pl.multiple_of(si * rq, rq) for si in sis]
        qss = []
        for q0 in q0s:
            q2 = q_ref[0, pl.ds(q0, rq), :].astype(F32)
            qss.append(jnp.concatenate([jnp.where(first, q2, 0.0), jnp.where(first, 0.0, q2)],
                                       axis=0).astype(BF16))
        carries = [jnp.zeros((rr, blk), F32)] * chains
        accs = [jnp.zeros((rq, LANES), F32)] * chains
        for i in range(qb):
            carries, accs = steps(qss, [si * qb + (qb - 1 - i) for si in sis], q0s, [None] * chains,
                                  carries, accs, True)

        def cond(st):
            n, top = st[0], st[1]
            return (n <= sis[-1] * qb) & (top > SB_LOG2_FLOOR)

        def key_blocks(n, last_chain_in_range):
            js, v_scales = [], []
            for c, si in enumerate(sis):
                d = si * qb - n
                if c < chains - 1 or not last_chain_in_range:
                    v_scales.append(jnp.where(d >= 0, 1.0, 0.0).astype(BF16))
                    d = jnp.maximum(d, 0)
                else:
                    v_scales.append(None)
                js.append(d)
            return js, v_scales

        def body(st):
            n = st[0]
            js, v_scales = key_blocks(n, True)
            skip = corner_region & (n == 2)
            cs_out, as_out = steps(qss, js, q0s, v_scales, list(st[2]), list(st[3]), False, skip)
            return n + 1, highest(cs_out), tuple(cs_out), tuple(as_out)

        js, v_scales = key_blocks(jnp.int32(1), False)
        carries, accs = steps(qss, js, q0s, v_scales, carries, accs, False)
        js, v_scales = key_blocks(jnp.int32(2), False)
        carries, accs = corner(qss, js, v_scales, carries, accs)
        st = lax.while_loop(cond, body, (jnp.int32(2), highest(carries), tuple(carries), tuple(accs)))
        for c, si in enumerate(sis):
            acc = st[3][c]
            q0 = pl.multiple_of(si * rq, rq)
            sq = acc * acc
            s0 = jnp.sum(jnp.where(first, sq, 0.0), axis=-1, keepdims=True)
            s1 = jnp.sum(jnp.where(first, 0.0, sq), axis=-1, keepdims=True)
            ms = jnp.where(first, s0, s1) * (1.0 / dh)
            o_ref[0, pl.ds(q0, rq), :] = (acc * lax.rsqrt(ms + EPS) * nw_ref[...]).astype(o_ref.dtype)
        return 0

    lax.fori_loop(0, n_groups, group, 0)


def _sb_attn(q, kt, v_even, v_odd, nw, *, chains, qb):
    b, s, w = q.shape
    assert s % (chains * qb * SB_BLOCK) == 0 and w % LANES == 0
    n_pairs = w // LANES
    cw = _sb_cumsum_weights()
    tok = pl.BlockSpec((1, s, LANES), lambda bi, hp: (bi, 0, hp))
    return pl.pallas_call(
        functools.partial(_sb_attn_kernel, n_groups=s // (chains * qb * SB_BLOCK), chains=chains, qb=qb),
        grid=(b, n_pairs),
        in_specs=[
            tok,
            pl.BlockSpec((1, LANES, s), lambda bi, hp: (bi, hp, 0)),
            tok, tok,
            pl.BlockSpec((1, LANES), lambda bi, hp: (0, hp)),
            pl.BlockSpec(cw.shape, lambda bi, hp: (0, 0)),
        ],
        out_specs=tok,
        out_shape=jax.ShapeDtypeStruct((b, s, w), BF16),
        compiler_params=pltpu.CompilerParams(
            dimension_semantics=("arbitrary", "arbitrary"), vmem_limit_bytes=VMEM_LIMIT),
        name="sb_attn",
    )(q, kt, v_even, v_odd, nw, cw)


def _hgrn_levels():
    hs = []
    h = HG_CHUNK // 2
    while h >= 1:
        hs.append(h)
        h //= 2
    return hs


def _hgrn_consts():
    n = HG_CHUNK
    t = np.arange(n)[:, None]
    j = np.arange(n)[None, :]
    mats = [j <= t]
    masks = []
    for h in _hgrn_levels():
        mid = (t // (2 * h)) * (2 * h) + h - 1
        upper = (t % (2 * h)) >= h
        if h > 1:
            mats.append(np.where(upper, (j > mid) & (j <= t), (j > t) & (j <= mid)))
        masks.append(((t // (2 * h)) == (j // (2 * h))) & upper & ((j % (2 * h)) < h))
    masks.append(t == j)
    sums = np.concatenate(mats, axis=0).astype(np.float32)
    sums = np.concatenate([sums, sums], axis=1)
    pm = np.concatenate(masks, axis=0).astype(np.float32)
    return jnp.asarray(sums, dtype=BF16), jnp.asarray(pm, dtype=F32)


def _hgrn2_kernel(q_ref, gl_ref, k_ref, v_ref, gate_ref, nw_ref, cs_ref, pm_ref, o_ref,
                  *, n_iters, chunks_per_iter):
    n = HG_CHUNK
    dk = HG_HEAD_DIM
    heads = 2
    levels = _hgrn_levels()
    rows = lax.broadcasted_iota(jnp.int32, (n, heads * dk), 0)
    odd = (rows & 1) != 0

    def body(it, states):
        base = it * (chunks_per_iter * n)
        cs = cs_ref[...]
        half = cs.shape[0] // 2
        units = []
        for c in range(chunks_per_iter):
            sl = pl.ds(pl.multiple_of(base + c * n, n), n)
            g = gl_ref[0, sl, :]
            g_parts = jnp.concatenate(_split2(g), axis=0)
            d = jnp.concatenate(
                [jnp.dot(cs[:half], g_parts, preferred_element_type=F32),
                 jnp.dot(cs[half:], g_parts, preferred_element_type=F32)], axis=0)
            units.append(dict(sl=sl, g=g, d=d, q=q_ref[0, sl, :], k=k_ref[0, sl, :], v=v_ref[0, sl, :]))
        for un in units:
            q, k, g, d = un["q"], un["k"], un["g"], un["d"]
            bc = d[0:n]
            b_last = bc[n - 1:n, :]
            ws = []
            for li, h in enumerate(levels):
                dl = d[(1 + li) * n:(2 + li) * n] if h > 1 else jnp.where(odd, g, 0.0)
                upper = (rows & h) != 0
                ws.append((jnp.where(upper, q, k) * jnp.exp(dl)).astype(BF16))
            un["ws"] = ws
            un["qk"] = q * k
            un["qd"] = (q * jnp.exp(bc)).astype(BF16)
            un["kd"] = (k * jnp.exp(b_last - bc)).astype(BF16)
            un["decay"] = jnp.exp(b_last)
        hslices = [slice(hh * dk, (hh + 1) * dk) for hh in range(heads)]
        for un in units:
            un["gram"] = [[lax.dot_general(w[:, ls], w[:, ls], _NT, preferred_element_type=F32)
                           for w in un["ws"]] for ls in hslices]
            un["kv"] = [lax.dot_general(un["v"][:, ls], un["kd"][:, ls], _TN, preferred_element_type=F32)
                        for ls in hslices]
        for un in units:
            un["st"] = states
            states = tuple(states[hh] * un["decay"][:, ls] + un["kv"][hh] for hh, ls in enumerate(hslices))
            ps = []
            for hh, ls in enumerate(hslices):
                p = pm_ref[len(levels) * n:(len(levels) + 1) * n, :] * jnp.sum(un["qk"][:, ls], axis=1, keepdims=True)
                for li in range(len(levels)):
                    p = p + pm_ref[li * n:(li + 1) * n, :] * un["gram"][hh][li]
                ps.append(p.astype(BF16))
            un["p"] = ps
        for un in units:
            un["o"] = [jnp.dot(un["p"][hh], un["v"][:, ls], preferred_element_type=F32)
                       + lax.dot_general(un["qd"][:, ls], un["st"][hh].astype(BF16), _NT,
                                         preferred_element_type=F32)
                       for hh, ls in enumerate(hslices)]
        for un in units:
            outs = []
            for o in un["o"]:
                ms = jnp.mean(o * o, axis=-1, keepdims=True)
                outs.append(o * lax.rsqrt(ms + EPS))
            o2 = jnp.concatenate(outs, axis=1)
            o_ref[0, un["sl"], :] = (o2 * nw_ref[...] * gate_ref[0, un["sl"], :]).astype(o_ref.dtype)
        return states

    zero = jnp.zeros((dk, dk), F32)
    lax.fori_loop(0, n_iters, body, (zero, zero))


def _hgrn2(q, gl, k, v, gate, nw, *, chunks_per_iter):
    b, s, w = q.shape
    assert s % (HG_CHUNK * chunks_per_iter) == 0 and w % (2 * HG_HEAD_DIM) == 0
    pair = 2 * HG_HEAD_DIM
    heads = w // pair
    cs, pm = _hgrn_consts()
    tok = pl.BlockSpec((1, s, pair), lambda bi, h: (bi, 0, h))
    return pl.pallas_call(
        functools.partial(_hgrn2_kernel, n_iters=s // (HG_CHUNK * chunks_per_iter),
                          chunks_per_iter=chunks_per_iter),
        grid=(b, heads),
        in_specs=[tok, tok, tok, tok, tok,
                  pl.BlockSpec((1, pair), lambda bi, h: (0, h)),
                  pl.BlockSpec(cs.shape, lambda bi, h: (0, 0)),
                  pl.BlockSpec(pm.shape, lambda bi, h: (0, 0))],
        out_specs=tok,
        out_shape=jax.ShapeDtypeStruct((b, s, w), BF16),
        compiler_params=pltpu.CompilerParams(
            dimension_semantics=("arbitrary", "arbitrary"), vmem_limit_bytes=VMEM_LIMIT),
        name="hgrn2",
    )(q, gl, k, v, gate, nw, cs, pm)


def _out_route_kernel(x_ref, sbo_ref, hgo_ref, wo_ref, nw_ref, wr_ref, br_ref, su_ref,
                      h1_ref, m_ref, meta_ref, gates_ref, counts_ref, carry_ref, *, sbw):
    i = pl.program_id(0)

    @pl.when(i == 0)
    def _():
        carry_ref[...] = jnp.zeros_like(carry_ref)

    h1 = (x_ref[...]
          + jnp.dot(sbo_ref[...], wo_ref[0:sbw, :], preferred_element_type=F32)
          + jnp.dot(hgo_ref[...], wo_ref[sbw:, :], preferred_element_type=F32))
    h1_ref[...] = h1
    ms = jnp.mean(h1 * h1, axis=-1, keepdims=True)
    m = h1 * lax.rsqrt(ms + EPS) * nw_ref[...]
    _store_row_tiles(m_ref, m)

    m2 = _split2(m)
    w2 = _split2(wr_ref[...])
    logits = br_ref[...]
    for wi, mi in ((0, 0), (0, 1), (1, 0)):
        logits = logits + lax.dot_general(w2[wi], m2[mi], _NT, preferred_element_type=F32)
    rid = lax.broadcasted_iota(jnp.int32, logits.shape, 0)
    neg = -jnp.inf
    big = jnp.int32(2 * LANES)

    def first_argmax(vals):
        vmax = jnp.max(vals, axis=0, keepdims=True)
        idx = jnp.min(jnp.where(vals == vmax, rid, big), axis=0, keepdims=True)
        return vmax, idx

    is_group = rid < N_GROUPS
    gmax, g_idx = first_argmax(jnp.where(is_group, logits, neg))
    gsum = jnp.sum(jnp.where(is_group, jnp.exp(logits - gmax), 0.0), axis=0, keepdims=True)
    g_prob = 1.0 / gsum
    lo_row = ROUTER_ROW0 + EXPERTS_PER_GROUP * g_idx
    el = jnp.where((rid >= lo_row) & (rid < lo_row + EXPERTS_PER_GROUP), logits, neg)
    v1, i1 = first_argmax(el)
    v2, i2 = first_argmax(jnp.where(rid == i1, neg, el))
    dd = jnp.exp(v2 - v1)
    p1 = 1.0 / (1.0 + dd)
    g1 = p1 * g_prob
    g2 = dd * p1 * g_prob

    hit1 = rid == i1
    hit2 = rid == i2
    onehot = jnp.where(hit1 | hit2, 1.0, 0.0)
    before_cnt = carry_ref[...] + jnp.dot(onehot.astype(BF16), su_ref[...], preferred_element_type=F32)
    r1 = jnp.sum(jnp.where(hit1, before_cnt, 0.0), axis=0, keepdims=True)
    r2 = jnp.sum(jnp.where(hit2, before_cnt, 0.0), axis=0, keepdims=True)
    carry_ref[...] = carry_ref[...] + jnp.sum(onehot, axis=1, keepdims=True)
    counts_ref[...] = carry_ref[...]

    meta_ref[...] = jnp.zeros_like(meta_ref)
    meta_ref[0:1, :] = i1 - ROUTER_ROW0
    meta_ref[1:2, :] = i2 - ROUTER_ROW0
    meta_ref[2:3, :] = r1.astype(jnp.int32)
    meta_ref[3:4, :] = r2.astype(jnp.int32)
    lane_rows = lax.broadcasted_iota(jnp.int32, (LANES, g1.shape[1]), 0)
    gates_ref[...] = jnp.where(lane_rows == 0, g1, jnp.where(lane_rows == 1, g2, 0.0)).T


def _out_route(x2, sbo, hgo, w_out, nw, wr_t, br, *, tm):
    n, d = x2.shape
    assert d == ROW_CHUNKS * LANES
    sbw = sbo.shape[1]
    j = np.arange(tm)[:, None]
    t = np.arange(tm)[None, :]
    su = jnp.asarray((j < t).astype(np.float32), dtype=BF16)
    tok = lambda width: pl.BlockSpec((tm, width), lambda i: (i, 0))
    const = lambda shape: pl.BlockSpec(shape, lambda i: (0,) * len(shape))
    return pl.pallas_call(
        functools.partial(_out_route_kernel, sbw=sbw),
        grid=(n // tm,),
        in_specs=[tok(d), tok(sbw), tok(hgo.shape[1]), const(w_out.shape), const((1, d)),
                  const(wr_t.shape), const(br.shape), const(su.shape)],
        out_specs=[tok(d), pl.BlockSpec((tm * ROW_CHUNKS, LANES), lambda i: (i, 0)),
                   pl.BlockSpec((8, tm), lambda i: (0, i)), tok(LANES), const((ROUTER_ROWS, 1))],
        out_shape=[jax.ShapeDtypeStruct((n, d), F32), jax.ShapeDtypeStruct((n * ROW_CHUNKS, LANES), F32),
                   jax.ShapeDtypeStruct((8, n), jnp.int32), jax.ShapeDtypeStruct((n, LANES), F32),
                   jax.ShapeDtypeStruct((ROUTER_ROWS, 1), F32)],
        scratch_shapes=[pltpu.VMEM((ROUTER_ROWS, 1), F32)],
        compiler_params=pltpu.CompilerParams(
            dimension_semantics=("arbitrary",), vmem_limit_bytes=VMEM_LIMIT),
        name="out_route",
    )(x2, sbo, hgo, w_out, nw, wr_t, br, su)


def _positions_kernel(seg_ref, meta_ref, pos_ref):
    e = meta_ref[0:TOP_K, :]
    start = jnp.zeros(e.shape, jnp.int32)
    for x in range(N_EXPERTS):
        start = jnp.where(e == x, seg_ref[x], start)
    pos_ref[...] = jnp.zeros_like(pos_ref)
    pos_ref[0:TOP_K, :] = start + meta_ref[TOP_K:2 * TOP_K, :]


def _positions(seg, meta):
    return pl.pallas_call(
        _positions_kernel,
        in_specs=[pl.BlockSpec(memory_space=pltpu.SMEM), pl.BlockSpec(memory_space=pltpu.VMEM)],
        out_specs=pl.BlockSpec(memory_space=pltpu.VMEM),
        out_shape=jax.ShapeDtypeStruct(meta.shape, jnp.int32),
        compiler_params=pltpu.CompilerParams(vmem_limit_bytes=VMEM_LIMIT),
        name="positions",
    )(seg, meta)


def _dispatch_kernel(seg_ref, pos_ref, m_ref, xs_ref, zbuf, sem, zsem, *, tm, tmm, n_tiles):
    i = pl.program_id(0)

    rc = ROW_CHUNKS

    def zero_copy(e):
        tail = pl.multiple_of((seg_ref[N_EXPERTS + e] - tmm) * rc, tmm * rc)
        return pltpu.make_async_copy(zbuf, xs_ref.at[pl.ds(tail, tmm * rc), :], zsem)

    @pl.when(i == 0)
    def _():
        zbuf[...] = jnp.zeros_like(zbuf)
        for e in range(N_EXPERTS):
            @pl.when(seg_ref[2 * N_EXPERTS + e] > 0)
            def _():
                zero_copy(e).start()
        for e in range(N_EXPERTS):
            @pl.when(seg_ref[2 * N_EXPERTS + e] > 0)
            def _():
                zero_copy(e).wait()

        def unused_copy(t):
            first = pl.multiple_of(t * (tmm * rc), tmm * rc)
            return pltpu.make_async_copy(zbuf, xs_ref.at[pl.ds(first, tmm * rc), :], zsem)

        def start_unused(t, _):
            unused_copy(t).start()
            return 0

        def wait_unused(t, _):
            unused_copy(t).wait()
            return 0

        lax.fori_loop(seg_ref[3 * N_EXPERTS], n_tiles, start_unused, 0)
        lax.fori_loop(seg_ref[3 * N_EXPERTS], n_tiles, wait_unused, 0)

    for r in range(tm):
        src = m_ref.at[pl.ds(r * rc, rc), :]
        for k in range(TOP_K):
            dst = pl.multiple_of(pos_ref[k, r] * rc, rc)
            pltpu.make_async_copy(src, xs_ref.at[pl.ds(dst, rc), :], sem).start(priority=k)
    for _ in range(TOP_K):
        pltpu.make_async_copy(m_ref, xs_ref.at[pl.ds(0, tm * rc), :], sem).wait()


def _dispatch(seg, pos, m, n_rows, *, tm, tmm):
    n = m.shape[0] // ROW_CHUNKS
    return pl.pallas_call(
        functools.partial(_dispatch_kernel, tm=tm, tmm=tmm, n_tiles=n_rows // tmm),
        grid=(n // tm,),
        in_specs=[pl.BlockSpec(memory_space=pltpu.SMEM),
                  pl.BlockSpec((8, tm), lambda i: (0, i), memory_space=pltpu.SMEM),
                  pl.BlockSpec((tm * ROW_CHUNKS, LANES), lambda i: (i, 0))],
        out_specs=pl.BlockSpec(memory_space=pl.ANY),
        out_shape=jax.ShapeDtypeStruct((n_rows * ROW_CHUNKS, LANES), F32),
        scratch_shapes=[pltpu.VMEM((tmm * ROW_CHUNKS, LANES), F32), pltpu.SemaphoreType.DMA(()),
                        pltpu.SemaphoreType.DMA(())],
        compiler_params=pltpu.CompilerParams(
            dimension_semantics=("arbitrary",), vmem_limit_bytes=VMEM_LIMIT),
        name="dispatch",
    )(seg, pos, m)


def _experts_kernel(te_ref, ts_ref, xs_ref, wg_ref, wu_ref, wd_ref, y_ref, wg_b, wu_b, wd_b, *, tmm):
    t = pl.program_id(0)

    @pl.when((t == 0) | (te_ref[t] != te_ref[jnp.maximum(t - 1, 0)]))
    def _():
        wg_b[...] = wg_ref[0].astype(BF16)
        wu_b[...] = wu_ref[0].astype(BF16)
        wd_b[...] = wd_ref[0].astype(BF16)

    @pl.when(ts_ref[t] == t)
    def _():
        x = _load_row_tiles(xs_ref, tmm).astype(BF16)
        hg = jnp.dot(x, wg_b[...], preferred_element_type=F32)
        hu = jnp.dot(x, wu_b[...], preferred_element_type=F32)
        act = (hg * _sigmoid(hg) * hu).astype(BF16)
        _store_row_tiles(y_ref, jnp.dot(act, wd_b[...], preferred_element_type=F32))

    @pl.when(ts_ref[t] != t)
    def _():
        y_ref[...] = jnp.zeros_like(y_ref)


def _experts(tile_expert, tile_src, xs, wg, wu, wd, *, tmm):
    n_rows = xs.shape[0] // ROW_CHUNKS
    d, de = wg.shape[1], wg.shape[2]
    assert d == ROW_CHUNKS * LANES
    rows = pl.BlockSpec((tmm * ROW_CHUNKS, LANES), lambda t, te, ts: (ts[t], 0))
    grid_spec = pltpu.PrefetchScalarGridSpec(
        num_scalar_prefetch=2,
        grid=(n_rows // tmm,),
        in_specs=[rows,
                  pl.BlockSpec((1, d, de), lambda t, te, ts: (te[t], 0, 0)),
                  pl.BlockSpec((1, d, de), lambda t, te, ts: (te[t], 0, 0)),
                  pl.BlockSpec((1, de, d), lambda t, te, ts: (te[t], 0, 0))],
        out_specs=pl.BlockSpec((tmm * ROW_CHUNKS, LANES), lambda t, te, ts: (t, 0)),
        scratch_shapes=[pltpu.VMEM((d, de), BF16), pltpu.VMEM((d, de), BF16), pltpu.VMEM((de, d), BF16)],
    )
    return pl.pallas_call(
        functools.partial(_experts_kernel, tmm=tmm),
        grid_spec=grid_spec,
        out_shape=jax.ShapeDtypeStruct((n_rows * ROW_CHUNKS, LANES), F32),
        compiler_params=pltpu.CompilerParams(
            dimension_semantics=("arbitrary",), vmem_limit_bytes=VMEM_LIMIT),
        name="experts",
    )(tile_expert, tile_src, xs, wg, wu, wd)


def _combine_kernel(pos_ref, posn_ref, h1_ref, gates_ref, p_ref, pnw_ref, wpp_ref, wpg_ref, fnw_ref,
                    y_ref, o_ref, ybuf_even, ybuf_odd, sem, *, tm, n_steps):
    i = pl.program_id(0)

    rc = ROW_CHUNKS
    ybufs = (ybuf_even, ybuf_odd)

    def gather(p_ref_, slot):
        for r in range(tm):
            for k in range(TOP_K):
                src = pl.multiple_of(p_ref_[k, r] * rc, rc)
                pltpu.make_async_copy(y_ref.at[pl.ds(src, rc), :],
                                      ybufs[slot].at[k, pl.ds(r * rc, rc), :], sem.at[slot]).start(priority=k)

    def wait(slot):
        for k in range(TOP_K):
            pltpu.make_async_copy(y_ref.at[pl.ds(0, tm * rc), :], ybufs[slot].at[k], sem.at[slot]).wait()

    @pl.when(i == 0)
    def _():
        gather(pos_ref, 0)

    def step(slot):
        wait(slot)
        gather(posn_ref, 1 - slot)
        gts = gates_ref[...]
        h2 = (h1_ref[...] + gts[:, 0:1] * _load_row_tiles(ybufs[slot].at[0], tm)
              + gts[:, 1:2] * _load_row_tiles(ybufs[slot].at[1], tm))
        e = jnp.dot(p_ref[...].astype(BF16), wpp_ref[...], preferred_element_type=F32)
        ms = jnp.mean(h2 * h2, axis=-1, keepdims=True)
        hn = (h2 * lax.rsqrt(ms + EPS) * pnw_ref[...]).astype(BF16)
        gate = _sigmoid(jnp.dot(hn, wpg_ref[...], preferred_element_type=F32))
        h3 = h2 + gate * e
        ms3 = jnp.mean(h3 * h3, axis=-1, keepdims=True)
        o_ref[...] = h3 * lax.rsqrt(ms3 + EPS) * fnw_ref[...]

        @pl.when(i == n_steps - 1)
        def _():
            wait(1 - slot)

    for parity in range(2):
        @pl.when(i % 2 == parity)
        def _():
            step(parity)


def _combine(pos, h1, gates, p2, pnw, wpp, wpg, fnw, y, *, tm):
    n, d = h1.shape
    n_steps = n // tm
    tok = lambda width: pl.BlockSpec((tm, width), lambda i: (i, 0))
    const = lambda shape: pl.BlockSpec(shape, lambda i: (0,) * len(shape))
    return pl.pallas_call(
        functools.partial(_combine_kernel, tm=tm, n_steps=n_steps),
        grid=(n_steps,),
        in_specs=[pl.BlockSpec((8, tm), lambda i: (0, i), memory_space=pltpu.SMEM),
                  pl.BlockSpec((8, tm), lambda i: (0, jnp.minimum(i + 1, n_steps - 1)),
                               memory_space=pltpu.SMEM),
                  tok(d), tok(LANES), tok(p2.shape[1]), const((1, d)), const(wpp.shape),
                  const(wpg.shape), const((1, d)),
                  pl.BlockSpec(memory_space=pl.ANY)],
        out_specs=tok(d),
        out_shape=jax.ShapeDtypeStruct((n, d), F32),
        scratch_shapes=[pltpu.VMEM((TOP_K, tm * ROW_CHUNKS, LANES), F32),
                        pltpu.VMEM((TOP_K, tm * ROW_CHUNKS, LANES), F32), pltpu.SemaphoreType.DMA((2,))],
        compiler_params=pltpu.CompilerParams(
            dimension_semantics=("arbitrary",), vmem_limit_bytes=VMEM_LIMIT),
        name="combine",
    )(pos, pos, h1, gates, p2, pnw, wpp, wpg, fnw, y)


def kernel(x, p, attn_norm_w, w_in, sb_norm_w, hg_lower_bounds, hg_norm_w, w_out, ffn_norm_w,
           w_group_router, b_group_router, w_expert_router, b_expert_router, w_exp_gate, w_exp_up,
           w_exp_down, ple_norm_w, w_ple_proj, w_ple_gate, final_norm_w):
    b, s, d = x.shape
    depth = w_in.shape[0]
    assert depth == 1, "single-layer trunk"
    sbw = sb_norm_w.shape[1]
    hgw = hg_norm_w.shape[1]
    n = b * s
    tm_proj = min(512, s)
    tm_route = min(512, n)
    tm_disp = min(512, n)
    tm_comb = min(256, n)
    tmm = min(512, n)

    wi = w_in[0]
    w_main = jnp.concatenate([wi[:, 0:sbw], wi[:, 2 * sbw:]], axis=1).astype(BF16)
    w_kt = wi[:, sbw:2 * sbw].T.astype(BF16)
    gap = ROUTER_ROW0 - N_GROUPS
    tail = ROUTER_ROWS - ROUTER_ROW0 - N_EXPERTS
    wr_t = jnp.concatenate([w_group_router[0].T, jnp.zeros((gap, d), F32), w_expert_router[0].T,
                            jnp.zeros((tail, d), F32)], axis=0)
    br = jnp.concatenate([b_group_router[0], jnp.zeros((gap,), F32), b_expert_router[0],
                          jnp.zeros((tail,), F32)])[:, None]

    sbq, sbkt, sbve, sbvo, hq, hgl, hk, hv, hgate = _in_proj(
        x, attn_norm_w[0][None, :], w_main, w_kt, hg_lower_bounds, tm=tm_proj)
    sbo = _sb_attn(sbq, sbkt, sbve, sbvo, sb_norm_w, chains=SB_CHAINS, qb=SB_QUERY_BLOCKS)
    hgo = _hgrn2(hq, hgl, hk, hv, hgate, hg_norm_w, chunks_per_iter=HG_CHUNKS_PER_ITER)

    h1, m, meta, gates, counts = _out_route(
        x.reshape(n, d), sbo.reshape(n, sbw), hgo.reshape(n, hgw), w_out[0].astype(BF16),
        ffn_norm_w[0][None, :], wr_t, br, tm=tm_route)

    cnt = counts[ROUTER_ROW0:ROUTER_ROW0 + N_EXPERTS, 0].astype(jnp.int32)
    padded = ((cnt + tmm - 1) // tmm) * tmm
    ends = jnp.cumsum(padded)
    starts = ends - padded
    n_rows = n * TOP_K + N_EXPERTS * tmm
    n_tiles = n_rows // tmm
    last_tile = ends[-1] // tmm - 1
    tile_src = jnp.minimum(jnp.arange(n_tiles, dtype=jnp.int32), last_tile)
    tile_expert = jnp.sum((ends[None, :] <= (tile_src * tmm)[:, None]).astype(jnp.int32), axis=1)
    seg = jnp.concatenate([starts, ends, padded, (last_tile + 1)[None],
                           jnp.zeros((LANES - 3 * N_EXPERTS - 1,), jnp.int32)])

    pos = _positions(seg, meta)
    xs = _dispatch(seg, pos, m, n_rows, tm=tm_disp, tmm=tmm)
    y = _experts(tile_expert, tile_src, xs, w_exp_gate[0], w_exp_up[0], w_exp_down[0], tmm=tmm)
    out = _combine(pos, h1, gates, p[0].reshape(n, -1), ple_norm_w[0][None, :],
                   w_ple_proj[0].astype(BF16), w_ple_gate[0].astype(BF16), final_norm_w[None, :],
                   y, tm=tm_comb)
    return out.reshape(b, s, d)
```

```python
import functools

import numpy as np
import jax
import jax.numpy as jnp
from jax import lax
from jax.experimental import pallas as pl
from jax.experimental.pallas import tpu as pltpu

F32 = jnp.float32
BF16 = jnp.bfloat16
EPS = 1e-6

SB_HEADS = 8
SB_HEAD_DIM = 64
HG_HEAD_DIM = 128
HG_CHUNK = 64
HG_CHUNKS_PER_ITER = 8
N_GROUPS = 4
EXPERTS_PER_GROUP = 8
N_EXPERTS = N_GROUPS * EXPERTS_PER_GROUP
TOP_K = 2
LANES = 128
ROW_CHUNKS = 8
ROUTER_ROW0 = 8
ROUTER_ROWS = 48
SB_BLOCK = 128
SB_QUERY_BLOCKS = 1
SB_CHAINS = 16
SB_LOG2_FLOOR = -152.0
LOG2E = 1.4426950408889634
VMEM_LIMIT = 56 * 1024 * 1024

_NT = (((1,), (1,)), ((), ()))
_TN = (((0,), (0,)), ((), ()))


def _sigmoid(x):
    return 1.0 / (1.0 + jnp.exp(-x))


def _store_row_tiles(ref, x):
    rows = x.shape[0]
    for c in range(ROW_CHUNKS):
        ref[pl.ds(c, rows, stride=ROW_CHUNKS), :] = x[:, c * LANES:(c + 1) * LANES]


def _load_row_tiles(ref, rows):
    return jnp.concatenate([ref[pl.ds(c, rows, stride=ROW_CHUNKS), :] for c in range(ROW_CHUNKS)], axis=1)


def _split2(x):
    hi = x.astype(BF16)
    lo = (x - hi.astype(F32)).astype(BF16)
    return hi, lo


def _in_proj_kernel(x_ref, nw_ref, w_ref, wkt_ref, lbp_ref,
                    sbq_ref, sbkt_ref, sbve_ref, sbvo_ref, hq_ref, hgl_ref, hk_ref, hv_ref, hgate_ref,
                    *, sbw, hgw):
    x = x_ref[0]
    ms = jnp.mean(x * x, axis=-1, keepdims=True)
    a = (x * lax.rsqrt(ms + EPS) * nw_ref[...]).astype(BF16)

    def seg(lo, width):
        return jnp.dot(a, w_ref[:, lo:lo + width], preferred_element_type=F32)

    sbq_ref[0] = (seg(0, sbw) * (SB_HEAD_DIM ** -0.5 * LOG2E)).astype(BF16)
    sbkt_ref[0] = lax.dot_general(wkt_ref[...], a, _NT, preferred_element_type=F32).astype(BF16)
    v = seg(sbw, sbw)
    even_head = (lax.broadcasted_iota(jnp.int32, v.shape, 1) & SB_HEAD_DIM) == 0
    sbve_ref[0] = jnp.where(even_head, v, 0.0).astype(BF16)
    sbvo_ref[0] = jnp.where(even_head, 0.0, v).astype(BF16)
    c = 2 * sbw
    q = seg(c, hgw)
    hq_ref[0] = q * _sigmoid(q)
    p0 = lbp_ref[0:1, :]
    p1 = lbp_ref[1:2, :]
    pm = jnp.maximum(p0, p1)
    e0 = jnp.exp(p0 - pm)
    e1 = jnp.exp(p1 - pm)
    lb = e0 / (e0 + e1)
    fz = seg(c + hgw, hgw)
    sg = _sigmoid(fz)
    f = lb + (1.0 - lb) * sg
    hgl_ref[0] = jnp.log(f)
    hk_ref[0] = 1.0 - f
    hv_ref[0] = seg(c + 2 * hgw, hgw).astype(BF16)
    g = seg(c + 3 * hgw, hgw)
    hgate_ref[0] = g * _sigmoid(g)


def _in_proj(x, nw, w_main, w_kt, lbp, *, tm):
    b, s, d = x.shape
    sbw = w_kt.shape[0]
    hgw = (w_main.shape[1] - 2 * sbw) // 4
    tok = lambda width: pl.BlockSpec((1, tm, width), lambda bi, i: (bi, i, 0))
    const = lambda shape: pl.BlockSpec(shape, lambda bi, i: (0,) * len(shape))
    out_shape = [
        jax.ShapeDtypeStruct((b, s, sbw), BF16),
        jax.ShapeDtypeStruct((b, sbw, s), BF16),
        jax.ShapeDtypeStruct((b, s, sbw), BF16),
        jax.ShapeDtypeStruct((b, s, sbw), BF16),
        jax.ShapeDtypeStruct((b, s, hgw), F32),
        jax.ShapeDtypeStruct((b, s, hgw), F32),
        jax.ShapeDtypeStruct((b, s, hgw), F32),
        jax.ShapeDtypeStruct((b, s, hgw), BF16),
        jax.ShapeDtypeStruct((b, s, hgw), F32),
    ]
    out_specs = [tok(sbw), pl.BlockSpec((1, sbw, tm), lambda bi, i: (bi, 0, i)), tok(sbw), tok(sbw),
                 tok(hgw), tok(hgw), tok(hgw), tok(hgw), tok(hgw)]
    return pl.pallas_call(
        functools.partial(_in_proj_kernel, sbw=sbw, hgw=hgw),
        grid=(b, s // tm),
        in_specs=[tok(d), const((1, d)), const(w_main.shape), const(w_kt.shape), const(lbp.shape)],
        out_specs=out_specs,
        out_shape=out_shape,
        compiler_params=pltpu.CompilerParams(
            dimension_semantics=("arbitrary", "arbitrary"), vmem_limit_bytes=VMEM_LIMIT),
        name="in_proj",
    )(x, nw, w_main, w_kt, lbp)


def _sb_cumsum_weights():
    j = np.arange(SB_BLOCK)[:, None]
    s = np.arange(SB_BLOCK)[None, :]
    half = np.concatenate([(j > s).astype(np.float32), np.ones((SB_BLOCK, SB_BLOCK), np.float32)], axis=1)
    return jnp.asarray(np.concatenate([half, half], axis=0), dtype=BF16)


def _sb_attn_kernel(q_ref, kt_ref, ve_ref, vo_ref, nw_ref, cw_ref, o_ref, *, n_groups, chains, qb):
    assert qb == 1, "the corner step assumes one query block per chain"
    blk = SB_BLOCK
    dh = SB_HEAD_DIM
    rq = qb * blk
    rr = 2 * rq
    first = lax.broadcasted_iota(jnp.int32, (rq, LANES), 1) < dh
    rowpos = lax.broadcasted_iota(jnp.int32, (rr, blk), 0) & (rq - 1)
    colpos = lax.broadcasted_iota(jnp.int32, (rr, blk), 1)
    strictly_before = colpos < rowpos
    cw = cw_ref[...]

    def neg_abs(z):
        bits = lax.bitcast_convert_type(z, jnp.uint32) | jnp.uint32(0x80000000)
        return lax.bitcast_convert_type(bits, F32)

    ch = blk // 2
    corner_region = (rowpos < ch) & (colpos >= blk - ch)
    corner_keys = lax.broadcasted_iota(jnp.int32, (2 * ch, blk), 1) >= blk - ch

    def corner_rows(x):
        return jnp.concatenate([x[0:ch], x[rq:rq + ch]], axis=0)

    def corner(qss, js, v_scales, carries, accs):
        k0s = [pl.multiple_of(j * blk, blk) for j in js]
        zs = [jnp.dot(corner_rows(qs), kt_ref[0, :, pl.ds(k0, blk)], preferred_element_type=F32)
              for qs, k0 in zip(qss, k0s)]
        lss, hls = [], []
        for z in zs:
            sp = jnp.log2(1.0 + jnp.exp2(neg_abs(z)))
            ls = jnp.minimum(z, 0.0) - sp
            hi, lo = _split2(jnp.where(corner_keys, ls - z, 0.0))
            lss.append(ls)
            hls.append(jnp.concatenate([hi, lo], axis=1))
        css = [jnp.dot(hl, cw, preferred_element_type=F32) for hl in hls]
        new_carries, new_accs = [], []
        for ls, cs, carry, acc, k0, v_scale in zip(lss, css, carries, accs, k0s, v_scales):
            a = jnp.where(corner_keys, jnp.exp2(ls + cs[:, :blk] + corner_rows(carry)), 0.0).astype(BF16)
            vst = jnp.concatenate([ve_ref[0, pl.ds(k0, blk), :], vo_ref[0, pl.ds(k0, blk), :]], axis=0) * v_scale
            delta = jnp.dot(jnp.concatenate([a[:ch], a[ch:]], axis=1), vst, preferred_element_type=F32)
            new_accs.append(jnp.concatenate([acc[:ch] + delta, acc[ch:]], axis=0))
            tot = cs[:, blk:]
            new_carries.append(jnp.concatenate(
                [carry[0:ch] + tot[:ch], carry[ch:rq], carry[rq:rq + ch] + tot[ch:], carry[rq + ch:]], axis=0))
        return new_carries, new_accs

    def steps(qss, js, v_scales, carries, accs, masked, skip=None):
        k0s = [pl.multiple_of(j * blk, blk) for j in js]
        zs = [jnp.dot(qs, kt_ref[0, :, pl.ds(k0, blk)], preferred_element_type=F32)
              for qs, k0 in zip(qss, k0s)]
        lss, hls, befores = [], [], []
        for z in zs:
            sp = jnp.log2(1.0 + jnp.exp2(neg_abs(z)))
            ls = jnp.minimum(z, 0.0) - sp
            lk = ls - z
            before = None
            if masked:
                before = strictly_before
                lk = jnp.where(before, lk, 0.0)
            if skip is not None:
                lk = jnp.where(skip, 0.0, lk)
            hi, lo = _split2(lk)
            lss.append(ls)
            hls.append(jnp.concatenate([hi, lo], axis=1))
            befores.append(before)
        css = [jnp.dot(hl, cw, preferred_element_type=F32) for hl in hls]
        abs_ = []
        for ls, cs, carry, before in zip(lss, css, carries, befores):
            a = jnp.exp2(ls + cs[:, :blk] + carry)
            if masked:
                a = jnp.where(before, a, 0.0)
            if skip is not None:
                a = jnp.where(skip, 0.0, a)
            ab = a.astype(BF16)
            abs_.append(jnp.concatenate([ab[:rq], ab[rq:]], axis=1))
        new_accs = []
        for ab, k0, v_scale, acc in zip(abs_, k0s, v_scales, accs):
            vst = jnp.concatenate([ve_ref[0, pl.ds(k0, blk), :], vo_ref[0, pl.ds(k0, blk), :]], axis=0)
            if v_scale is not None:
                vst = vst * v_scale
            new_accs.append(acc + jnp.dot(ab, vst, preferred_element_type=F32))
        new_carries = [carry + cs[:, blk:] for carry, cs in zip(carries, css)]
        return new_carries, new_accs

    def highest(carries):
        m = carries[0]
        for c in carries[1:]:
            m = jnp.maximum(m, c)
        return jnp.max(m)

    def group(gi, _):
        sis = [gi * chains + c for c in range(chains)]
        q0s = [pl.multiple_of(si * rq, rq) for si in sis]
        qss = []
        for q0 in q0s:
            q2 = q_ref[0, pl.ds(q0, rq), :].astype(F32)
            qss.append(jnp.concatenate([jnp.where(first, q2, 0.0), jnp.where(first, 0.0, q2)],
                                       axis=0).astype(BF16))
        carries = [jnp.zeros((rr, blk), F32)] * chains
        accs = [jnp.zeros((rq, LANES), F32)] * chains
        for i in range(qb):
            carries, accs = steps(qss, [si * qb + (qb - 1 - i) for si in sis], [None] * chains,
                                  carries, accs, True)

        def cond(st):
            n, top = st[0], st[1]
            return (n <= sis[-1] * qb) & (top > SB_LOG2_FLOOR)

        def key_blocks(n, last_chain_in_range):
            js, v_scales = [], []
            for c, si in enumerate(sis):
                d = si * qb - n
                if c < chains - 1 or not last_chain_in_range:
                    v_scales.append(jnp.where(d >= 0, 1.0, 0.0).astype(BF16))
                    d = jnp.maximum(d, 0)
                else:
                    v_scales.append(None)
                js.append(d)
            return js, v_scales

        def body(st):
            n = st[0]
            js, v_scales = key_blocks(n, True)
            skip = corner_region & (n == 2)
            cs_out, as_out = steps(qss, js, v_scales, list(st[2]), list(st[3]), False, skip)
            return n + 1, highest(cs_out), tuple(cs_out), tuple(as_out)

        js, v_scales = key_blocks(jnp.int32(1), False)
        carries, accs = steps(qss, js, v_scales, carries, accs, False)
        js, v_scales = key_blocks(jnp.int32(2), False)
        carries, accs = corner(qss, js, v_scales, carries, accs)
        st = lax.while_loop(cond, body, (jnp.int32(2), highest(carries), tuple(carries), tuple(accs)))
        for c, si in enumerate(sis):
            acc = st[3][c]
            q0 = pl.multiple_of(si * rq, rq)
            sq = acc * acc
            s0 = jnp.sum(jnp.where(first, sq, 0.0), axis=-1, keepdims=True)
            s1 = jnp.sum(jnp.where(first, 0.0, sq), axis=-1, keepdims=True)
            ms = jnp.where(first, s0, s1) * (1.0 / dh)
            o_ref[0, pl.ds(q0, rq), :] = (acc * lax.rsqrt(ms + EPS) * nw_ref[...]).astype(o_ref.dtype)
        return 0

    lax.fori_loop(0, n_groups, group, 0)


def _sb_attn(q, kt, v_even, v_odd, nw, *, chains, qb):
    b, s, w = q.shape
    assert s % (chains * qb * SB_BLOCK) == 0 and w % LANES == 0
    n_pairs = w // LANES
    cw = _sb_cumsum_weights()
    tok = pl.BlockSpec((1, s, LANES), lambda bi, hp: (bi, 0, hp))
    return pl.pallas_call(
        functools.partial(_sb_attn_kernel, n_groups=s // (chains * qb * SB_BLOCK), chains=chains, qb=qb),
        grid=(b, n_pairs),
        in_specs=[
            tok,
            pl.BlockSpec((1, LANES, s), lambda bi, hp: (bi, hp, 0)),
            tok, tok,
            pl.BlockSpec((1, LANES), lambda bi, hp: (0, hp)),
            pl.BlockSpec(cw.shape, lambda bi, hp: (0, 0)),
        ],
        out_specs=tok,
        out_shape=jax.ShapeDtypeStruct((b, s, w), BF16),
        compiler_params=pltpu.CompilerParams(
            dimension_semantics=("arbitrary", "arbitrary"), vmem_limit_bytes=VMEM_LIMIT),
        name="sb_attn",
    )(q, kt, v_even, v_odd, nw, cw)


def _hgrn_levels():
    hs = []
    h = HG_CHUNK // 2
    while h >= 1:
        hs.append(h)
        h //= 2
    return hs


def _hgrn_consts():
    n = HG_CHUNK
    t = np.arange(n)[:, None]
    j = np.arange(n)[None, :]
    mats = [j <= t]
    masks = []
    for h in _hgrn_levels():
        mid = (t // (2 * h)) * (2 * h) + h - 1
        upper = (t % (2 * h)) >= h
        if h > 1:
            mats.append(np.where(upper, (j > mid) & (j <= t), (j > t) & (j <= mid)))
        masks.append(((t // (2 * h)) == (j // (2 * h))) & upper & ((j % (2 * h)) < h))
    masks.append(t == j)
    sums = np.concatenate(mats, axis=0).astype(np.float32)
    sums = np.concatenate([sums, sums], axis=1)
    pm = np.concatenate(masks, axis=0).astype(np.float32)
    return jnp.asarray(sums, dtype=BF16), jnp.asarray(pm, dtype=F32)


def _hgrn2_kernel(q_ref, gl_ref, k_ref, v_ref, gate_ref, nw_ref, cs_ref, pm_ref, o_ref,
                  *, n_iters, chunks_per_iter):
    n = HG_CHUNK
    dk = HG_HEAD_DIM
    heads = 2
    levels = _hgrn_levels()
    rows = lax.broadcasted_iota(jnp.int32, (n, heads * dk), 0)
    odd = (rows & 1) != 0

    def body(it, states):
        base = it * (chunks_per_iter * n)
        cs = cs_ref[...]
        half = cs.shape[0] // 2
        units = []
        for c in range(chunks_per_iter):
            sl = pl.ds(pl.multiple_of(base + c * n, n), n)
            g = gl_ref[0, sl, :]
            g_parts = jnp.concatenate(_split2(g), axis=0)
            d = jnp.concatenate(
                [jnp.dot(cs[:half], g_parts, preferred_element_type=F32),
                 jnp.dot(cs[half:], g_parts, preferred_element_type=F32)], axis=0)
            units.append(dict(sl=sl, g=g, d=d, q=q_ref[0, sl, :], k=k_ref[0, sl, :], v=v_ref[0, sl, :]))
        for un in units:
            q, k, g, d = un["q"], un["k"], un["g"], un["d"]
            bc = d[0:n]
            b_last = bc[n - 1:n, :]
            ws = []
            for li, h in enumerate(levels):
                dl = d[(1 + li) * n:(2 + li) * n] if h > 1 else jnp.where(odd, g, 0.0)
                upper = (rows & h) != 0
                ws.append((jnp.where(upper, q, k) * jnp.exp(dl)).astype(BF16))
            un["ws"] = ws
            un["qk"] = q * k
            un["qd"] = (q * jnp.exp(bc)).astype(BF16)
            un["kd"] = (k * jnp.exp(b_last - bc)).astype(BF16)
            un["decay"] = jnp.exp(b_last)
        hslices = [slice(hh * dk, (hh + 1) * dk) for hh in range(heads)]
        for un in units:
            un["gram"] = [[lax.dot_general(w[:, ls], w[:, ls], _NT, preferred_element_type=F32)
                           for w in un["ws"]] for ls in hslices]
            un["kv"] = [lax.dot_general(un["v"][:, ls], un["kd"][:, ls], _TN, preferred_element_type=F32)
                        for ls in hslices]
        for un in units:
            un["st"] = states
            states = tuple(states[hh] * un["decay"][:, ls] + un["kv"][hh] for hh, ls in enumerate(hslices))
            ps = []
            for hh, ls in enumerate(hslices):
                p = pm_ref[len(levels) * n:(len(levels) + 1) * n, :] * jnp.sum(un["qk"][:, ls], axis=1, keepdims=True)
                for li in range(len(levels)):
                    p = p + pm_ref[li * n:(li + 1) * n, :] * un["gram"][hh][li]
                ps.append(p.astype(BF16))
            un["p"] = ps
        for un in units:
            un["o"] = [jnp.dot(un["p"][hh], un["v"][:, ls], preferred_element_type=F32)
                       + lax.dot_general(un["qd"][:, ls], un["st"][hh].astype(BF16), _NT,
                                         preferred_element_type=F32)
                       for hh, ls in enumerate(hslices)]
        for un in units:
            outs = []
            for o in un["o"]:
                ms = jnp.mean(o * o, axis=-1, keepdims=True)
                outs.append(o * lax.rsqrt(ms + EPS))
            o2 = jnp.concatenate(outs, axis=1)
            o_ref[0, un["sl"], :] = (o2 * nw_ref[...] * gate_ref[0, un["sl"], :]).astype(o_ref.dtype)
        return states

    zero = jnp.zeros((dk, dk), F32)
    lax.fori_loop(0, n_iters, body, (zero, zero))


def _hgrn2(q, gl, k, v, gate, nw, *, chunks_per_iter):
    b, s, w = q.shape
    assert s % (HG_CHUNK * chunks_per_iter) == 0 and w % (2 * HG_HEAD_DIM) == 0
    pair = 2 * HG_HEAD_DIM
    heads = w // pair
    cs, pm = _hgrn_consts()
    tok = pl.BlockSpec((1, s, pair), lambda bi, h: (bi, 0, h))
    return pl.pallas_call(
        functools.partial(_hgrn2_kernel, n_iters=s // (HG_CHUNK * chunks_per_iter),
                          chunks_per_iter=chunks_per_iter),
        grid=(b, heads),
        in_specs=[tok, tok, tok, tok, tok,
                  pl.BlockSpec((1, pair), lambda bi, h: (0, h)),
                  pl.BlockSpec(cs.shape, lambda bi, h: (0, 0)),
                  pl.BlockSpec(pm.shape, lambda bi, h: (0, 0))],
        out_specs=tok,
        out_shape=jax.ShapeDtypeStruct((b, s, w), BF16),
        compiler_params=pltpu.CompilerParams(
            dimension_semantics=("arbitrary", "arbitrary"), vmem_limit_bytes=VMEM_LIMIT),
        name="hgrn2",
    )(q, gl, k, v, gate, nw, cs, pm)


def _out_route_kernel(x_ref, sbo_ref, hgo_ref, wo_ref, nw_ref, wr_ref, br_ref, su_ref,
                      h1_ref, m_ref, meta_ref, gates_ref, counts_ref, carry_ref, *, sbw):
    i = pl.program_id(0)

    @pl.when(i == 0)
    def _():
        carry_ref[...] = jnp.zeros_like(carry_ref)

    h1 = (x_ref[...]
          + jnp.dot(sbo_ref[...], wo_ref[0:sbw, :], preferred_element_type=F32)
          + jnp.dot(hgo_ref[...], wo_ref[sbw:, :], preferred_element_type=F32))
    h1_ref[...] = h1
    ms = jnp.mean(h1 * h1, axis=-1, keepdims=True)
    m = h1 * lax.rsqrt(ms + EPS) * nw_ref[...]
    _store_row_tiles(m_ref, m)

    m2 = _split2(m)
    w2 = _split2(wr_ref[...])
    logits = br_ref[...]
    for wi, mi in ((0, 0), (0, 1), (1, 0)):
        logits = logits + lax.dot_general(w2[wi], m2[mi], _NT, preferred_element_type=F32)
    rid = lax.broadcasted_iota(jnp.int32, logits.shape, 0)
    neg = -jnp.inf
    big = jnp.int32(2 * LANES)

    def first_argmax(vals):
        vmax = jnp.max(vals, axis=0, keepdims=True)
        idx = jnp.min(jnp.where(vals == vmax, rid, big), axis=0, keepdims=True)
        return vmax, idx

    is_group = rid < N_GROUPS
    gmax, g_idx = first_argmax(jnp.where(is_group, logits, neg))
    gsum = jnp.sum(jnp.where(is_group, jnp.exp(logits - gmax), 0.0), axis=0, keepdims=True)
    g_prob = 1.0 / gsum
    lo_row = ROUTER_ROW0 + EXPERTS_PER_GROUP * g_idx
    el = jnp.where((rid >= lo_row) & (rid < lo_row + EXPERTS_PER_GROUP), logits, neg)
    v1, i1 = first_argmax(el)
    v2, i2 = first_argmax(jnp.where(rid == i1, neg, el))
    dd = jnp.exp(v2 - v1)
    p1 = 1.0 / (1.0 + dd)
    g1 = p1 * g_prob
    g2 = dd * p1 * g_prob

    hit1 = rid == i1
    hit2 = rid == i2
    onehot = jnp.where(hit1 | hit2, 1.0, 0.0)
    before_cnt = carry_ref[...] + jnp.dot(onehot.astype(BF16), su_ref[...], preferred_element_type=F32)
    r1 = jnp.sum(jnp.where(hit1, before_cnt, 0.0), axis=0, keepdims=True)
    r2 = jnp.sum(jnp.where(hit2, before_cnt, 0.0), axis=0, keepdims=True)
    carry_ref[...] = carry_ref[...] + jnp.sum(onehot, axis=1, keepdims=True)
    counts_ref[...] = carry_ref[...]

    meta_ref[...] = jnp.zeros_like(meta_ref)
    meta_ref[0:1, :] = i1 - ROUTER_ROW0
    meta_ref[1:2, :] = i2 - ROUTER_ROW0
    meta_ref[2:3, :] = r1.astype(jnp.int32)
    meta_ref[3:4, :] = r2.astype(jnp.int32)
    lane_rows = lax.broadcasted_iota(jnp.int32, (LANES, g1.shape[1]), 0)
    gates_ref[...] = jnp.where(lane_rows == 0, g1, jnp.where(lane_rows == 1, g2, 0.0)).T


def _out_route(x2, sbo, hgo, w_out, nw, wr_t, br, *, tm):
    n, d = x2.shape
    assert d == ROW_CHUNKS * LANES
    sbw = sbo.shape[1]
    j = np.arange(tm)[:, None]
    t = np.arange(tm)[None, :]
    su = jnp.asarray((j < t).astype(np.float32), dtype=BF16)
    tok = lambda width: pl.BlockSpec((tm, width), lambda i: (i, 0))
    const = lambda shape: pl.BlockSpec(shape, lambda i: (0,) * len(shape))
    return pl.pallas_call(
        functools.partial(_out_route_kernel, sbw=sbw),
        grid=(n // tm,),
        in_specs=[tok(d), tok(sbw), tok(hgo.shape[1]), const(w_out.shape), const((1, d)),
                  const(wr_t.shape), const(br.shape), const(su.shape)],
        out_specs=[tok(d), pl.BlockSpec((tm * ROW_CHUNKS, LANES), lambda i: (i, 0)),
                   pl.BlockSpec((8, tm), lambda i: (0, i)), tok(LANES), const((ROUTER_ROWS, 1))],
        out_shape=[jax.ShapeDtypeStruct((n, d), F32), jax.ShapeDtypeStruct((n * ROW_CHUNKS, LANES), F32),
                   jax.ShapeDtypeStruct((8, n), jnp.int32), jax.ShapeDtypeStruct((n, LANES), F32),
                   jax.ShapeDtypeStruct((ROUTER_ROWS, 1), F32)],
        scratch_shapes=[pltpu.VMEM((ROUTER_ROWS, 1), F32)],
        compiler_params=pltpu.CompilerParams(
            dimension_semantics=("arbitrary",), vmem_limit_bytes=VMEM_LIMIT),
        name="out_route",
    )(x2, sbo, hgo, w_out, nw, wr_t, br, su)


def _positions_kernel(seg_ref, meta_ref, pos_ref):
    e = meta_ref[0:TOP_K, :]
    start = jnp.zeros(e.shape, jnp.int32)
    for x in range(N_EXPERTS):
        start = jnp.where(e == x, seg_ref[x], start)
    pos_ref[...] = jnp.zeros_like(pos_ref)
    pos_ref[0:TOP_K, :] = start + meta_ref[TOP_K:2 * TOP_K, :]


def _positions(seg, meta):
    return pl.pallas_call(
        _positions_kernel,
        in_specs=[pl.BlockSpec(memory_space=pltpu.SMEM), pl.BlockSpec(memory_space=pltpu.VMEM)],
        out_specs=pl.BlockSpec(memory_space=pltpu.VMEM),
        out_shape=jax.ShapeDtypeStruct(meta.shape, jnp.int32),
        compiler_params=pltpu.CompilerParams(vmem_limit_bytes=VMEM_LIMIT),
        name="positions",
    )(seg, meta)


def _dispatch_kernel(seg_ref, pos_ref, m_ref, xs_ref, zbuf, sem, zsem, *, tm, tmm, n_tiles):
    i = pl.program_id(0)

    rc = ROW_CHUNKS

    def zero_copy(e):
        tail = pl.multiple_of((seg_ref[N_EXPERTS + e] - tmm) * rc, tmm * rc)
        return pltpu.make_async_copy(zbuf, xs_ref.at[pl.ds(tail, tmm * rc), :], zsem)

    @pl.when(i == 0)
    def _():
        zbuf[...] = jnp.zeros_like(zbuf)
        for e in range(N_EXPERTS):
            @pl.when(seg_ref[2 * N_EXPERTS + e] > 0)
            def _():
                zero_copy(e).start()
        for e in range(N_EXPERTS):
            @pl.when(seg_ref[2 * N_EXPERTS + e] > 0)
            def _():
                zero_copy(e).wait()

        def unused_copy(t):
            first = pl.multiple_of(t * (tmm * rc), tmm * rc)
            return pltpu.make_async_copy(zbuf, xs_ref.at[pl.ds(first, tmm * rc), :], zsem)

        def start_unused(t, _):
            unused_copy(t).start()
            return 0

        def wait_unused(t, _):
            unused_copy(t).wait()
            return 0

        lax.fori_loop(seg_ref[3 * N_EXPERTS], n_tiles, start_unused, 0)
        lax.fori_loop(seg_ref[3 * N_EXPERTS], n_tiles, wait_unused, 0)

    for r in range(tm):
        src = m_ref.at[pl.ds(r * rc, rc), :]
        for k in range(TOP_K):
            dst = pl.multiple_of(pos_ref[k, r] * rc, rc)
            pltpu.make_async_copy(src, xs_ref.at[pl.ds(dst, rc), :], sem).start(priority=k)
    for _ in range(TOP_K):
        pltpu.make_async_copy(m_ref, xs_ref.at[pl.ds(0, tm * rc), :], sem).wait()


def _dispatch(seg, pos, m, n_rows, *, tm, tmm):
    n = m.shape[0] // ROW_CHUNKS
    return pl.pallas_call(
        functools.partial(_dispatch_kernel, tm=tm, tmm=tmm, n_tiles=n_rows // tmm),
        grid=(n // tm,),
        in_specs=[pl.BlockSpec(memory_space=pltpu.SMEM),
                  pl.BlockSpec((8, tm), lambda i: (0, i), memory_space=pltpu.SMEM),
                  pl.BlockSpec((tm * ROW_CHUNKS, LANES), lambda i: (i, 0))],
        out_specs=pl.BlockSpec(memory_space=pl.ANY),
        out_shape=jax.ShapeDtypeStruct((n_rows * ROW_CHUNKS, LANES), F32),
        scratch_shapes=[pltpu.VMEM((tmm * ROW_CHUNKS, LANES), F32), pltpu.SemaphoreType.DMA(()),
                        pltpu.SemaphoreType.DMA(())],
        compiler_params=pltpu.CompilerParams(
            dimension_semantics=("arbitrary",), vmem_limit_bytes=VMEM_LIMIT),
        name="dispatch",
    )(seg, pos, m)


def _experts_kernel(te_ref, ts_ref, xs_ref, wg_ref, wu_ref, wd_ref, y_ref, wg_b, wu_b, wd_b, *, tmm):
    t = pl.program_id(0)

    @pl.when((t == 0) | (te_ref[t] != te_ref[jnp.maximum(t - 1, 0)]))
    def _():
        wg_b[...] = wg_ref[0].astype(BF16)
        wu_b[...] = wu_ref[0].astype(BF16)
        wd_b[...] = wd_ref[0].astype(BF16)

    @pl.when(ts_ref[t] == t)
    def _():
        x = _load_row_tiles(xs_ref, tmm).astype(BF16)
        hg = jnp.dot(x, wg_b[...], preferred_element_type=F32)
        hu = jnp.dot(x, wu_b[...], preferred_element_type=F32)
        act = (hg * _sigmoid(hg) * hu).astype(BF16)
        _store_row_tiles(y_ref, jnp.dot(act, wd_b[...], preferred_element_type=F32))

    @pl.when(ts_ref[t] != t)
    def _():
        y_ref[...] = jnp.zeros_like(y_ref)


def _experts(tile_expert, tile_src, xs, wg, wu, wd, *, tmm):
    n_rows = xs.shape[0] // ROW_CHUNKS
    d, de = wg.shape[1], wg.shape[2]
    assert d == ROW_CHUNKS * LANES
    rows = pl.BlockSpec((tmm * ROW_CHUNKS, LANES), lambda t, te, ts: (ts[t], 0))
    grid_spec = pltpu.PrefetchScalarGridSpec(
        num_scalar_prefetch=2,
        grid=(n_rows // tmm,),
        in_specs=[rows,
                  pl.BlockSpec((1, d, de), lambda t, te, ts: (te[t], 0, 0)),
                  pl.BlockSpec((1, d, de), lambda t, te, ts: (te[t], 0, 0)),
                  pl.BlockSpec((1, de, d), lambda t, te, ts: (te[t], 0, 0))],
        out_specs=pl.BlockSpec((tmm * ROW_CHUNKS, LANES), lambda t, te, ts: (t, 0)),
        scratch_shapes=[pltpu.VMEM((d, de), BF16), pltpu.VMEM((d, de), BF16), pltpu.VMEM((de, d), BF16)],
    )
    return pl.pallas_call(
        functools.partial(_experts_kernel, tmm=tmm),
        grid_spec=grid_spec,
        out_shape=jax.ShapeDtypeStruct((n_rows * ROW_CHUNKS, LANES), F32),
        compiler_params=pltpu.CompilerParams(
            dimension_semantics=("arbitrary",), vmem_limit_bytes=VMEM_LIMIT),
        name="experts",
    )(tile_expert, tile_src, xs, wg, wu, wd)


def _combine_kernel(pos_ref, posn_ref, h1_ref, gates_ref, p_ref, pnw_ref, wpp_ref, wpg_ref, fnw_ref,
                    y_ref, o_ref, ybuf_even, ybuf_odd, sem, *, tm, n_steps):
    i = pl.program_id(0)

    rc = ROW_CHUNKS
    ybufs = (ybuf_even, ybuf_odd)

    def gather(p_ref_, slot):
        for r in range(tm):
            for k in range(TOP_K):
                src = pl.multiple_of(p_ref_[k, r] * rc, rc)
                pltpu.make_async_copy(y_ref.at[pl.ds(src, rc), :],
                                      ybufs[slot].at[k, pl.ds(r * rc, rc), :], sem.at[slot]).start(priority=k)

    def wait(slot):
        for k in range(TOP_K):
            pltpu.make_async_copy(y_ref.at[pl.ds(0, tm * rc), :], ybufs[slot].at[k], sem.at[slot]).wait()

    @pl.when(i == 0)
    def _():
        gather(pos_ref, 0)

    def step(slot):
        wait(slot)
        gather(posn_ref, 1 - slot)
        gts = gates_ref[...]
        h2 = (h1_ref[...] + gts[:, 0:1] * _load_row_tiles(ybufs[slot].at[0], tm)
              + gts[:, 1:2] * _load_row_tiles(ybufs[slot].at[1], tm))
        e = jnp.dot(p_ref[...].astype(BF16), wpp_ref[...], preferred_element_type=F32)
        ms = jnp.mean(h2 * h2, axis=-1, keepdims=True)
        hn = (h2 * lax.rsqrt(ms + EPS) * pnw_ref[...]).astype(BF16)
        gate = _sigmoid(jnp.dot(hn, wpg_ref[...], preferred_element_type=F32))
        h3 = h2 + gate * e
        ms3 = jnp.mean(h3 * h3, axis=-1, keepdims=True)
        o_ref[...] = h3 * lax.rsqrt(ms3 + EPS) * fnw_ref[...]

        @pl.when(i == n_steps - 1)
        def _():
            wait(1 - slot)

    for parity in range(2):
        @pl.when(i % 2 == parity)
        def _():
            step(parity)


def _combine(pos, h1, gates, p2, pnw, wpp, wpg, fnw, y, *, tm):
    n, d = h1.shape
    n_steps = n // tm
    tok = lambda width: pl.BlockSpec((tm, width), lambda i: (i, 0))
    const = lambda shape: pl.BlockSpec(shape, lambda i: (0,) * len(shape))
    return pl.pallas_call(
        functools.partial(_combine_kernel, tm=tm, n_steps=n_steps),
        grid=(n_steps,),
        in_specs=[pl.BlockSpec((8, tm), lambda i: (0, i), memory_space=pltpu.SMEM),
                  pl.BlockSpec((8, tm), lambda i: (0, jnp.minimum(i + 1, n_steps - 1)),
                               memory_space=pltpu.SMEM),
                  tok(d), tok(LANES), tok(p2.shape[1]), const((1, d)), const(wpp.shape),
                  const(wpg.shape), const((1, d)),
                  pl.BlockSpec(memory_space=pl.ANY)],
        out_specs=tok(d),
        out_shape=jax.ShapeDtypeStruct((n, d), F32),
        scratch_shapes=[pltpu.VMEM((TOP_K, tm * ROW_CHUNKS, LANES), F32),
                        pltpu.VMEM((TOP_K, tm * ROW_CHUNKS, LANES), F32), pltpu.SemaphoreType.DMA((2,))],
        compiler_params=pltpu.CompilerParams(
            dimension_semantics=("arbitrary",), vmem_limit_bytes=VMEM_LIMIT),
        name="combine",
    )(pos, pos, h1, gates, p2, pnw, wpp, wpg, fnw, y)


def kernel(x, p, attn_norm_w, w_in, sb_norm_w, hg_lower_bounds, hg_norm_w, w_out, ffn_norm_w,
           w_group_router, b_group_router, w_expert_router, b_expert_router, w_exp_gate, w_exp_up,
           w_exp_down, ple_norm_w, w_ple_proj, w_ple_gate, final_norm_w):
    b, s, d = x.shape
    depth = w_in.shape[0]
    assert depth == 1, "single-layer trunk"
    sbw = sb_norm_w.shape[1]
    hgw = hg_norm_w.shape[1]
    n = b * s
    tm_proj = min(512, s)
    tm_route = min(512, n)
    tm_disp = min(512, n)
    tm_comb = min(256, n)
    tmm = min(512, n)

    wi = w_in[0]
    w_main = jnp.concatenate([wi[:, 0:sbw], wi[:, 2 * sbw:]], axis=1).astype(BF16)
    w_kt = wi[:, sbw:2 * sbw].T.astype(BF16)
    gap = ROUTER_ROW0 - N_GROUPS
    tail = ROUTER_ROWS - ROUTER_ROW0 - N_EXPERTS
    wr_t = jnp.concatenate([w_group_router[0].T, jnp.zeros((gap, d), F32), w_expert_router[0].T,
                            jnp.zeros((tail, d), F32)], axis=0)
    br = jnp.concatenate([b_group_router[0], jnp.zeros((gap,), F32), b_expert_router[0],
                          jnp.zeros((tail,), F32)])[:, None]

    sbq, sbkt, sbve, sbvo, hq, hgl, hk, hv, hgate = _in_proj(
        x, attn_norm_w[0][None, :], w_main, w_kt, hg_lower_bounds, tm=tm_proj)
    sbo = _sb_attn(sbq, sbkt, sbve, sbvo, sb_norm_w, chains=SB_CHAINS, qb=SB_QUERY_BLOCKS)
    hgo = _hgrn2(hq, hgl, hk, hv, hgate, hg_norm_w, chunks_per_iter=HG_CHUNKS_PER_ITER)

    h1, m, meta, gates, counts = _out_route(
        x.reshape(n, d), sbo.reshape(n, sbw), hgo.reshape(n, hgw), w_out[0].astype(BF16),
        ffn_norm_w[0][None, :], wr_t, br, tm=tm_route)

    cnt = counts[ROUTER_ROW0:ROUTER_ROW0 + N_EXPERTS, 0].astype(jnp.int32)
    padded = ((cnt + tmm - 1) // tmm) * tmm
    ends = jnp.cumsum(padded)
    starts = ends - padded
    n_rows = n * TOP_K + N_EXPERTS * tmm
    n_tiles = n_rows // tmm
    last_tile = ends[-1] // tmm - 1
    tile_src = jnp.minimum(jnp.arange(n_tiles, dtype=jnp.int32), last_tile)
    tile_expert = jnp.sum((ends[None, :] <= (tile_src * tmm)[:, None]).astype(jnp.int32), axis=1)
    seg = jnp.concatenate([starts, ends, padded, (last_tile + 1)[None],
                           jnp.zeros((LANES - 3 * N_EXPERTS - 1,), jnp.int32)])

    pos = _positions(seg, meta)
    xs = _dispatch(seg, pos, m, n_rows, tm=tm_disp, tmm=tmm)
    y = _experts(tile_expert, tile_src, xs, w_exp_gate[0], w_exp_up[0], w_exp_down[0], tmm=tmm)
    out = _combine(pos, h1, gates, p[0].reshape(n, -1), ple_norm_w[0][None, :],
                   w_ple_proj[0].astype(BF16), w_ple_gate[0].astype(BF16), final_norm_w[None, :],
                   y, tm=tm_comb)
    return out.reshape(b, s, d)
```

```python
import functools

import numpy as np
import jax
import jax.numpy as jnp
from jax import lax
from jax.experimental import pallas as pl
from jax.experimental.pallas import tpu as pltpu

F32 = jnp.float32
BF16 = jnp.bfloat16
EPS = 1e-6

SB_HEAD_DIM = 64
HG_HEAD_DIM = 128
HG_CHUNK = 64
HG_CHUNKS_PER_ITER = 8
N_GROUPS = 4
EXPERTS_PER_GROUP = 8
N_EXPERTS = N_GROUPS * EXPERTS_PER_GROUP
TOP_K = 2
LANES = 128
ROW_CHUNKS = 8
ROUTER_ROW0 = 8
ROUTER_ROWS = 48
SB_BLOCK = 128
SB_QUERY_BLOCKS = 1
SB_CHAINS = 16
SB_LOG2_FLOOR = -152.0
LOG2E = 1.4426950408889634
TOKEN_TILE = 512
COMBINE_TILE = 256
EXPERT_TILE = 512
VMEM_LIMIT = 56 * 1024 * 1024

_NT = (((1,), (1,)), ((), ()))
_TN = (((0,), (0,)), ((), ()))


def _sigmoid(x):
    return 1.0 / (1.0 + jnp.exp(-x))


def _store_row_tiles(ref, x):
    rows = x.shape[0]
    for c in range(ROW_CHUNKS):
        ref[pl.ds(c, rows, stride=ROW_CHUNKS), :] = x[:, c * LANES:(c + 1) * LANES]


def _load_row_tiles(ref, rows):
    return jnp.concatenate([ref[pl.ds(c, rows, stride=ROW_CHUNKS), :] for c in range(ROW_CHUNKS)], axis=1)


def _split2(x):
    hi = x.astype(BF16)
    lo = (x - hi.astype(F32)).astype(BF16)
    return hi, lo


def _in_proj_kernel(x_ref, nw_ref, w_ref, wkt_ref, lbp_ref,
                    sbq_ref, sbkt_ref, sbve_ref, sbvo_ref, hq_ref, hgl_ref, hk_ref, hv_ref, hgate_ref,
                    *, sbw, hgw):
    x = x_ref[0]
    ms = jnp.mean(x * x, axis=-1, keepdims=True)
    a = (x * lax.rsqrt(ms + EPS) * nw_ref[...]).astype(BF16)

    def seg(lo, width):
        return jnp.dot(a, w_ref[:, lo:lo + width], preferred_element_type=F32)

    sbq_ref[0] = (seg(0, sbw) * (SB_HEAD_DIM ** -0.5 * LOG2E)).astype(BF16)
    sbkt_ref[0] = lax.dot_general(wkt_ref[...], a, _NT, preferred_element_type=F32).astype(BF16)
    v = seg(sbw, sbw)
    even_head = (lax.broadcasted_iota(jnp.int32, v.shape, 1) & SB_HEAD_DIM) == 0
    sbve_ref[0] = jnp.where(even_head, v, 0.0).astype(BF16)
    sbvo_ref[0] = jnp.where(even_head, 0.0, v).astype(BF16)
    c = 2 * sbw
    q = seg(c, hgw)
    hq_ref[0] = q * _sigmoid(q)
    p0 = lbp_ref[0:1, :]
    p1 = lbp_ref[1:2, :]
    pm = jnp.maximum(p0, p1)
    e0 = jnp.exp(p0 - pm)
    e1 = jnp.exp(p1 - pm)
    lb = e0 / (e0 + e1)
    fz = seg(c + hgw, hgw)
    sg = _sigmoid(fz)
    f = lb + (1.0 - lb) * sg
    hgl_ref[0] = jnp.log(f)
    hk_ref[0] = 1.0 - f
    hv_ref[0] = seg(c + 2 * hgw, hgw).astype(BF16)
    g = seg(c + 3 * hgw, hgw)
    hgate_ref[0] = g * _sigmoid(g)


def _in_proj(x, nw, w_main, w_kt, lbp, *, tm):
    b, s, d = x.shape
    sbw = w_kt.shape[0]
    hgw = (w_main.shape[1] - 2 * sbw) // 4
    tok = lambda width: pl.BlockSpec((1, tm, width), lambda bi, i: (bi, i, 0))
    const = lambda shape: pl.BlockSpec(shape, lambda bi, i: (0,) * len(shape))
    out_shape = [
        jax.ShapeDtypeStruct((b, s, sbw), BF16),
        jax.ShapeDtypeStruct((b, sbw, s), BF16),
        jax.ShapeDtypeStruct((b, s, sbw), BF16),
        jax.ShapeDtypeStruct((b, s, sbw), BF16),
        jax.ShapeDtypeStruct((b, s, hgw), F32),
        jax.ShapeDtypeStruct((b, s, hgw), F32),
        jax.ShapeDtypeStruct((b, s, hgw), F32),
        jax.ShapeDtypeStruct((b, s, hgw), BF16),
        jax.ShapeDtypeStruct((b, s, hgw), F32),
    ]
    out_specs = [tok(sbw), pl.BlockSpec((1, sbw, tm), lambda bi, i: (bi, 0, i)), tok(sbw), tok(sbw),
                 tok(hgw), tok(hgw), tok(hgw), tok(hgw), tok(hgw)]
    return pl.pallas_call(
        functools.partial(_in_proj_kernel, sbw=sbw, hgw=hgw),
        grid=(b, s // tm),
        in_specs=[tok(d), const((1, d)), const(w_main.shape), const(w_kt.shape), const(lbp.shape)],
        out_specs=out_specs,
        out_shape=out_shape,
        compiler_params=pltpu.CompilerParams(
            dimension_semantics=("arbitrary", "arbitrary"), vmem_limit_bytes=VMEM_LIMIT),
        name="in_proj",
    )(x, nw, w_main, w_kt, lbp)


def _sb_cumsum_weights():
    j = np.arange(SB_BLOCK)[:, None]
    s = np.arange(SB_BLOCK)[None, :]
    half = np.concatenate([(j > s).astype(np.float32), np.ones((SB_BLOCK, SB_BLOCK), np.float32)], axis=1)
    return jnp.asarray(np.concatenate([half, half], axis=0), dtype=BF16)


def _sb_attn_kernel(q_ref, kt_ref, ve_ref, vo_ref, nw_ref, cw_ref, o_ref, *, n_groups, chains, qb):
    assert qb == 1, "the corner step assumes one query block per chain"
    blk = SB_BLOCK
    dh = SB_HEAD_DIM
    rq = qb * blk
    rr = 2 * rq
    first = lax.broadcasted_iota(jnp.int32, (rq, LANES), 1) < dh
    rowpos = lax.broadcasted_iota(jnp.int32, (rr, blk), 0) & (rq - 1)
    colpos = lax.broadcasted_iota(jnp.int32, (rr, blk), 1)
    strictly_before = colpos < rowpos
    cw = cw_ref[...]

    def neg_abs(z):
        bits = lax.bitcast_convert_type(z, jnp.uint32) | jnp.uint32(0x80000000)
        return lax.bitcast_convert_type(bits, F32)

    ch = blk // 2
    corner_region = (rowpos < ch) & (colpos >= blk - ch)
    corner_keys = lax.broadcasted_iota(jnp.int32, (2 * ch, blk), 1) >= blk - ch

    def corner_rows(x):
        return jnp.concatenate([x[0:ch], x[rq:rq + ch]], axis=0)

    def corner(qss, js, v_scales, carries, accs):
        k0s = [pl.multiple_of(j * blk, blk) for j in js]
        zs = [jnp.dot(corner_rows(qs), kt_ref[0, :, pl.ds(k0, blk)], preferred_element_type=F32)
              for qs, k0 in zip(qss, k0s)]
        lss, hls = [], []
        for z in zs:
            sp = jnp.log2(1.0 + jnp.exp2(neg_abs(z)))
            ls = jnp.minimum(z, 0.0) - sp
            hi, lo = _split2(jnp.where(corner_keys, ls - z, 0.0))
            lss.append(ls)
            hls.append(jnp.concatenate([hi, lo], axis=1))
        css = [jnp.dot(hl, cw, preferred_element_type=F32) for hl in hls]
        new_carries, new_accs = [], []
        for ls, cs, carry, acc, k0, v_scale in zip(lss, css, carries, accs, k0s, v_scales):
            a = jnp.where(corner_keys, jnp.exp2(ls + cs[:, :blk] + corner_rows(carry)), 0.0).astype(BF16)
            vst = jnp.concatenate([ve_ref[0, pl.ds(k0, blk), :], vo_ref[0, pl.ds(k0, blk), :]], axis=0) * v_scale
            delta = jnp.dot(jnp.concatenate([a[:ch], a[ch:]], axis=1), vst, preferred_element_type=F32)
            new_accs.append(jnp.concatenate([acc[:ch] + delta, acc[ch:]], axis=0))
            tot = cs[:, blk:]
            new_carries.append(jnp.concatenate(
                [carry[0:ch] + tot[:ch], carry[ch:rq], carry[rq:rq + ch] + tot[ch:], carry[rq + ch:]], axis=0))
        return new_carries, new_accs

    def steps(qss, js, v_scales, carries, accs, masked, skip=None):
        k0s = [pl.multiple_of(j * blk, blk) for j in js]
        zs = [jnp.dot(qs, kt_ref[0, :, pl.ds(k0, blk)], preferred_element_type=F32)
              for qs, k0 in zip(qss, k0s)]
        lss, hls, befores = [], [], []
        for z in zs:
            sp = jnp.log2(1.0 + jnp.exp2(neg_abs(z)))
            ls = jnp.minimum(z, 0.0) - sp
            lk = ls - z
            before = None
            if masked:
                before = strictly_before
                lk = jnp.where(before, lk, 0.0)
            if skip is not None:
                lk = jnp.where(skip, 0.0, lk)
            hi, lo = _split2(lk)
            lss.append(ls)
            hls.append(jnp.concatenate([hi, lo], axis=1))
            befores.append(before)
        css = [jnp.dot(hl, cw, preferred_element_type=F32) for hl in hls]
        abs_ = []
        for ls, cs, carry, before in zip(lss, css, carries, befores):
            a = jnp.exp2(ls + cs[:, :blk] + carry)
            if masked:
                a = jnp.where(before, a, 0.0)
            if skip is not None:
                a = jnp.where(skip, 0.0, a)
            ab = a.astype(BF16)
            abs_.append(jnp.concatenate([ab[:rq], ab[rq:]], axis=1))
        new_accs = []
        for ab, k0, v_scale, acc in zip(abs_, k0s, v_scales, accs):
            vst = jnp.concatenate([ve_ref[0, pl.ds(k0, blk), :], vo_ref[0, pl.ds(k0, blk), :]], axis=0)
            if v_scale is not None:
                vst = vst * v_scale
            new_accs.append(acc + jnp.dot(ab, vst, preferred_element_type=F32))
        new_carries = [carry + cs[:, blk:] for carry, cs in zip(carries, css)]
        return new_carries, new_accs

    def highest(carries):
        m = carries[0]
        for c in carries[1:]:
            m = jnp.maximum(m, c)
        return jnp.max(m)

    def group(gi, _):
        sis = [gi * chains + c for c in range(chains)]
        q0s = [pl.multiple_of(si * rq, rq) for si in sis]
        qss = []
        for q0 in q0s:
            q2 = q_ref[0, pl.ds(q0, rq), :].astype(F32)
            qss.append(jnp.concatenate([jnp.where(first, q2, 0.0), jnp.where(first, 0.0, q2)],
                                       axis=0).astype(BF16))
        carries = [jnp.zeros((rr, blk), F32)] * chains
        accs = [jnp.zeros((rq, LANES), F32)] * chains
        for i in range(qb):
            carries, accs = steps(qss, [si * qb + (qb - 1 - i) for si in sis], [None] * chains,
                                  carries, accs, True)

        def cond(st):
            n, top = st[0], st[1]
            return (n <= sis[-1] * qb) & (top > SB_LOG2_FLOOR)

        def key_blocks(n, last_chain_in_range):
            js, v_scales = [], []
            for c, si in enumerate(sis):
                d = si * qb - n
                if c < chains - 1 or not last_chain_in_range:
                    v_scales.append(jnp.where(d >= 0, 1.0, 0.0).astype(BF16))
                    d = jnp.maximum(d, 0)
                else:
                    v_scales.append(None)
                js.append(d)
            return js, v_scales

        def body(st):
            n = st[0]
            js, v_scales = key_blocks(n, True)
            skip = corner_region & (n == 2)
            cs_out, as_out = steps(qss, js, v_scales, list(st[2]), list(st[3]), False, skip)
            return n + 1, highest(cs_out), tuple(cs_out), tuple(as_out)

        js, v_scales = key_blocks(jnp.int32(1), False)
        carries, accs = steps(qss, js, v_scales, carries, accs, False)
        js, v_scales = key_blocks(jnp.int32(2), False)
        carries, accs = corner(qss, js, v_scales, carries, accs)
        st = lax.while_loop(cond, body, (jnp.int32(2), highest(carries), tuple(carries), tuple(accs)))
        for c, si in enumerate(sis):
            acc = st[3][c]
            q0 = pl.multiple_of(si * rq, rq)
            sq = acc * acc
            s0 = jnp.sum(jnp.where(first, sq, 0.0), axis=-1, keepdims=True)
            s1 = jnp.sum(jnp.where(first, 0.0, sq), axis=-1, keepdims=True)
            ms = jnp.where(first, s0, s1) * (1.0 / dh)
            o_ref[0, pl.ds(q0, rq), :] = (acc * lax.rsqrt(ms + EPS) * nw_ref[...]).astype(o_ref.dtype)
        return 0

    lax.fori_loop(0, n_groups, group, 0)


def _sb_attn(q, kt, v_even, v_odd, nw, *, chains, qb):
    b, s, w = q.shape
    assert s % (chains * qb * SB_BLOCK) == 0 and w % LANES == 0
    n_pairs = w // LANES
    cw = _sb_cumsum_weights()
    tok = pl.BlockSpec((1, s, LANES), lambda bi, hp: (bi, 0, hp))
    return pl.pallas_call(
        functools.partial(_sb_attn_kernel, n_groups=s // (chains * qb * SB_BLOCK), chains=chains, qb=qb),
        grid=(b, n_pairs),
        in_specs=[
            tok,
            pl.BlockSpec((1, LANES, s), lambda bi, hp: (bi, hp, 0)),
            tok, tok,
            pl.BlockSpec((1, LANES), lambda bi, hp: (0, hp)),
            pl.BlockSpec(cw.shape, lambda bi, hp: (0, 0)),
        ],
        out_specs=tok,
        out_shape=jax.ShapeDtypeStruct((b, s, w), BF16),
        compiler_params=pltpu.CompilerParams(
            dimension_semantics=("arbitrary", "arbitrary"), vmem_limit_bytes=VMEM_LIMIT),
        name="sb_attn",
    )(q, kt, v_even, v_odd, nw, cw)


def _hgrn_levels():
    hs = []
    h = HG_CHUNK // 2
    while h >= 1:
        hs.append(h)
        h //= 2
    return hs


def _hgrn_consts():
    n = HG_CHUNK
    t = np.arange(n)[:, None]
    j = np.arange(n)[None, :]
    mats = [j <= t]
    masks = []
    for h in _hgrn_levels():
        mid = (t // (2 * h)) * (2 * h) + h - 1
        upper = (t % (2 * h)) >= h
        if h > 1:
            mats.append(np.where(upper, (j > mid) & (j <= t), (j > t) & (j <= mid)))
        masks.append(((t // (2 * h)) == (j // (2 * h))) & upper & ((j % (2 * h)) < h))
    masks.append(t == j)
    sums = np.concatenate(mats, axis=0).astype(np.float32)
    sums = np.concatenate([sums, sums], axis=1)
    pm = np.concatenate(masks, axis=0).astype(np.float32)
    return jnp.asarray(sums, dtype=BF16), jnp.asarray(pm, dtype=F32)


def _hgrn2_kernel(q_ref, gl_ref, k_ref, v_ref, gate_ref, nw_ref, cs_ref, pm_ref, o_ref,
                  *, n_iters, chunks_per_iter):
    n = HG_CHUNK
    dk = HG_HEAD_DIM
    heads = 2
    levels = _hgrn_levels()
    rows = lax.broadcasted_iota(jnp.int32, (n, heads * dk), 0)
    odd = (rows & 1) != 0

    def body(it, states):
        base = it * (chunks_per_iter * n)
        cs = cs_ref[...]
        half = cs.shape[0] // 2
        units = []
        for c in range(chunks_per_iter):
            sl = pl.ds(pl.multiple_of(base + c * n, n), n)
            g = gl_ref[0, sl, :]
            g_parts = jnp.concatenate(_split2(g), axis=0)
            d = jnp.concatenate(
                [jnp.dot(cs[:half], g_parts, preferred_element_type=F32),
                 jnp.dot(cs[half:], g_parts, preferred_element_type=F32)], axis=0)
            units.append(dict(sl=sl, g=g, d=d, q=q_ref[0, sl, :], k=k_ref[0, sl, :], v=v_ref[0, sl, :]))
        for un in units:
            q, k, g, d = un["q"], un["k"], un["g"], un["d"]
            bc = d[0:n]
            b_last = bc[n - 1:n, :]
            ws = []
            for li, h in enumerate(levels):
                dl = d[(1 + li) * n:(2 + li) * n] if h > 1 else jnp.where(odd, g, 0.0)
                upper = (rows & h) != 0
                ws.append((jnp.where(upper, q, k) * jnp.exp(dl)).astype(BF16))
            un["ws"] = ws
            un["qk"] = q * k
            un["qd"] = (q * jnp.exp(bc)).astype(BF16)
            un["kd"] = (k * jnp.exp(b_last - bc)).astype(BF16)
            un["decay"] = jnp.exp(b_last)
        hslices = [slice(hh * dk, (hh + 1) * dk) for hh in range(heads)]
        for un in units:
            un["gram"] = [[lax.dot_general(w[:, ls], w[:, ls], _NT, preferred_element_type=F32)
                           for w in un["ws"]] for ls in hslices]
            un["kv"] = [lax.dot_general(un["v"][:, ls], un["kd"][:, ls], _TN, preferred_element_type=F32)
                        for ls in hslices]
        for un in units:
            un["st"] = states
            states = tuple(states[hh] * un["decay"][:, ls] + un["kv"][hh] for hh, ls in enumerate(hslices))
            ps = []
            for hh, ls in enumerate(hslices):
                p = pm_ref[len(levels) * n:(len(levels) + 1) * n, :] * jnp.sum(un["qk"][:, ls], axis=1, keepdims=True)
                for li in range(len(levels)):
                    p = p + pm_ref[li * n:(li + 1) * n, :] * un["gram"][hh][li]
                ps.append(p.astype(BF16))
            un["p"] = ps
        for un in units:
            un["o"] = [jnp.dot(un["p"][hh], un["v"][:, ls], preferred_element_type=F32)
                       + lax.dot_general(un["qd"][:, ls], un["st"][hh].astype(BF16), _NT,
                                         preferred_element_type=F32)
                       for hh, ls in enumerate(hslices)]
        for un in units:
            outs = []
            for o in un["o"]:
                ms = jnp.mean(o * o, axis=-1, keepdims=True)
                outs.append(o * lax.rsqrt(ms + EPS))
            o2 = jnp.concatenate(outs, axis=1)
            o_ref[0, un["sl"], :] = (o2 * nw_ref[...] * gate_ref[0, un["sl"], :]).astype(o_ref.dtype)
        return states

    zero = jnp.zeros((dk, dk), F32)
    lax.fori_loop(0, n_iters, body, (zero, zero))


def _hgrn2(q, gl, k, v, gate, nw, *, chunks_per_iter):
    b, s, w = q.shape
    assert s % (HG_CHUNK * chunks_per_iter) == 0 and w % (2 * HG_HEAD_DIM) == 0
    pair = 2 * HG_HEAD_DIM
    heads = w // pair
    cs, pm = _hgrn_consts()
    tok = pl.BlockSpec((1, s, pair), lambda bi, h: (bi, 0, h))
    return pl.pallas_call(
        functools.partial(_hgrn2_kernel, n_iters=s // (HG_CHUNK * chunks_per_iter),
                          chunks_per_iter=chunks_per_iter),
        grid=(b, heads),
        in_specs=[tok, tok, tok, tok, tok,
                  pl.BlockSpec((1, pair), lambda bi, h: (0, h)),
                  pl.BlockSpec(cs.shape, lambda bi, h: (0, 0)),
                  pl.BlockSpec(pm.shape, lambda bi, h: (0, 0))],
        out_specs=tok,
        out_shape=jax.ShapeDtypeStruct((b, s, w), BF16),
        compiler_params=pltpu.CompilerParams(
            dimension_semantics=("arbitrary", "arbitrary"), vmem_limit_bytes=VMEM_LIMIT),
        name="hgrn2",
    )(q, gl, k, v, gate, nw, cs, pm)


def _out_route_kernel(x_ref, sbo_ref, hgo_ref, wo_ref, nw_ref, wr_ref, br_ref, su_ref,
                      h1_ref, m_ref, meta_ref, gates_ref, counts_ref, carry_ref, *, sbw):
    i = pl.program_id(0)

    @pl.when(i == 0)
    def _():
        carry_ref[...] = jnp.zeros_like(carry_ref)

    h1 = (x_ref[...]
          + jnp.dot(sbo_ref[...], wo_ref[0:sbw, :], preferred_element_type=F32)
          + jnp.dot(hgo_ref[...], wo_ref[sbw:, :], preferred_element_type=F32))
    h1_ref[...] = h1
    ms = jnp.mean(h1 * h1, axis=-1, keepdims=True)
    m = h1 * lax.rsqrt(ms + EPS) * nw_ref[...]
    _store_row_tiles(m_ref, m)

    m2 = _split2(m)
    w2 = _split2(wr_ref[...])
    logits = br_ref[...]
    for wi, mi in ((0, 0), (0, 1), (1, 0)):
        logits = logits + lax.dot_general(w2[wi], m2[mi], _NT, preferred_element_type=F32)
    rid = lax.broadcasted_iota(jnp.int32, logits.shape, 0)
    neg = -jnp.inf
    big = jnp.int32(2 * LANES)

    def first_argmax(vals):
        vmax = jnp.max(vals, axis=0, keepdims=True)
        idx = jnp.min(jnp.where(vals == vmax, rid, big), axis=0, keepdims=True)
        return vmax, idx

    is_group = rid < N_GROUPS
    gmax, g_idx = first_argmax(jnp.where(is_group, logits, neg))
    gsum = jnp.sum(jnp.where(is_group, jnp.exp(logits - gmax), 0.0), axis=0, keepdims=True)
    g_prob = 1.0 / gsum
    lo_row = ROUTER_ROW0 + EXPERTS_PER_GROUP * g_idx
    el = jnp.where((rid >= lo_row) & (rid < lo_row + EXPERTS_PER_GROUP), logits, neg)
    v1, i1 = first_argmax(el)
    v2, i2 = first_argmax(jnp.where(rid == i1, neg, el))
    dd = jnp.exp(v2 - v1)
    p1 = 1.0 / (1.0 + dd)
    g1 = p1 * g_prob
    g2 = dd * p1 * g_prob

    hit1 = rid == i1
    hit2 = rid == i2
    onehot = jnp.where(hit1 | hit2, 1.0, 0.0)
    before_cnt = carry_ref[...] + jnp.dot(onehot.astype(BF16), su_ref[...], preferred_element_type=F32)
    r1 = jnp.sum(jnp.where(hit1, before_cnt, 0.0), axis=0, keepdims=True)
    r2 = jnp.sum(jnp.where(hit2, before_cnt, 0.0), axis=0, keepdims=True)
    carry_ref[...] = carry_ref[...] + jnp.sum(onehot, axis=1, keepdims=True)
    counts_ref[...] = carry_ref[...]

    meta_ref[...] = jnp.zeros_like(meta_ref)
    meta_ref[0:1, :] = i1 - ROUTER_ROW0
    meta_ref[1:2, :] = i2 - ROUTER_ROW0
    meta_ref[2:3, :] = r1.astype(jnp.int32)
    meta_ref[3:4, :] = r2.astype(jnp.int32)
    lane_rows = lax.broadcasted_iota(jnp.int32, (LANES, g1.shape[1]), 0)
    gates_ref[...] = jnp.where(lane_rows == 0, g1, jnp.where(lane_rows == 1, g2, 0.0)).T


def _out_route(x2, sbo, hgo, w_out, nw, wr_t, br, *, tm):
    n, d = x2.shape
    assert d == ROW_CHUNKS * LANES
    sbw = sbo.shape[1]
    j = np.arange(tm)[:, None]
    t = np.arange(tm)[None, :]
    su = jnp.asarray((j < t).astype(np.float32), dtype=BF16)
    tok = lambda width: pl.BlockSpec((tm, width), lambda i: (i, 0))
    const = lambda shape: pl.BlockSpec(shape, lambda i: (0,) * len(shape))
    return pl.pallas_call(
        functools.partial(_out_route_kernel, sbw=sbw),
        grid=(n // tm,),
        in_specs=[tok(d), tok(sbw), tok(hgo.shape[1]), const(w_out.shape), const((1, d)),
                  const(wr_t.shape), const(br.shape), const(su.shape)],
        out_specs=[tok(d), pl.BlockSpec((tm * ROW_CHUNKS, LANES), lambda i: (i, 0)),
                   pl.BlockSpec((8, tm), lambda i: (0, i)), tok(LANES), const((ROUTER_ROWS, 1))],
        out_shape=[jax.ShapeDtypeStruct((n, d), F32), jax.ShapeDtypeStruct((n * ROW_CHUNKS, LANES), F32),
                   jax.ShapeDtypeStruct((8, n), jnp.int32), jax.ShapeDtypeStruct((n, LANES), F32),
                   jax.ShapeDtypeStruct((ROUTER_ROWS, 1), F32)],
        scratch_shapes=[pltpu.VMEM((ROUTER_ROWS, 1), F32)],
        compiler_params=pltpu.CompilerParams(
            dimension_semantics=("arbitrary",), vmem_limit_bytes=VMEM_LIMIT),
        name="out_route",
    )(x2, sbo, hgo, w_out, nw, wr_t, br, su)


def _positions_kernel(seg_ref, meta_ref, pos_ref):
    e = meta_ref[0:TOP_K, :]
    start = jnp.zeros(e.shape, jnp.int32)
    for x in range(N_EXPERTS):
        start = jnp.where(e == x, seg_ref[x], start)
    pos_ref[...] = jnp.zeros_like(pos_ref)
    pos_ref[0:TOP_K, :] = start + meta_ref[TOP_K:2 * TOP_K, :]


def _positions(seg, meta):
    return pl.pallas_call(
        _positions_kernel,
        in_specs=[pl.BlockSpec(memory_space=pltpu.SMEM), pl.BlockSpec(memory_space=pltpu.VMEM)],
        out_specs=pl.BlockSpec(memory_space=pltpu.VMEM),
        out_shape=jax.ShapeDtypeStruct(meta.shape, jnp.int32),
        compiler_params=pltpu.CompilerParams(vmem_limit_bytes=VMEM_LIMIT),
        name="positions",
    )(seg, meta)


def _dispatch_kernel(seg_ref, pos_ref, m_ref, xs_ref, zbuf, sem, zsem, *, tm, tmm, n_tiles):
    i = pl.program_id(0)

    rc = ROW_CHUNKS

    def zero_copy(e):
        tail = pl.multiple_of((seg_ref[N_EXPERTS + e] - tmm) * rc, tmm * rc)
        return pltpu.make_async_copy(zbuf, xs_ref.at[pl.ds(tail, tmm * rc), :], zsem)

    @pl.when(i == 0)
    def _():
        zbuf[...] = jnp.zeros_like(zbuf)
        for e in range(N_EXPERTS):
            @pl.when(seg_ref[2 * N_EXPERTS + e] > 0)
            def _():
                zero_copy(e).start()
        for e in range(N_EXPERTS):
            @pl.when(seg_ref[2 * N_EXPERTS + e] > 0)
            def _():
                zero_copy(e).wait()

        def unused_copy(t):
            first = pl.multiple_of(t * (tmm * rc), tmm * rc)
            return pltpu.make_async_copy(zbuf, xs_ref.at[pl.ds(first, tmm * rc), :], zsem)

        def start_unused(t, _):
            unused_copy(t).start()
            return 0

        def wait_unused(t, _):
            unused_copy(t).wait()
            return 0

        lax.fori_loop(seg_ref[3 * N_EXPERTS], n_tiles, start_unused, 0)
        lax.fori_loop(seg_ref[3 * N_EXPERTS], n_tiles, wait_unused, 0)

    for r in range(tm):
        src = m_ref.at[pl.ds(r * rc, rc), :]
        for k in range(TOP_K):
            dst = pl.multiple_of(pos_ref[k, r] * rc, rc)
            pltpu.make_async_copy(src, xs_ref.at[pl.ds(dst, rc), :], sem).start(priority=k)
    for _ in range(TOP_K):
        pltpu.make_async_copy(m_ref, xs_ref.at[pl.ds(0, tm * rc), :], sem).wait()


def _dispatch(seg, pos, m, n_rows, *, tm, tmm):
    n = m.shape[0] // ROW_CHUNKS
    return pl.pallas_call(
        functools.partial(_dispatch_kernel, tm=tm, tmm=tmm, n_tiles=n_rows // tmm),
        grid=(n // tm,),
        in_specs=[pl.BlockSpec(memory_space=pltpu.SMEM),
                  pl.BlockSpec((8, tm), lambda i: (0, i), memory_space=pltpu.SMEM),
                  pl.BlockSpec((tm * ROW_CHUNKS, LANES), lambda i: (i, 0))],
        out_specs=pl.BlockSpec(memory_space=pl.ANY),
        out_shape=jax.ShapeDtypeStruct((n_rows * ROW_CHUNKS, LANES), F32),
        scratch_shapes=[pltpu.VMEM((tmm * ROW_CHUNKS, LANES), F32), pltpu.SemaphoreType.DMA(()),
                        pltpu.SemaphoreType.DMA(())],
        compiler_params=pltpu.CompilerParams(
            dimension_semantics=("arbitrary",), vmem_limit_bytes=VMEM_LIMIT),
        name="dispatch",
    )(seg, pos, m)


def _experts_kernel(te_ref, ts_ref, xs_ref, wg_ref, wu_ref, wd_ref, y_ref, wg_b, wu_b, wd_b, *, tmm):
    t = pl.program_id(0)

    @pl.when((t == 0) | (te_ref[t] != te_ref[jnp.maximum(t - 1, 0)]))
    def _():
        wg_b[...] = wg_ref[0].astype(BF16)
        wu_b[...] = wu_ref[0].astype(BF16)
        wd_b[...] = wd_ref[0].astype(BF16)

    @pl.when(ts_ref[t] == t)
    def _():
        x = _load_row_tiles(xs_ref, tmm).astype(BF16)
        hg = jnp.dot(x, wg_b[...], preferred_element_type=F32)
        hu = jnp.dot(x, wu_b[...], preferred_element_type=F32)
        act = (hg * _sigmoid(hg) * hu).astype(BF16)
        _store_row_tiles(y_ref, jnp.dot(act, wd_b[...], preferred_element_type=F32))

    @pl.when(ts_ref[t] != t)
    def _():
        y_ref[...] = jnp.zeros_like(y_ref)


def _experts(tile_expert, tile_src, xs, wg, wu, wd, *, tmm):
    n_rows = xs.shape[0] // ROW_CHUNKS
    d, de = wg.shape[1], wg.shape[2]
    assert d == ROW_CHUNKS * LANES
    rows = pl.BlockSpec((tmm * ROW_CHUNKS, LANES), lambda t, te, ts: (ts[t], 0))
    grid_spec = pltpu.PrefetchScalarGridSpec(
        num_scalar_prefetch=2,
        grid=(n_rows // tmm,),
        in_specs=[rows,
                  pl.BlockSpec((1, d, de), lambda t, te, ts: (te[t], 0, 0)),
                  pl.BlockSpec((1, d, de), lambda t, te, ts: (te[t], 0, 0)),
                  pl.BlockSpec((1, de, d), lambda t, te, ts: (te[t], 0, 0))],
        out_specs=pl.BlockSpec((tmm * ROW_CHUNKS, LANES), lambda t, te, ts: (t, 0)),
        scratch_shapes=[pltpu.VMEM((d, de), BF16), pltpu.VMEM((d, de), BF16), pltpu.VMEM((de, d), BF16)],
    )
    return pl.pallas_call(
        functools.partial(_experts_kernel, tmm=tmm),
        grid_spec=grid_spec,
        out_shape=jax.ShapeDtypeStruct((n_rows * ROW_CHUNKS, LANES), F32),
        compiler_params=pltpu.CompilerParams(
            dimension_semantics=("arbitrary",), vmem_limit_bytes=VMEM_LIMIT),
        name="experts",
    )(tile_expert, tile_src, xs, wg, wu, wd)


def _combine_kernel(pos_ref, posn_ref, h1_ref, gates_ref, p_ref, pnw_ref, wpp_ref, wpg_ref, fnw_ref,
                    y_ref, o_ref, ybuf_even, ybuf_odd, sem, *, tm, n_steps):
    i = pl.program_id(0)

    rc = ROW_CHUNKS
    ybufs = (ybuf_even, ybuf_odd)

    def gather(p_ref_, slot):
        for r in range(tm):
            for k in range(TOP_K):
                src = pl.multiple_of(p_ref_[k, r] * rc, rc)
                pltpu.make_async_copy(y_ref.at[pl.ds(src, rc), :],
                                      ybufs[slot].at[k, pl.ds(r * rc, rc), :], sem.at[slot]).start(priority=k)

    def wait(slot):
        for k in range(TOP_K):
            pltpu.make_async_copy(y_ref.at[pl.ds(0, tm * rc), :], ybufs[slot].at[k], sem.at[slot]).wait()

    @pl.when(i == 0)
    def _():
        gather(pos_ref, 0)

    def step(slot):
        wait(slot)
        gather(posn_ref, 1 - slot)
        gts = gates_ref[...]
        h2 = (h1_ref[...] + gts[:, 0:1] * _load_row_tiles(ybufs[slot].at[0], tm)
              + gts[:, 1:2] * _load_row_tiles(ybufs[slot].at[1], tm))
        e = jnp.dot(p_ref[...].astype(BF16), wpp_ref[...], preferred_element_type=F32)
        ms = jnp.mean(h2 * h2, axis=-1, keepdims=True)
        hn = (h2 * lax.rsqrt(ms + EPS) * pnw_ref[...]).astype(BF16)
        gate = _sigmoid(jnp.dot(hn, wpg_ref[...], preferred_element_type=F32))
        h3 = h2 + gate * e
        ms3 = jnp.mean(h3 * h3, axis=-1, keepdims=True)
        o_ref[...] = h3 * lax.rsqrt(ms3 + EPS) * fnw_ref[...]

        @pl.when(i == n_steps - 1)
        def _():
            wait(1 - slot)

    for parity in range(2):
        @pl.when(i % 2 == parity)
        def _():
            step(parity)


def _combine(pos, h1, gates, p2, pnw, wpp, wpg, fnw, y, *, tm):
    n, d = h1.shape
    n_steps = n // tm
    tok = lambda width: pl.BlockSpec((tm, width), lambda i: (i, 0))
    const = lambda shape: pl.BlockSpec(shape, lambda i: (0,) * len(shape))
    return pl.pallas_call(
        functools.partial(_combine_kernel, tm=tm, n_steps=n_steps),
        grid=(n_steps,),
        in_specs=[pl.BlockSpec((8, tm), lambda i: (0, i), memory_space=pltpu.SMEM),
                  pl.BlockSpec((8, tm), lambda i: (0, jnp.minimum(i + 1, n_steps - 1)),
                               memory_space=pltpu.SMEM),
                  tok(d), tok(LANES), tok(p2.shape[1]), const((1, d)), const(wpp.shape),
                  const(wpg.shape), const((1, d)),
                  pl.BlockSpec(memory_space=pl.ANY)],
        out_specs=tok(d),
        out_shape=jax.ShapeDtypeStruct((n, d), F32),
        scratch_shapes=[pltpu.VMEM((TOP_K, tm * ROW_CHUNKS, LANES), F32),
                        pltpu.VMEM((TOP_K, tm * ROW_CHUNKS, LANES), F32), pltpu.SemaphoreType.DMA((2,))],
        compiler_params=pltpu.CompilerParams(
            dimension_semantics=("arbitrary",), vmem_limit_bytes=VMEM_LIMIT),
        name="combine",
    )(pos, pos, h1, gates, p2, pnw, wpp, wpg, fnw, y)


def kernel(x, p, attn_norm_w, w_in, sb_norm_w, hg_lower_bounds, hg_norm_w, w_out, ffn_norm_w,
           w_group_router, b_group_router, w_expert_router, b_expert_router, w_exp_gate, w_exp_up,
           w_exp_down, ple_norm_w, w_ple_proj, w_ple_gate, final_norm_w):
    b, s, d = x.shape
    depth = w_in.shape[0]
    assert depth == 1, "single-layer trunk"
    sbw = sb_norm_w.shape[1]
    hgw = hg_norm_w.shape[1]
    n = b * s
    tm_proj = min(TOKEN_TILE, s)
    tm_route = min(TOKEN_TILE, n)
    tm_disp = min(TOKEN_TILE, n)
    tm_comb = min(COMBINE_TILE, n)
    tmm = min(EXPERT_TILE, n)

    wi = w_in[0]
    w_main = jnp.concatenate([wi[:, 0:sbw], wi[:, 2 * sbw:]], axis=1).astype(BF16)
    w_kt = wi[:, sbw:2 * sbw].T.astype(BF16)
    gap = ROUTER_ROW0 - N_GROUPS
    tail = ROUTER_ROWS - ROUTER_ROW0 - N_EXPERTS
    wr_t = jnp.concatenate([w_group_router[0].T, jnp.zeros((gap, d), F32), w_expert_router[0].T,
                            jnp.zeros((tail, d), F32)], axis=0)
    br = jnp.concatenate([b_group_router[0], jnp.zeros((gap,), F32), b_expert_router[0],
                          jnp.zeros((tail,), F32)])[:, None]

    sbq, sbkt, sbve, sbvo, hq, hgl, hk, hv, hgate = _in_proj(
        x, attn_norm_w[0][None, :], w_main, w_kt, hg_lower_bounds, tm=tm_proj)
    sbo = _sb_attn(sbq, sbkt, sbve, sbvo, sb_norm_w, chains=SB_CHAINS, qb=SB_QUERY_BLOCKS)
    hgo = _hgrn2(hq, hgl, hk, hv, hgate, hg_norm_w, chunks_per_iter=HG_CHUNKS_PER_ITER)

    h1, m, meta, gates, counts = _out_route(
        x.reshape(n, d), sbo.reshape(n, sbw), hgo.reshape(n, hgw), w_out[0].astype(BF16),
        ffn_norm_w[0][None, :], wr_t, br, tm=tm_route)

    cnt = counts[ROUTER_ROW0:ROUTER_ROW0 + N_EXPERTS, 0].astype(jnp.int32)
    padded = ((cnt + tmm - 1) // tmm) * tmm
    ends = jnp.cumsum(padded)
    starts = ends - padded
    n_rows = n * TOP_K + N_EXPERTS * tmm
    n_tiles = n_rows // tmm
    last_tile = ends[-1] // tmm - 1
    tile_src = jnp.minimum(jnp.arange(n_tiles, dtype=jnp.int32), last_tile)
    tile_expert = jnp.sum((ends[None, :] <= (tile_src * tmm)[:, None]).astype(jnp.int32), axis=1)
    seg = jnp.concatenate([starts, ends, padded, (last_tile + 1)[None],
                           jnp.zeros((LANES - 3 * N_EXPERTS - 1,), jnp.int32)])

    pos = _positions(seg, meta)
    xs = _dispatch(seg, pos, m, n_rows, tm=tm_disp, tmm=tmm)
    y = _experts(tile_expert, tile_src, xs, w_exp_gate[0], w_exp_up[0], w_exp_down[0], tmm=tmm)
    out = _combine(pos, h1, gates, p[0].reshape(n, -1), ple_norm_w[0][None, :],
                   w_ple_proj[0].astype(BF16), w_ple_gate[0].astype(BF16), final_norm_w[None, :],
                   y, tm=tm_comb)
    return out.reshape(b, s, d)
```

```python
import functools

import numpy as np
import jax
import jax.numpy as jnp
from jax import lax
from jax.experimental import pallas as pl
from jax.experimental.pallas import tpu as pltpu

F32 = jnp.float32
BF16 = jnp.bfloat16
EPS = 1e-6

SB_HEAD_DIM = 64
HG_HEAD_DIM = 128
HG_CHUNK = 64
HG_CHUNKS_PER_ITER = 8
N_GROUPS = 4
EXPERTS_PER_GROUP = 8
N_EXPERTS = N_GROUPS * EXPERTS_PER_GROUP
TOP_K = 2
LANES = 128
ROW_CHUNKS = 8
ROUTER_ROW0 = 8
ROUTER_ROWS = 48
SB_BLOCK = 128
SB_QUERY_BLOCKS = 1
SB_CHAINS = 16
SB_LOG2_FLOOR = -152.0
LOG2E = 1.4426950408889634
TOKEN_TILE = 512
ROUTE_TILE = 1024
COMBINE_TILE = 256
EXPERT_TILE = 512
VMEM_LIMIT = 56 * 1024 * 1024

_NT = (((1,), (1,)), ((), ()))
_TN = (((0,), (0,)), ((), ()))


def _sigmoid(x):
    return 1.0 / (1.0 + jnp.exp(-x))


def _store_row_tiles(ref, x):
    rows = x.shape[0]
    for c in range(ROW_CHUNKS):
        ref[pl.ds(c, rows, stride=ROW_CHUNKS), :] = x[:, c * LANES:(c + 1) * LANES]


def _load_row_tiles(ref, rows):
    return jnp.concatenate([ref[pl.ds(c, rows, stride=ROW_CHUNKS), :] for c in range(ROW_CHUNKS)], axis=1)


def _split2(x):
    hi = x.astype(BF16)
    lo = (x - hi.astype(F32)).astype(BF16)
    return hi, lo


def _in_proj_kernel(x_ref, nw_ref, w_ref, wkt_ref, lbp_ref,
                    sbq_ref, sbkt_ref, sbve_ref, sbvo_ref, hq_ref, hgl_ref, hk_ref, hv_ref, hgate_ref,
                    *, sbw, hgw):
    x = x_ref[0]
    ms = jnp.mean(x * x, axis=-1, keepdims=True)
    a = (x * lax.rsqrt(ms + EPS) * nw_ref[...]).astype(BF16)

    def seg(lo, width):
        return jnp.dot(a, w_ref[:, lo:lo + width], preferred_element_type=F32)

    sbq_ref[0] = (seg(0, sbw) * (SB_HEAD_DIM ** -0.5 * LOG2E)).astype(BF16)
    sbkt_ref[0] = lax.dot_general(wkt_ref[...], a, _NT, preferred_element_type=F32).astype(BF16)
    v = seg(sbw, sbw)
    even_head = (lax.broadcasted_iota(jnp.int32, v.shape, 1) & SB_HEAD_DIM) == 0
    sbve_ref[0] = jnp.where(even_head, v, 0.0).astype(BF16)
    sbvo_ref[0] = jnp.where(even_head, 0.0, v).astype(BF16)
    c = 2 * sbw
    q = seg(c, hgw)
    hq_ref[0] = q * _sigmoid(q)
    p0 = lbp_ref[0:1, :]
    p1 = lbp_ref[1:2, :]
    pm = jnp.maximum(p0, p1)
    e0 = jnp.exp(p0 - pm)
    e1 = jnp.exp(p1 - pm)
    lb = e0 / (e0 + e1)
    fz = seg(c + hgw, hgw)
    sg = _sigmoid(fz)
    f = lb + (1.0 - lb) * sg
    hgl_ref[0] = jnp.log(f)
    hk_ref[0] = 1.0 - f
    hv_ref[0] = seg(c + 2 * hgw, hgw).astype(BF16)
    g = seg(c + 3 * hgw, hgw)
    hgate_ref[0] = g * _sigmoid(g)


def _in_proj(x, nw, w_main, w_kt, lbp, *, tm):
    b, s, d = x.shape
    sbw = w_kt.shape[0]
    hgw = (w_main.shape[1] - 2 * sbw) // 4
    tok = lambda width: pl.BlockSpec((1, tm, width), lambda bi, i: (bi, i, 0))
    const = lambda shape: pl.BlockSpec(shape, lambda bi, i: (0,) * len(shape))
    out_shape = [
        jax.ShapeDtypeStruct((b, s, sbw), BF16),
        jax.ShapeDtypeStruct((b, sbw, s), BF16),
        jax.ShapeDtypeStruct((b, s, sbw), BF16),
        jax.ShapeDtypeStruct((b, s, sbw), BF16),
        jax.ShapeDtypeStruct((b, s, hgw), F32),
        jax.ShapeDtypeStruct((b, s, hgw), F32),
        jax.ShapeDtypeStruct((b, s, hgw), F32),
        jax.ShapeDtypeStruct((b, s, hgw), BF16),
        jax.ShapeDtypeStruct((b, s, hgw), F32),
    ]
    out_specs = [tok(sbw), pl.BlockSpec((1, sbw, tm), lambda bi, i: (bi, 0, i)), tok(sbw), tok(sbw),
                 tok(hgw), tok(hgw), tok(hgw), tok(hgw), tok(hgw)]
    return pl.pallas_call(
        functools.partial(_in_proj_kernel, sbw=sbw, hgw=hgw),
        grid=(b, s // tm),
        in_specs=[tok(d), const((1, d)), const(w_main.shape), const(w_kt.shape), const(lbp.shape)],
        out_specs=out_specs,
        out_shape=out_shape,
        compiler_params=pltpu.CompilerParams(
            dimension_semantics=("arbitrary", "arbitrary"), vmem_limit_bytes=VMEM_LIMIT),
        name="in_proj",
    )(x, nw, w_main, w_kt, lbp)


def _sb_cumsum_weights():
    j = np.arange(SB_BLOCK)[:, None]
    s = np.arange(SB_BLOCK)[None, :]
    half = np.concatenate([(j > s).astype(np.float32), np.ones((SB_BLOCK, SB_BLOCK), np.float32)], axis=1)
    return jnp.asarray(np.concatenate([half, half], axis=0), dtype=BF16)


def _sb_attn_kernel(q_ref, kt_ref, ve_ref, vo_ref, nw_ref, cw_ref, o_ref, *, n_groups, chains, qb):
    assert qb == 1, "the corner step assumes one query block per chain"
    blk = SB_BLOCK
    dh = SB_HEAD_DIM
    rq = qb * blk
    rr = 2 * rq
    first = lax.broadcasted_iota(jnp.int32, (rq, LANES), 1) < dh
    rowpos = lax.broadcasted_iota(jnp.int32, (rr, blk), 0) & (rq - 1)
    colpos = lax.broadcasted_iota(jnp.int32, (rr, blk), 1)
    strictly_before = colpos < rowpos
    cw = cw_ref[...]

    def neg_abs(z):
        bits = lax.bitcast_convert_type(z, jnp.uint32) | jnp.uint32(0x80000000)
        return lax.bitcast_convert_type(bits, F32)

    ch = blk // 2
    corner_region = (rowpos < ch) & (colpos >= blk - ch)
    corner_keys = lax.broadcasted_iota(jnp.int32, (2 * ch, blk), 1) >= blk - ch

    def corner_rows(x):
        return jnp.concatenate([x[0:ch], x[rq:rq + ch]], axis=0)

    def corner(qss, js, v_scales, carries, accs):
        k0s = [pl.multiple_of(j * blk, blk) for j in js]
        zs = [jnp.dot(corner_rows(qs), kt_ref[0, :, pl.ds(k0, blk)], preferred_element_type=F32)
              for qs, k0 in zip(qss, k0s)]
        lss, hls = [], []
        for z in zs:
            sp = jnp.log2(1.0 + jnp.exp2(neg_abs(z)))
            ls = jnp.minimum(z, 0.0) - sp
            hi, lo = _split2(jnp.where(corner_keys, ls - z, 0.0))
            lss.append(ls)
            hls.append(jnp.concatenate([hi, lo], axis=1))
        css = [jnp.dot(hl, cw, preferred_element_type=F32) for hl in hls]
        new_carries, new_accs = [], []
        for ls, cs, carry, acc, k0, v_scale in zip(lss, css, carries, accs, k0s, v_scales):
            a = jnp.where(corner_keys, jnp.exp2(ls + cs[:, :blk] + corner_rows(carry)), 0.0).astype(BF16)
            vst = jnp.concatenate([ve_ref[0, pl.ds(k0, blk), :], vo_ref[0, pl.ds(k0, blk), :]], axis=0) * v_scale
            delta = jnp.dot(jnp.concatenate([a[:ch], a[ch:]], axis=1), vst, preferred_element_type=F32)
            new_accs.append(jnp.concatenate([acc[:ch] + delta, acc[ch:]], axis=0))
            tot = cs[:, blk:]
            new_carries.append(jnp.concatenate(
                [carry[0:ch] + tot[:ch], carry[ch:rq], carry[rq:rq + ch] + tot[ch:], carry[rq + ch:]], axis=0))
        return new_carries, new_accs

    def steps(qss, js, v_scales, carries, accs, masked, skip=None):
        k0s = [pl.multiple_of(j * blk, blk) for j in js]
        zs = [jnp.dot(qs, kt_ref[0, :, pl.ds(k0, blk)], preferred_element_type=F32)
              for qs, k0 in zip(qss, k0s)]
        lss, hls, befores = [], [], []
        for z in zs:
            sp = jnp.log2(1.0 + jnp.exp2(neg_abs(z)))
            ls = jnp.minimum(z, 0.0) - sp
            lk = ls - z
            before = None
            if masked:
                before = strictly_before
                lk = jnp.where(before, lk, 0.0)
            if skip is not None:
                lk = jnp.where(skip, 0.0, lk)
            hi, lo = _split2(lk)
            lss.append(ls)
            hls.append(jnp.concatenate([hi, lo], axis=1))
            befores.append(before)
        css = [jnp.dot(hl, cw, preferred_element_type=F32) for hl in hls]
        abs_ = []
        for ls, cs, carry, before in zip(lss, css, carries, befores):
            a = jnp.exp2(ls + cs[:, :blk] + carry)
            if masked:
                a = jnp.where(before, a, 0.0)
            if skip is not None:
                a = jnp.where(skip, 0.0, a)
            ab = a.astype(BF16)
            abs_.append(jnp.concatenate([ab[:rq], ab[rq:]], axis=1))
        new_accs = []
        for ab, k0, v_scale, acc in zip(abs_, k0s, v_scales, accs):
            vst = jnp.concatenate([ve_ref[0, pl.ds(k0, blk), :], vo_ref[0, pl.ds(k0, blk), :]], axis=0)
            if v_scale is not None:
                vst = vst * v_scale
            new_accs.append(acc + jnp.dot(ab, vst, preferred_element_type=F32))
        new_carries = [carry + cs[:, blk:] for carry, cs in zip(carries, css)]
        return new_carries, new_accs

    def highest(carries):
        m = carries[0]
        for c in carries[1:]:
            m = jnp.maximum(m, c)
        return jnp.max(m)

    def group(gi, _):
        sis = [gi * chains + c for c in range(chains)]
        q0s = [pl.multiple_of(si * rq, rq) for si in sis]
        qss = []
        for q0 in q0s:
            q2 = q_ref[0, pl.ds(q0, rq), :].astype(F32)
            qss.append(jnp.concatenate([jnp.where(first, q2, 0.0), jnp.where(first, 0.0, q2)],
                                       axis=0).astype(BF16))
        carries = [jnp.zeros((rr, blk), F32)] * chains
        accs = [jnp.zeros((rq, LANES), F32)] * chains
        for i in range(qb):
            carries, accs = steps(qss, [si * qb + (qb - 1 - i) for si in sis], [None] * chains,
                                  carries, accs, True)

        def cond(st):
            n, top = st[0], st[1]
            return (n <= sis[-1] * qb) & (top > SB_LOG2_FLOOR)

        def key_blocks(n, last_chain_in_range):
            js, v_scales = [], []
            for c, si in enumerate(sis):
                d = si * qb - n
                if c < chains - 1 or not last_chain_in_range:
                    v_scales.append(jnp.where(d >= 0, 1.0, 0.0).astype(BF16))
                    d = jnp.maximum(d, 0)
                else:
                    v_scales.append(None)
                js.append(d)
            return js, v_scales

        def body(st):
            n = st[0]
            js, v_scales = key_blocks(n, True)
            skip = corner_region & (n == 2)
            cs_out, as_out = steps(qss, js, v_scales, list(st[2]), list(st[3]), False, skip)
            return n + 1, highest(cs_out), tuple(cs_out), tuple(as_out)

        js, v_scales = key_blocks(jnp.int32(1), False)
        carries, accs = steps(qss, js, v_scales, carries, accs, False)
        js, v_scales = key_blocks(jnp.int32(2), False)
        carries, accs = corner(qss, js, v_scales, carries, accs)
        st = lax.while_loop(cond, body, (jnp.int32(2), highest(carries), tuple(carries), tuple(accs)))
        for c, si in enumerate(sis):
            acc = st[3][c]
            q0 = pl.multiple_of(si * rq, rq)
            sq = acc * acc
            s0 = jnp.sum(jnp.where(first, sq, 0.0), axis=-1, keepdims=True)
            s1 = jnp.sum(jnp.where(first, 0.0, sq), axis=-1, keepdims=True)
            ms = jnp.where(first, s0, s1) * (1.0 / dh)
            o_ref[0, pl.ds(q0, rq), :] = (acc * lax.rsqrt(ms + EPS) * nw_ref[...]).astype(o_ref.dtype)
        return 0

    lax.fori_loop(0, n_groups, group, 0)


def _sb_attn(q, kt, v_even, v_odd, nw, *, chains, qb):
    b, s, w = q.shape
    assert s % (chains * qb * SB_BLOCK) == 0 and w % LANES == 0
    n_pairs = w // LANES
    cw = _sb_cumsum_weights()
    tok = pl.BlockSpec((1, s, LANES), lambda bi, hp: (bi, 0, hp))
    return pl.pallas_call(
        functools.partial(_sb_attn_kernel, n_groups=s // (chains * qb * SB_BLOCK), chains=chains, qb=qb),
        grid=(b, n_pairs),
        in_specs=[
            tok,
            pl.BlockSpec((1, LANES, s), lambda bi, hp: (bi, hp, 0)),
            tok, tok,
            pl.BlockSpec((1, LANES), lambda bi, hp: (0, hp)),
            pl.BlockSpec(cw.shape, lambda bi, hp: (0, 0)),
        ],
        out_specs=tok,
        out_shape=jax.ShapeDtypeStruct((b, s, w), BF16),
        compiler_params=pltpu.CompilerParams(
            dimension_semantics=("arbitrary", "arbitrary"), vmem_limit_bytes=VMEM_LIMIT),
        name="sb_attn",
    )(q, kt, v_even, v_odd, nw, cw)


def _hgrn_levels():
    hs = []
    h = HG_CHUNK // 2
    while h >= 1:
        hs.append(h)
        h //= 2
    return hs


def _hgrn_consts():
    n = HG_CHUNK
    t = np.arange(n)[:, None]
    j = np.arange(n)[None, :]
    mats = [j <= t]
    masks = []
    for h in _hgrn_levels():
        mid = (t // (2 * h)) * (2 * h) + h - 1
        upper = (t % (2 * h)) >= h
        if h > 1:
            mats.append(np.where(upper, (j > mid) & (j <= t), (j > t) & (j <= mid)))
        masks.append(((t // (2 * h)) == (j // (2 * h))) & upper & ((j % (2 * h)) < h))
    masks.append(t == j)
    sums = np.concatenate(mats, axis=0).astype(np.float32)
    sums = np.concatenate([sums, sums], axis=1)
    pm = np.concatenate(masks, axis=0).astype(np.float32)
    return jnp.asarray(sums, dtype=BF16), jnp.asarray(pm, dtype=F32)


def _hgrn2_kernel(q_ref, gl_ref, k_ref, v_ref, gate_ref, nw_ref, cs_ref, pm_ref, o_ref,
                  *, n_iters, chunks_per_iter):
    n = HG_CHUNK
    dk = HG_HEAD_DIM
    heads = 2
    levels = _hgrn_levels()
    rows = lax.broadcasted_iota(jnp.int32, (n, heads * dk), 0)
    odd = (rows & 1) != 0

    def body(it, states):
        base = it * (chunks_per_iter * n)
        cs = cs_ref[...]
        half = cs.shape[0] // 2
        units = []
        for c in range(chunks_per_iter):
            sl = pl.ds(pl.multiple_of(base + c * n, n), n)
            g = gl_ref[0, sl, :]
            g_parts = jnp.concatenate(_split2(g), axis=0)
            d = jnp.concatenate(
                [jnp.dot(cs[:half], g_parts, preferred_element_type=F32),
                 jnp.dot(cs[half:], g_parts, preferred_element_type=F32)], axis=0)
            units.append(dict(sl=sl, g=g, d=d, q=q_ref[0, sl, :], k=k_ref[0, sl, :], v=v_ref[0, sl, :]))
        for un in units:
            q, k, g, d = un["q"], un["k"], un["g"], un["d"]
            bc = d[0:n]
            b_last = bc[n - 1:n, :]
            ws = []
            for li, h in enumerate(levels):
                dl = d[(1 + li) * n:(2 + li) * n] if h > 1 else jnp.where(odd, g, 0.0)
                upper = (rows & h) != 0
                ws.append((jnp.where(upper, q, k) * jnp.exp(dl)).astype(BF16))
            un["ws"] = ws
            un["qk"] = q * k
            un["qd"] = (q * jnp.exp(bc)).astype(BF16)
            un["kd"] = (k * jnp.exp(b_last - bc)).astype(BF16)
            un["decay"] = jnp.exp(b_last)
        hslices = [slice(hh * dk, (hh + 1) * dk) for hh in range(heads)]
        for un in units:
            un["gram"] = [[lax.dot_general(w[:, ls], w[:, ls], _NT, preferred_element_type=F32)
                           for w in un["ws"]] for ls in hslices]
            un["kv"] = [lax.dot_general(un["v"][:, ls], un["kd"][:, ls], _TN, preferred_element_type=F32)
                        for ls in hslices]
        for un in units:
            un["st"] = states
            states = tuple(states[hh] * un["decay"][:, ls] + un["kv"][hh] for hh, ls in enumerate(hslices))
            ps = []
            for hh, ls in enumerate(hslices):
                p = pm_ref[len(levels) * n:(len(levels) + 1) * n, :] * jnp.sum(un["qk"][:, ls], axis=1, keepdims=True)
                for li in range(len(levels)):
                    p = p + pm_ref[li * n:(li + 1) * n, :] * un["gram"][hh][li]
                ps.append(p.astype(BF16))
            un["p"] = ps
        for un in units:
            un["o"] = [jnp.dot(un["p"][hh], un["v"][:, ls], preferred_element_type=F32)
                       + lax.dot_general(un["qd"][:, ls], un["st"][hh].astype(BF16), _NT,
                                         preferred_element_type=F32)
                       for hh, ls in enumerate(hslices)]
        for un in units:
            outs = []
            for o in un["o"]:
                ms = jnp.mean(o * o, axis=-1, keepdims=True)
                outs.append(o * lax.rsqrt(ms + EPS))
            o2 = jnp.concatenate(outs, axis=1)
            o_ref[0, un["sl"], :] = (o2 * nw_ref[...] * gate_ref[0, un["sl"], :]).astype(o_ref.dtype)
        return states

    zero = jnp.zeros((dk, dk), F32)
    lax.fori_loop(0, n_iters, body, (zero, zero))


def _hgrn2(q, gl, k, v, gate, nw, *, chunks_per_iter):
    b, s, w = q.shape
    assert s % (HG_CHUNK * chunks_per_iter) == 0 and w % (2 * HG_HEAD_DIM) == 0
    pair = 2 * HG_HEAD_DIM
    heads = w // pair
    cs, pm = _hgrn_consts()
    tok = pl.BlockSpec((1, s, pair), lambda bi, h: (bi, 0, h))
    return pl.pallas_call(
        functools.partial(_hgrn2_kernel, n_iters=s // (HG_CHUNK * chunks_per_iter),
                          chunks_per_iter=chunks_per_iter),
        grid=(b, heads),
        in_specs=[tok, tok, tok, tok, tok,
                  pl.BlockSpec((1, pair), lambda bi, h: (0, h)),
                  pl.BlockSpec(cs.shape, lambda bi, h: (0, 0)),
                  pl.BlockSpec(pm.shape, lambda bi, h: (0, 0))],
        out_specs=tok,
        out_shape=jax.ShapeDtypeStruct((b, s, w), BF16),
        compiler_params=pltpu.CompilerParams(
            dimension_semantics=("arbitrary", "arbitrary"), vmem_limit_bytes=VMEM_LIMIT),
        name="hgrn2",
    )(q, gl, k, v, gate, nw, cs, pm)


def _out_route_kernel(x_ref, sbo_ref, hgo_ref, wo_ref, nw_ref, wr_ref, br_ref, su_ref,
                      h1_ref, m_ref, meta_ref, gates_ref, counts_ref, carry_ref, *, sbw):
    i = pl.program_id(0)

    @pl.when(i == 0)
    def _():
        carry_ref[...] = jnp.zeros_like(carry_ref)

    h1 = (x_ref[...]
          + jnp.dot(sbo_ref[...], wo_ref[0:sbw, :], preferred_element_type=F32)
          + jnp.dot(hgo_ref[...], wo_ref[sbw:, :], preferred_element_type=F32))
    h1_ref[...] = h1
    ms = jnp.mean(h1 * h1, axis=-1, keepdims=True)
    m = h1 * lax.rsqrt(ms + EPS) * nw_ref[...]
    _store_row_tiles(m_ref, m)

    m2 = _split2(m)
    w2 = _split2(wr_ref[...])
    logits = br_ref[...]
    for wi, mi in ((0, 0), (0, 1), (1, 0)):
        logits = logits + lax.dot_general(w2[wi], m2[mi], _NT, preferred_element_type=F32)
    rid = lax.broadcasted_iota(jnp.int32, logits.shape, 0)
    neg = -jnp.inf
    big = jnp.int32(2 * LANES)

    def first_argmax(vals):
        vmax = jnp.max(vals, axis=0, keepdims=True)
        idx = jnp.min(jnp.where(vals == vmax, rid, big), axis=0, keepdims=True)
        return vmax, idx

    is_group = rid < N_GROUPS
    gmax, g_idx = first_argmax(jnp.where(is_group, logits, neg))
    gsum = jnp.sum(jnp.where(is_group, jnp.exp(logits - gmax), 0.0), axis=0, keepdims=True)
    g_prob = 1.0 / gsum
    lo_row = ROUTER_ROW0 + EXPERTS_PER_GROUP * g_idx
    el = jnp.where((rid >= lo_row) & (rid < lo_row + EXPERTS_PER_GROUP), logits, neg)
    v1, i1 = first_argmax(el)
    v2, i2 = first_argmax(jnp.where(rid == i1, neg, el))
    dd = jnp.exp(v2 - v1)
    p1 = 1.0 / (1.0 + dd)
    g1 = p1 * g_prob
    g2 = dd * p1 * g_prob

    hit1 = rid == i1
    hit2 = rid == i2
    onehot = jnp.where(hit1 | hit2, 1.0, 0.0)
    before_cnt = carry_ref[...] + jnp.dot(onehot.astype(BF16), su_ref[...], preferred_element_type=F32)
    r1 = jnp.sum(jnp.where(hit1, before_cnt, 0.0), axis=0, keepdims=True)
    r2 = jnp.sum(jnp.where(hit2, before_cnt, 0.0), axis=0, keepdims=True)
    carry_ref[...] = carry_ref[...] + jnp.sum(onehot, axis=1, keepdims=True)
    counts_ref[...] = carry_ref[...]

    meta_ref[...] = jnp.zeros_like(meta_ref)
    meta_ref[0:1, :] = i1 - ROUTER_ROW0
    meta_ref[1:2, :] = i2 - ROUTER_ROW0
    meta_ref[2:3, :] = r1.astype(jnp.int32)
    meta_ref[3:4, :] = r2.astype(jnp.int32)
    lane_rows = lax.broadcasted_iota(jnp.int32, (LANES, g1.shape[1]), 0)
    gates_ref[...] = jnp.where(lane_rows == 0, g1, jnp.where(lane_rows == 1, g2, 0.0)).T


def _out_route(x2, sbo, hgo, w_out, nw, wr_t, br, *, tm):
    n, d = x2.shape
    assert d == ROW_CHUNKS * LANES
    sbw = sbo.shape[1]
    j = np.arange(tm)[:, None]
    t = np.arange(tm)[None, :]
    su = jnp.asarray((j < t).astype(np.float32), dtype=BF16)
    tok = lambda width: pl.BlockSpec((tm, width), lambda i: (i, 0))
    const = lambda shape: pl.BlockSpec(shape, lambda i: (0,) * len(shape))
    return pl.pallas_call(
        functools.partial(_out_route_kernel, sbw=sbw),
        grid=(n // tm,),
        in_specs=[tok(d), tok(sbw), tok(hgo.shape[1]), const(w_out.shape), const((1, d)),
                  const(wr_t.shape), const(br.shape), const(su.shape)],
        out_specs=[tok(d), pl.BlockSpec((tm * ROW_CHUNKS, LANES), lambda i: (i, 0)),
                   pl.BlockSpec((8, tm), lambda i: (0, i)), tok(LANES), const((ROUTER_ROWS, 1))],
        out_shape=[jax.ShapeDtypeStruct((n, d), F32), jax.ShapeDtypeStruct((n * ROW_CHUNKS, LANES), F32),
                   jax.ShapeDtypeStruct((8, n), jnp.int32), jax.ShapeDtypeStruct((n, LANES), F32),
                   jax.ShapeDtypeStruct((ROUTER_ROWS, 1), F32)],
        scratch_shapes=[pltpu.VMEM((ROUTER_ROWS, 1), F32)],
        compiler_params=pltpu.CompilerParams(
            dimension_semantics=("arbitrary",), vmem_limit_bytes=VMEM_LIMIT),
        name="out_route",
    )(x2, sbo, hgo, w_out, nw, wr_t, br, su)


def _positions_kernel(seg_ref, meta_ref, pos_ref):
    e = meta_ref[0:TOP_K, :]
    start = jnp.zeros(e.shape, jnp.int32)
    for x in range(N_EXPERTS):
        start = jnp.where(e == x, seg_ref[x], start)
    pos_ref[...] = jnp.zeros_like(pos_ref)
    pos_ref[0:TOP_K, :] = start + meta_ref[TOP_K:2 * TOP_K, :]


def _positions(seg, meta):
    return pl.pallas_call(
        _positions_kernel,
        in_specs=[pl.BlockSpec(memory_space=pltpu.SMEM), pl.BlockSpec(memory_space=pltpu.VMEM)],
        out_specs=pl.BlockSpec(memory_space=pltpu.VMEM),
        out_shape=jax.ShapeDtypeStruct(meta.shape, jnp.int32),
        compiler_params=pltpu.CompilerParams(vmem_limit_bytes=VMEM_LIMIT),
        name="positions",
    )(seg, meta)


def _dispatch_kernel(seg_ref, pos_ref, m_ref, xs_ref, zbuf, sem, zsem, *, tm, tmm, n_tiles):
    i = pl.program_id(0)

    rc = ROW_CHUNKS

    def zero_copy(e):
        tail = pl.multiple_of((seg_ref[N_EXPERTS + e] - tmm) * rc, tmm * rc)
        return pltpu.make_async_copy(zbuf, xs_ref.at[pl.ds(tail, tmm * rc), :], zsem)

    @pl.when(i == 0)
    def _():
        zbuf[...] = jnp.zeros_like(zbuf)
        for e in range(N_EXPERTS):
            @pl.when(seg_ref[2 * N_EXPERTS + e] > 0)
            def _():
                zero_copy(e).start()
        for e in range(N_EXPERTS):
            @pl.when(seg_ref[2 * N_EXPERTS + e] > 0)
            def _():
                zero_copy(e).wait()

        def unused_copy(t):
            first = pl.multiple_of(t * (tmm * rc), tmm * rc)
            return pltpu.make_async_copy(zbuf, xs_ref.at[pl.ds(first, tmm * rc), :], zsem)

        def start_unused(t, _):
            unused_copy(t).start()
            return 0

        def wait_unused(t, _):
            unused_copy(t).wait()
            return 0

        lax.fori_loop(seg_ref[3 * N_EXPERTS], n_tiles, start_unused, 0)
        lax.fori_loop(seg_ref[3 * N_EXPERTS], n_tiles, wait_unused, 0)

    for r in range(tm):
        src = m_ref.at[pl.ds(r * rc, rc), :]
        for k in range(TOP_K):
            dst = pl.multiple_of(pos_ref[k, r] * rc, rc)
            pltpu.make_async_copy(src, xs_ref.at[pl.ds(dst, rc), :], sem).start(priority=k)
    for _ in range(TOP_K):
        pltpu.make_async_copy(m_ref, xs_ref.at[pl.ds(0, tm * rc), :], sem).wait()


def _dispatch(seg, pos, m, n_rows, *, tm, tmm):
    n = m.shape[0] // ROW_CHUNKS
    return pl.pallas_call(
        functools.partial(_dispatch_kernel, tm=tm, tmm=tmm, n_tiles=n_rows // tmm),
        grid=(n // tm,),
        in_specs=[pl.BlockSpec(memory_space=pltpu.SMEM),
                  pl.BlockSpec((8, tm), lambda i: (0, i), memory_space=pltpu.SMEM),
                  pl.BlockSpec((tm * ROW_CHUNKS, LANES), lambda i: (i, 0))],
        out_specs=pl.BlockSpec(memory_space=pl.ANY),
        out_shape=jax.ShapeDtypeStruct((n_rows * ROW_CHUNKS, LANES), F32),
        scratch_shapes=[pltpu.VMEM((tmm * ROW_CHUNKS, LANES), F32), pltpu.SemaphoreType.DMA(()),
                        pltpu.SemaphoreType.DMA(())],
        compiler_params=pltpu.CompilerParams(
            dimension_semantics=("arbitrary",), vmem_limit_bytes=VMEM_LIMIT),
        name="dispatch",
    )(seg, pos, m)


def _experts_kernel(te_ref, ts_ref, first_ref, par_ref, nxt_ref, xs_ref, wg_hbm, wu_hbm, wd_hbm, y_ref,
                    wg_f, wu_f, wd_f, wg_b, wu_b, wd_b, sem, *, tmm):
    t = pl.program_id(0)

    def copies(e, slot):
        return (pltpu.make_async_copy(wg_hbm.at[e], wg_f.at[slot], sem.at[slot, 0]),
                pltpu.make_async_copy(wu_hbm.at[e], wu_f.at[slot], sem.at[slot, 1]),
                pltpu.make_async_copy(wd_hbm.at[e], wd_f.at[slot], sem.at[slot, 2]))

    @pl.when(t == 0)
    def _():
        for c in copies(te_ref[0], 0):
            c.start()

    for parity in range(2):
        @pl.when((first_ref[t] == 1) & (par_ref[t] == parity))
        def _():
            for c in copies(0, parity):
                c.wait()

            @pl.when(nxt_ref[t] >= 0)
            def _():
                for c in copies(nxt_ref[t], 1 - parity):
                    c.start()

            wg_b[...] = wg_f[parity].astype(BF16)
            wu_b[...] = wu_f[parity].astype(BF16)
            wd_b[...] = wd_f[parity].astype(BF16)

    @pl.when(ts_ref[t] == t)
    def _():
        x = _load_row_tiles(xs_ref, tmm).astype(BF16)
        hg = jnp.dot(x, wg_b[...], preferred_element_type=F32)
        hu = jnp.dot(x, wu_b[...], preferred_element_type=F32)
        act = (hg * _sigmoid(hg) * hu).astype(BF16)
        _store_row_tiles(y_ref, jnp.dot(act, wd_b[...], preferred_element_type=F32))

    @pl.when(ts_ref[t] != t)
    def _():
        y_ref[...] = jnp.zeros_like(y_ref)


def _experts(tile_expert, tile_src, tile_first, tile_parity, tile_next, xs, wg, wu, wd, *, tmm):
    n_rows = xs.shape[0] // ROW_CHUNKS
    d, de = wg.shape[1], wg.shape[2]
    assert d == ROW_CHUNKS * LANES
    grid_spec = pltpu.PrefetchScalarGridSpec(
        num_scalar_prefetch=5,
        grid=(n_rows // tmm,),
        in_specs=[pl.BlockSpec((tmm * ROW_CHUNKS, LANES), lambda t, te, ts, *_: (ts[t], 0)),
                  pl.BlockSpec(memory_space=pl.ANY),
                  pl.BlockSpec(memory_space=pl.ANY),
                  pl.BlockSpec(memory_space=pl.ANY)],
        out_specs=pl.BlockSpec((tmm * ROW_CHUNKS, LANES), lambda t, *_: (t, 0)),
        scratch_shapes=[pltpu.VMEM((2, d, de), F32), pltpu.VMEM((2, d, de), F32), pltpu.VMEM((2, de, d), F32),
                        pltpu.VMEM((d, de), BF16), pltpu.VMEM((d, de), BF16), pltpu.VMEM((de, d), BF16),
                        pltpu.SemaphoreType.DMA((2, 3))],
    )
    return pl.pallas_call(
        functools.partial(_experts_kernel, tmm=tmm),
        grid_spec=grid_spec,
        out_shape=jax.ShapeDtypeStruct((n_rows * ROW_CHUNKS, LANES), F32),
        compiler_params=pltpu.CompilerParams(
            dimension_semantics=("arbitrary",), vmem_limit_bytes=VMEM_LIMIT),
        name="experts",
    )(tile_expert, tile_src, tile_first, tile_parity, tile_next, xs, wg, wu, wd)


def _combine_kernel(pos_ref, posn_ref, h1_ref, gates_ref, p_ref, pnw_ref, wpp_ref, wpg_ref, fnw_ref,
                    y_ref, o_ref, ybuf_even, ybuf_odd, sem, *, tm, n_steps):
    i = pl.program_id(0)

    rc = ROW_CHUNKS
    ybufs = (ybuf_even, ybuf_odd)

    def gather(p_ref_, slot):
        for r in range(tm):
            for k in range(TOP_K):
                src = pl.multiple_of(p_ref_[k, r] * rc, rc)
                pltpu.make_async_copy(y_ref.at[pl.ds(src, rc), :],
                                      ybufs[slot].at[k, pl.ds(r * rc, rc), :], sem.at[slot]).start(priority=k)

    def wait(slot):
        for k in range(TOP_K):
            pltpu.make_async_copy(y_ref.at[pl.ds(0, tm * rc), :], ybufs[slot].at[k], sem.at[slot]).wait()

    @pl.when(i == 0)
    def _():
        gather(pos_ref, 0)

    def step(slot):
        wait(slot)
        gather(posn_ref, 1 - slot)
        gts = gates_ref[...]
        h2 = (h1_ref[...] + gts[:, 0:1] * _load_row_tiles(ybufs[slot].at[0], tm)
              + gts[:, 1:2] * _load_row_tiles(ybufs[slot].at[1], tm))
        e = jnp.dot(p_ref[...].astype(BF16), wpp_ref[...], preferred_element_type=F32)
        ms = jnp.mean(h2 * h2, axis=-1, keepdims=True)
        hn = (h2 * lax.rsqrt(ms + EPS) * pnw_ref[...]).astype(BF16)
        gate = _sigmoid(jnp.dot(hn, wpg_ref[...], preferred_element_type=F32))
        h3 = h2 + gate * e
        ms3 = jnp.mean(h3 * h3, axis=-1, keepdims=True)
        o_ref[...] = h3 * lax.rsqrt(ms3 + EPS) * fnw_ref[...]

        @pl.when(i == n_steps - 1)
        def _():
            wait(1 - slot)

    for parity in range(2):
        @pl.when(i % 2 == parity)
        def _():
            step(parity)


def _combine(pos, h1, gates, p2, pnw, wpp, wpg, fnw, y, *, tm):
    n, d = h1.shape
    n_steps = n // tm
    tok = lambda width: pl.BlockSpec((tm, width), lambda i: (i, 0))
    const = lambda shape: pl.BlockSpec(shape, lambda i: (0,) * len(shape))
    return pl.pallas_call(
        functools.partial(_combine_kernel, tm=tm, n_steps=n_steps),
        grid=(n_steps,),
        in_specs=[pl.BlockSpec((8, tm), lambda i: (0, i), memory_space=pltpu.SMEM),
                  pl.BlockSpec((8, tm), lambda i: (0, jnp.minimum(i + 1, n_steps - 1)),
                               memory_space=pltpu.SMEM),
                  tok(d), tok(LANES), tok(p2.shape[1]), const((1, d)), const(wpp.shape),
                  const(wpg.shape), const((1, d)),
                  pl.BlockSpec(memory_space=pl.ANY)],
        out_specs=tok(d),
        out_shape=jax.ShapeDtypeStruct((n, d), F32),
        scratch_shapes=[pltpu.VMEM((TOP_K, tm * ROW_CHUNKS, LANES), F32),
                        pltpu.VMEM((TOP_K, tm * ROW_CHUNKS, LANES), F32), pltpu.SemaphoreType.DMA((2,))],
        compiler_params=pltpu.CompilerParams(
            dimension_semantics=("arbitrary",), vmem_limit_bytes=VMEM_LIMIT),
        name="combine",
    )(pos, pos, h1, gates, p2, pnw, wpp, wpg, fnw, y)


def kernel(x, p, attn_norm_w, w_in, sb_norm_w, hg_lower_bounds, hg_norm_w, w_out, ffn_norm_w,
           w_group_router, b_group_router, w_expert_router, b_expert_router, w_exp_gate, w_exp_up,
           w_exp_down, ple_norm_w, w_ple_proj, w_ple_gate, final_norm_w):
    b, s, d = x.shape
    depth = w_in.shape[0]
    assert depth == 1, "single-layer trunk"
    sbw = sb_norm_w.shape[1]
    hgw = hg_norm_w.shape[1]
    n = b * s
    tm_proj = min(TOKEN_TILE, s)
    tm_route = min(ROUTE_TILE, n)
    tm_disp = min(TOKEN_TILE, n)
    tm_comb = min(COMBINE_TILE, n)
    tmm = min(EXPERT_TILE, n)

    wi = w_in[0]
    w_main = jnp.concatenate([wi[:, 0:sbw], wi[:, 2 * sbw:]], axis=1).astype(BF16)
    w_kt = wi[:, sbw:2 * sbw].T.astype(BF16)
    gap = ROUTER_ROW0 - N_GROUPS
    tail = ROUTER_ROWS - ROUTER_ROW0 - N_EXPERTS
    wr_t = jnp.concatenate([w_group_router[0].T, jnp.zeros((gap, d), F32), w_expert_router[0].T,
                            jnp.zeros((tail, d), F32)], axis=0)
    br = jnp.concatenate([b_group_router[0], jnp.zeros((gap,), F32), b_expert_router[0],
                          jnp.zeros((tail,), F32)])[:, None]

    sbq, sbkt, sbve, sbvo, hq, hgl, hk, hv, hgate = _in_proj(
        x, attn_norm_w[0][None, :], w_main, w_kt, hg_lower_bounds, tm=tm_proj)
    sbo = _sb_attn(sbq, sbkt, sbve, sbvo, sb_norm_w, chains=SB_CHAINS, qb=SB_QUERY_BLOCKS)
    hgo = _hgrn2(hq, hgl, hk, hv, hgate, hg_norm_w, chunks_per_iter=HG_CHUNKS_PER_ITER)

    h1, m, meta, gates, counts = _out_route(
        x.reshape(n, d), sbo.reshape(n, sbw), hgo.reshape(n, hgw), w_out[0].astype(BF16),
        ffn_norm_w[0][None, :], wr_t, br, tm=tm_route)

    cnt = counts[ROUTER_ROW0:ROUTER_ROW0 + N_EXPERTS, 0].astype(jnp.int32)
    padded = ((cnt + tmm - 1) // tmm) * tmm
    ends = jnp.cumsum(padded)
    starts = ends - padded
    n_rows = n * TOP_K + N_EXPERTS * tmm
    n_tiles = n_rows // tmm
    last_tile = ends[-1] // tmm - 1
    tile_src = jnp.minimum(jnp.arange(n_tiles, dtype=jnp.int32), last_tile)
    tile_expert = jnp.sum((ends[None, :] <= (tile_src * tmm)[:, None]).astype(jnp.int32), axis=1)
    seg = jnp.concatenate([starts, ends, padded, (last_tile + 1)[None],
                           jnp.zeros((LANES - 3 * N_EXPERTS - 1,), jnp.int32)])

    pos = _positions(seg, meta)
    xs = _dispatch(seg, pos, m, n_rows, tm=tm_disp, tmm=tmm)
    tiles = jnp.arange(n_tiles, dtype=jnp.int32)
    changed = jnp.concatenate([jnp.ones((1,), bool), tile_expert[1:] != tile_expert[:-1]])
    tile_first = ((tile_src == tiles) & changed).astype(jnp.int32)
    tile_parity = (jnp.cumsum(tile_first) - 1) % 2
    experts = jnp.arange(N_EXPERTS, dtype=jnp.int32)
    later = (experts[None, :] > experts[:, None]) & (padded[None, :] > 0)
    next_expert = jnp.min(jnp.where(later, experts[None, :], N_EXPERTS), axis=1)
    next_expert = jnp.where(next_expert == N_EXPERTS, -1, next_expert)
    tile_next = next_expert[tile_expert]
    y = _experts(tile_expert, tile_src, tile_first, tile_parity.astype(jnp.int32), tile_next.astype(jnp.int32),
                 xs, w_exp_gate[0], w_exp_up[0], w_exp_down[0], tmm=tmm)
    out = _combine(pos, h1, gates, p[0].reshape(n, -1), ple_norm_w[0][None, :],
                   w_ple_proj[0].astype(BF16), w_ple_gate[0].astype(BF16), final_norm_w[None, :],
                   y, tm=tm_comb)
    return out.reshape(b, s, d)
```

```python
import functools

import numpy as np
import jax
import jax.numpy as jnp
from jax import lax
from jax.experimental import pallas as pl
from jax.experimental.pallas import tpu as pltpu

F32 = jnp.float32
BF16 = jnp.bfloat16
EPS = 1e-6

SB_HEAD_DIM = 64
HG_HEAD_DIM = 128
HG_CHUNK = 64
HG_CHUNKS_PER_ITER = 8
N_GROUPS = 4
EXPERTS_PER_GROUP = 8
N_EXPERTS = N_GROUPS * EXPERTS_PER_GROUP
TOP_K = 2
LANES = 128
ROW_CHUNKS = 8
ROUTER_ROW0 = 8
ROUTER_ROWS = 48
SB_BLOCK = 128
SB_QUERY_BLOCKS = 1
SB_CHAINS = 16
SB_LOG2_FLOOR = -152.0
LOG2E = 1.4426950408889634
TOKEN_TILE = 512
ROUTE_TILE = 1024
COMBINE_TILE = 256
EXPERT_TILE = 512
VMEM_LIMIT = 56 * 1024 * 1024

_NT = (((1,), (1,)), ((), ()))
_TN = (((0,), (0,)), ((), ()))


def _sigmoid(x):
    return 1.0 / (1.0 + jnp.exp(-x))


def _store_row_tiles(ref, x):
    rows = x.shape[0]
    for c in range(ROW_CHUNKS):
        ref[pl.ds(c, rows, stride=ROW_CHUNKS), :] = x[:, c * LANES:(c + 1) * LANES]


def _load_row_tiles(ref, rows):
    return jnp.concatenate([ref[pl.ds(c, rows, stride=ROW_CHUNKS), :] for c in range(ROW_CHUNKS)], axis=1)


def _split2(x):
    hi = x.astype(BF16)
    lo = (x - hi.astype(F32)).astype(BF16)
    return hi, lo


def _in_proj_kernel(x_ref, nw_ref, w_ref, wkt_ref, lbp_ref,
                    sbq_ref, sbkt_ref, sbve_ref, sbvo_ref, hq_ref, hgl_ref, hk_ref, hv_ref, hgate_ref,
                    *, sbw, hgw):
    x = x_ref[0]
    ms = jnp.mean(x * x, axis=-1, keepdims=True)
    a = (x * lax.rsqrt(ms + EPS) * nw_ref[...]).astype(BF16)

    def seg(lo, width):
        return jnp.dot(a, w_ref[:, lo:lo + width], preferred_element_type=F32)

    sbq_ref[0] = (seg(0, sbw) * (SB_HEAD_DIM ** -0.5 * LOG2E)).astype(BF16)
    sbkt_ref[0] = lax.dot_general(wkt_ref[...], a, _NT, preferred_element_type=F32).astype(BF16)
    v = seg(sbw, sbw)
    even_head = (lax.broadcasted_iota(jnp.int32, v.shape, 1) & SB_HEAD_DIM) == 0
    sbve_ref[0] = jnp.where(even_head, v, 0.0).astype(BF16)
    sbvo_ref[0] = jnp.where(even_head, 0.0, v).astype(BF16)
    c = 2 * sbw
    q = seg(c, hgw)
    hq_ref[0] = q * _sigmoid(q)
    p0 = lbp_ref[0:1, :]
    p1 = lbp_ref[1:2, :]
    pm = jnp.maximum(p0, p1)
    e0 = jnp.exp(p0 - pm)
    e1 = jnp.exp(p1 - pm)
    lb = e0 / (e0 + e1)
    fz = seg(c + hgw, hgw)
    sg = _sigmoid(fz)
    f = lb + (1.0 - lb) * sg
    hgl_ref[0] = jnp.log(f)
    hk_ref[0] = 1.0 - f
    hv_ref[0] = seg(c + 2 * hgw, hgw).astype(BF16)
    g = seg(c + 3 * hgw, hgw)
    hgate_ref[0] = g * _sigmoid(g)


def _in_proj(x, nw, w_main, w_kt, lbp, *, tm):
    b, s, d = x.shape
    sbw = w_kt.shape[0]
    hgw = (w_main.shape[1] - 2 * sbw) // 4
    tok = lambda width: pl.BlockSpec((1, tm, width), lambda bi, i: (bi, i, 0))
    const = lambda shape: pl.BlockSpec(shape, lambda bi, i: (0,) * len(shape))
    out_shape = [
        jax.ShapeDtypeStruct((b, s, sbw), BF16),
        jax.ShapeDtypeStruct((b, sbw, s), BF16),
        jax.ShapeDtypeStruct((b, s, sbw), BF16),
        jax.ShapeDtypeStruct((b, s, sbw), BF16),
        jax.ShapeDtypeStruct((b, s, hgw), F32),
        jax.ShapeDtypeStruct((b, s, hgw), F32),
        jax.ShapeDtypeStruct((b, s, hgw), F32),
        jax.ShapeDtypeStruct((b, s, hgw), BF16),
        jax.ShapeDtypeStruct((b, s, hgw), F32),
    ]
    out_specs = [tok(sbw), pl.BlockSpec((1, sbw, tm), lambda bi, i: (bi, 0, i)), tok(sbw), tok(sbw),
                 tok(hgw), tok(hgw), tok(hgw), tok(hgw), tok(hgw)]
    return pl.pallas_call(
        functools.partial(_in_proj_kernel, sbw=sbw, hgw=hgw),
        grid=(b, s // tm),
        in_specs=[tok(d), const((1, d)), const(w_main.shape), const(w_kt.shape), const(lbp.shape)],
        out_specs=out_specs,
        out_shape=out_shape,
        compiler_params=pltpu.CompilerParams(
            dimension_semantics=("arbitrary", "arbitrary"), vmem_limit_bytes=VMEM_LIMIT),
        name="in_proj",
    )(x, nw, w_main, w_kt, lbp)


def _sb_cumsum_weights():
    j = np.arange(SB_BLOCK)[:, None]
    s = np.arange(SB_BLOCK)[None, :]
    half = np.concatenate([(j > s).astype(np.float32), np.ones((SB_BLOCK, SB_BLOCK), np.float32)], axis=1)
    return jnp.asarray(np.concatenate([half, half], axis=0), dtype=BF16)


def _sb_attn_kernel(q_ref, kt_ref, ve_ref, vo_ref, nw_ref, cw_ref, o_ref, *, n_groups, chains, qb):
    assert qb == 1, "the corner step assumes one query block per chain"
    blk = SB_BLOCK
    dh = SB_HEAD_DIM
    rq = qb * blk
    rr = 2 * rq
    first = lax.broadcasted_iota(jnp.int32, (rq, LANES), 1) < dh
    rowpos = lax.broadcasted_iota(jnp.int32, (rr, blk), 0) & (rq - 1)
    colpos = lax.broadcasted_iota(jnp.int32, (rr, blk), 1)
    strictly_before = colpos < rowpos
    cw = cw_ref[...]

    def neg_abs(z):
        bits = lax.bitcast_convert_type(z, jnp.uint32) | jnp.uint32(0x80000000)
        return lax.bitcast_convert_type(bits, F32)

    ch = blk // 2
    corner_region = (rowpos < ch) & (colpos >= blk - ch)
    corner_keys = lax.broadcasted_iota(jnp.int32, (2 * ch, blk), 1) >= blk - ch

    def corner_rows(x):
        return jnp.concatenate([x[0:ch], x[rq:rq + ch]], axis=0)

    def corner(qss, js, v_scales, carries, accs):
        k0s = [pl.multiple_of(j * blk, blk) for j in js]
        zs = [jnp.dot(corner_rows(qs), kt_ref[0, :, pl.ds(k0, blk)], preferred_element_type=F32)
              for qs, k0 in zip(qss, k0s)]
        lss, hls = [], []
        for z in zs:
            sp = jnp.log2(1.0 + jnp.exp2(neg_abs(z)))
            ls = jnp.minimum(z, 0.0) - sp
            hi, lo = _split2(jnp.where(corner_keys, ls - z, 0.0))
            lss.append(ls)
            hls.append(jnp.concatenate([hi, lo], axis=1))
        css = [jnp.dot(hl, cw, preferred_element_type=F32) for hl in hls]
        new_carries, new_accs = [], []
        for ls, cs, carry, acc, k0, v_scale in zip(lss, css, carries, accs, k0s, v_scales):
            a = jnp.where(corner_keys, jnp.exp2(ls + cs[:, :blk] + corner_rows(carry)), 0.0).astype(BF16)
            vst = jnp.concatenate([ve_ref[0, pl.ds(k0, blk), :], vo_ref[0, pl.ds(k0, blk), :]], axis=0) * v_scale
            delta = jnp.dot(jnp.concatenate([a[:ch], a[ch:]], axis=1), vst, preferred_element_type=F32)
            new_accs.append(jnp.concatenate([acc[:ch] + delta, acc[ch:]], axis=0))
            tot = cs[:, blk:]
            new_carries.append(jnp.concatenate(
                [carry[0:ch] + tot[:ch], carry[ch:rq], carry[rq:rq + ch] + tot[ch:], carry[rq + ch:]], axis=0))
        return new_carries, new_accs

    def steps(qss, js, v_scales, carries, accs, masked, skip=None):
        k0s = [pl.multiple_of(j * blk, blk) for j in js]
        zs = [jnp.dot(qs, kt_ref[0, :, pl.ds(k0, blk)], preferred_element_type=F32)
              for qs, k0 in zip(qss, k0s)]
        lss, hls, befores = [], [], []
        for z in zs:
            sp = jnp.log2(1.0 + jnp.exp2(neg_abs(z)))
            ls = jnp.minimum(z, 0.0) - sp
            lk = ls - z
            before = None
            if masked:
                before = strictly_before
                lk = jnp.where(before, lk, 0.0)
            if skip is not None:
                lk = jnp.where(skip, 0.0, lk)
            hi, lo = _split2(lk)
            lss.append(ls)
            hls.append(jnp.concatenate([hi, lo], axis=1))
            befores.append(before)
        css = [jnp.dot(hl, cw, preferred_element_type=F32) for hl in hls]
        abs_ = []
        for ls, cs, carry, before in zip(lss, css, carries, befores):
            a = jnp.exp2(ls + cs[:, :blk] + carry)
            if masked:
                a = jnp.where(before, a, 0.0)
            if skip is not None:
                a = jnp.where(skip, 0.0, a)
            ab = a.astype(BF16)
            abs_.append(jnp.concatenate([ab[:rq], ab[rq:]], axis=1))
        new_accs = []
        for ab, k0, v_scale, acc in zip(abs_, k0s, v_scales, accs):
            vst = jnp.concatenate([ve_ref[0, pl.ds(k0, blk), :], vo_ref[0, pl.ds(k0, blk), :]], axis=0)
            if v_scale is not None:
                vst = vst * v_scale
            new_accs.append(acc + jnp.dot(ab, vst, preferred_element_type=F32))
        new_carries = [carry + cs[:, blk:] for carry, cs in zip(carries, css)]
        return new_carries, new_accs

    def highest(carries):
        m = carries[0]
        for c in carries[1:]:
            m = jnp.maximum(m, c)
        return jnp.max(m)

    def group(gi, _):
        sis = [gi * chains + c for c in range(chains)]
        q0s = [pl.multiple_of(si * rq, rq) for si in sis]
        qss = []
        for q0 in q0s:
            q2 = q_ref[0, pl.ds(q0, rq), :].astype(F32)
            qss.append(jnp.concatenate([jnp.where(first, q2, 0.0), jnp.where(first, 0.0, q2)],
                                       axis=0).astype(BF16))
        carries = [jnp.zeros((rr, blk), F32)] * chains
        accs = [jnp.zeros((rq, LANES), F32)] * chains
        for i in range(qb):
            carries, accs = steps(qss, [si * qb + (qb - 1 - i) for si in sis], [None] * chains,
                                  carries, accs, True)

        def cond(st):
            n, top = st[0], st[1]
            return (n <= sis[-1] * qb) & (top > SB_LOG2_FLOOR)

        def key_blocks(n, last_chain_in_range):
            js, v_scales = [], []
            for c, si in enumerate(sis):
                d = si * qb - n
                if c < chains - 1 or not last_chain_in_range:
                    v_scales.append(jnp.where(d >= 0, 1.0, 0.0).astype(BF16))
                    d = jnp.maximum(d, 0)
                else:
                    v_scales.append(None)
                js.append(d)
            return js, v_scales

        def body(st):
            n = st[0]
            js, v_scales = key_blocks(n, True)
            skip = corner_region & (n == 2)
            cs_out, as_out = steps(qss, js, v_scales, list(st[2]), list(st[3]), False, skip)
            return n + 1, highest(cs_out), tuple(cs_out), tuple(as_out)

        js, v_scales = key_blocks(jnp.int32(1), False)
        carries, accs = steps(qss, js, v_scales, carries, accs, False)
        js, v_scales = key_blocks(jnp.int32(2), False)
        carries, accs = corner(qss, js, v_scales, carries, accs)
        st = lax.while_loop(cond, body, (jnp.int32(2), highest(carries), tuple(carries), tuple(accs)))
        for c, si in enumerate(sis):
            acc = st[3][c]
            q0 = pl.multiple_of(si * rq, rq)
            sq = acc * acc
            s0 = jnp.sum(jnp.where(first, sq, 0.0), axis=-1, keepdims=True)
            s1 = jnp.sum(jnp.where(first, 0.0, sq), axis=-1, keepdims=True)
            ms = jnp.where(first, s0, s1) * (1.0 / dh)
            o_ref[0, pl.ds(q0, rq), :] = (acc * lax.rsqrt(ms + EPS) * nw_ref[...]).astype(o_ref.dtype)
        return 0

    lax.fori_loop(0, n_groups, group, 0)


def _sb_attn(q, kt, v_even, v_odd, nw, *, chains, qb):
    b, s, w = q.shape
    assert s % (chains * qb * SB_BLOCK) == 0 and w % LANES == 0
    n_pairs = w // LANES
    cw = _sb_cumsum_weights()
    tok = pl.BlockSpec((1, s, LANES), lambda bi, hp: (bi, 0, hp))
    return pl.pallas_call(
        functools.partial(_sb_attn_kernel, n_groups=s // (chains * qb * SB_BLOCK), chains=chains, qb=qb),
        grid=(b, n_pairs),
        in_specs=[
            tok,
            pl.BlockSpec((1, LANES, s), lambda bi, hp: (bi, hp, 0)),
            tok, tok,
            pl.BlockSpec((1, LANES), lambda bi, hp: (0, hp)),
            pl.BlockSpec(cw.shape, lambda bi, hp: (0, 0)),
        ],
        out_specs=tok,
        out_shape=jax.ShapeDtypeStruct((b, s, w), BF16),
        compiler_params=pltpu.CompilerParams(
            dimension_semantics=("arbitrary", "arbitrary"), vmem_limit_bytes=VMEM_LIMIT),
        name="sb_attn",
    )(q, kt, v_even, v_odd, nw, cw)


def _hgrn_levels():
    hs = []
    h = HG_CHUNK // 2
    while h >= 1:
        hs.append(h)
        h //= 2
    return hs


def _hgrn_consts():
    n = HG_CHUNK
    t = np.arange(n)[:, None]
    j = np.arange(n)[None, :]
    mats = [j <= t]
    masks = []
    for h in _hgrn_levels():
        mid = (t // (2 * h)) * (2 * h) + h - 1
        upper = (t % (2 * h)) >= h
        if h > 1:
            mats.append(np.where(upper, (j > mid) & (j <= t), (j > t) & (j <= mid)))
        masks.append(((t // (2 * h)) == (j // (2 * h))) & upper & ((j % (2 * h)) < h))
    masks.append(t == j)
    sums = np.concatenate(mats, axis=0).astype(np.float32)
    sums = np.concatenate([sums, sums], axis=1)
    pm = np.concatenate(masks, axis=0).astype(np.float32)
    return jnp.asarray(sums, dtype=BF16), jnp.asarray(pm, dtype=F32)


def _hgrn2_kernel(q_ref, gl_ref, k_ref, v_ref, gate_ref, nw_ref, cs_ref, pm_ref, o_ref,
                  *, n_iters, chunks_per_iter):
    n = HG_CHUNK
    dk = HG_HEAD_DIM
    heads = 2
    levels = _hgrn_levels()
    rows = lax.broadcasted_iota(jnp.int32, (n, heads * dk), 0)
    odd = (rows & 1) != 0

    def body(it, states):
        base = it * (chunks_per_iter * n)
        cs = cs_ref[...]
        half = cs.shape[0] // 2
        units = []
        for c in range(chunks_per_iter):
            sl = pl.ds(pl.multiple_of(base + c * n, n), n)
            g = gl_ref[0, sl, :]
            g_parts = jnp.concatenate(_split2(g), axis=0)
            d = jnp.concatenate(
                [jnp.dot(cs[:half], g_parts, preferred_element_type=F32),
                 jnp.dot(cs[half:], g_parts, preferred_element_type=F32)], axis=0)
            units.append(dict(sl=sl, g=g, d=d, q=q_ref[0, sl, :], k=k_ref[0, sl, :], v=v_ref[0, sl, :]))
        for un in units:
            q, k, g, d = un["q"], un["k"], un["g"], un["d"]
            bc = d[0:n]
            b_last = bc[n - 1:n, :]
            ws = []
            for li, h in enumerate(levels):
                dl = d[(1 + li) * n:(2 + li) * n] if h > 1 else jnp.where(odd, g, 0.0)
                upper = (rows & h) != 0
                ws.append((jnp.where(upper, q, k) * jnp.exp(dl)).astype(BF16))
            un["ws"] = ws
            un["qk"] = q * k
            un["qd"] = (q * jnp.exp(bc)).astype(BF16)
            un["kd"] = (k * jnp.exp(b_last - bc)).astype(BF16)
            un["decay"] = jnp.exp(b_last)
        hslices = [slice(hh * dk, (hh + 1) * dk) for hh in range(heads)]
        for un in units:
            un["gram"] = [[lax.dot_general(w[:, ls], w[:, ls], _NT, preferred_element_type=F32)
                           for w in un["ws"]] for ls in hslices]
            un["kv"] = [lax.dot_general(un["v"][:, ls], un["kd"][:, ls], _TN, preferred_element_type=F32)
                        for ls in hslices]
        for un in units:
            un["st"] = states
            states = tuple(states[hh] * un["decay"][:, ls] + un["kv"][hh] for hh, ls in enumerate(hslices))
            ps = []
            for hh, ls in enumerate(hslices):
                p = pm_ref[len(levels) * n:(len(levels) + 1) * n, :] * jnp.sum(un["qk"][:, ls], axis=1, keepdims=True)
                for li in range(len(levels)):
                    p = p + pm_ref[li * n:(li + 1) * n, :] * un["gram"][hh][li]
                ps.append(p.astype(BF16))
            un["p"] = ps
        for un in units:
            un["o"] = [jnp.dot(un["p"][hh], un["v"][:, ls], preferred_element_type=F32)
                       + lax.dot_general(un["qd"][:, ls], un["st"][hh].astype(BF16), _NT,
                                         preferred_element_type=F32)
                       for hh, ls in enumerate(hslices)]
        for un in units:
            outs = []
            for o in un["o"]:
                ms = jnp.mean(o * o, axis=-1, keepdims=True)
                outs.append(o * lax.rsqrt(ms + EPS))
            o2 = jnp.concatenate(outs, axis=1)
            o_ref[0, un["sl"], :] = (o2 * nw_ref[...] * gate_ref[0, un["sl"], :]).astype(o_ref.dtype)
        return states

    zero = jnp.zeros((dk, dk), F32)
    lax.fori_loop(0, n_iters, body, (zero, zero))


def _hgrn2(q, gl, k, v, gate, nw, *, chunks_per_iter):
    b, s, w = q.shape
    assert s % (HG_CHUNK * chunks_per_iter) == 0 and w % (2 * HG_HEAD_DIM) == 0
    pair = 2 * HG_HEAD_DIM
    heads = w // pair
    cs, pm = _hgrn_consts()
    tok = pl.BlockSpec((1, s, pair), lambda bi, h: (bi, 0, h))
    return pl.pallas_call(
        functools.partial(_hgrn2_kernel, n_iters=s // (HG_CHUNK * chunks_per_iter),
                          chunks_per_iter=chunks_per_iter),
        grid=(b, heads),
        in_specs=[tok, tok, tok, tok, tok,
                  pl.BlockSpec((1, pair), lambda bi, h: (0, h)),
                  pl.BlockSpec(cs.shape, lambda bi, h: (0, 0)),
                  pl.BlockSpec(pm.shape, lambda bi, h: (0, 0))],
        out_specs=tok,
        out_shape=jax.ShapeDtypeStruct((b, s, w), BF16),
        compiler_params=pltpu.CompilerParams(
            dimension_semantics=("arbitrary", "arbitrary"), vmem_limit_bytes=VMEM_LIMIT),
        name="hgrn2",
    )(q, gl, k, v, gate, nw, cs, pm)


def _out_route_kernel(x_ref, sbo_ref, hgo_ref, wo_ref, nw_ref, wr_ref, br_ref, su_ref,
                      h1_ref, m_ref, meta_ref, gates_ref, counts_ref, carry_ref, *, sbw):
    i = pl.program_id(0)

    @pl.when(i == 0)
    def _():
        carry_ref[...] = jnp.zeros_like(carry_ref)

    h1 = (x_ref[...]
          + jnp.dot(sbo_ref[...], wo_ref[0:sbw, :], preferred_element_type=F32)
          + jnp.dot(hgo_ref[...], wo_ref[sbw:, :], preferred_element_type=F32))
    h1_ref[...] = h1
    ms = jnp.mean(h1 * h1, axis=-1, keepdims=True)
    m = h1 * lax.rsqrt(ms + EPS) * nw_ref[...]
    _store_row_tiles(m_ref, m)

    m2 = _split2(m)
    w2 = _split2(wr_ref[...])
    logits = br_ref[...]
    for wi, mi in ((0, 0), (0, 1), (1, 0)):
        logits = logits + lax.dot_general(w2[wi], m2[mi], _NT, preferred_element_type=F32)
    rid = lax.broadcasted_iota(jnp.int32, logits.shape, 0)
    neg = -jnp.inf
    big = jnp.int32(2 * LANES)

    def first_argmax(vals):
        vmax = jnp.max(vals, axis=0, keepdims=True)
        idx = jnp.min(jnp.where(vals == vmax, rid, big), axis=0, keepdims=True)
        return vmax, idx

    is_group = rid < N_GROUPS
    gmax, g_idx = first_argmax(jnp.where(is_group, logits, neg))
    gsum = jnp.sum(jnp.where(is_group, jnp.exp(logits - gmax), 0.0), axis=0, keepdims=True)
    g_prob = 1.0 / gsum
    lo_row = ROUTER_ROW0 + EXPERTS_PER_GROUP * g_idx
    el = jnp.where((rid >= lo_row) & (rid < lo_row + EXPERTS_PER_GROUP), logits, neg)
    v1, i1 = first_argmax(el)
    v2, i2 = first_argmax(jnp.where(rid == i1, neg, el))
    dd = jnp.exp(v2 - v1)
    p1 = 1.0 / (1.0 + dd)
    g1 = p1 * g_prob
    g2 = dd * p1 * g_prob

    hit1 = rid == i1
    hit2 = rid == i2
    onehot = jnp.where(hit1 | hit2, 1.0, 0.0)
    before_cnt = carry_ref[...] + jnp.dot(onehot.astype(BF16), su_ref[...], preferred_element_type=F32)
    r1 = jnp.sum(jnp.where(hit1, before_cnt, 0.0), axis=0, keepdims=True)
    r2 = jnp.sum(jnp.where(hit2, before_cnt, 0.0), axis=0, keepdims=True)
    carry_ref[...] = carry_ref[...] + jnp.sum(onehot, axis=1, keepdims=True)
    counts_ref[...] = carry_ref[...]

    meta_ref[...] = jnp.zeros_like(meta_ref)
    meta_ref[0:1, :] = i1 - ROUTER_ROW0
    meta_ref[1:2, :] = i2 - ROUTER_ROW0
    meta_ref[2:3, :] = r1.astype(jnp.int32)
    meta_ref[3:4, :] = r2.astype(jnp.int32)
    lane_rows = lax.broadcasted_iota(jnp.int32, (LANES, g1.shape[1]), 0)
    gates_ref[...] = jnp.where(lane_rows == 0, g1, jnp.where(lane_rows == 1, g2, 0.0)).T


def _out_route(x2, sbo, hgo, w_out, nw, wr_t, br, *, tm):
    n, d = x2.shape
    assert d == ROW_CHUNKS * LANES
    sbw = sbo.shape[1]
    j = np.arange(tm)[:, None]
    t = np.arange(tm)[None, :]
    su = jnp.asarray((j < t).astype(np.float32), dtype=BF16)
    tok = lambda width: pl.BlockSpec((tm, width), lambda i: (i, 0))
    const = lambda shape: pl.BlockSpec(shape, lambda i: (0,) * len(shape))
    return pl.pallas_call(
        functools.partial(_out_route_kernel, sbw=sbw),
        grid=(n // tm,),
        in_specs=[tok(d), tok(sbw), tok(hgo.shape[1]), const(w_out.shape), const((1, d)),
                  const(wr_t.shape), const(br.shape), const(su.shape)],
        out_specs=[tok(d), pl.BlockSpec((tm * ROW_CHUNKS, LANES), lambda i: (i, 0)),
                   pl.BlockSpec((8, tm), lambda i: (0, i)), tok(LANES), const((ROUTER_ROWS, 1))],
        out_shape=[jax.ShapeDtypeStruct((n, d), F32), jax.ShapeDtypeStruct((n * ROW_CHUNKS, LANES), F32),
                   jax.ShapeDtypeStruct((8, n), jnp.int32), jax.ShapeDtypeStruct((n, LANES), F32),
                   jax.ShapeDtypeStruct((ROUTER_ROWS, 1), F32)],
        scratch_shapes=[pltpu.VMEM((ROUTER_ROWS, 1), F32)],
        compiler_params=pltpu.CompilerParams(
            dimension_semantics=("arbitrary",), vmem_limit_bytes=VMEM_LIMIT),
        name="out_route",
    )(x2, sbo, hgo, w_out, nw, wr_t, br, su)


def _positions_kernel(seg_ref, meta_ref, pos_ref):
    e = meta_ref[0:TOP_K, :]
    start = jnp.zeros(e.shape, jnp.int32)
    for x in range(N_EXPERTS):
        start = jnp.where(e == x, seg_ref[x], start)
    pos_ref[...] = jnp.zeros_like(pos_ref)
    pos_ref[0:TOP_K, :] = start + meta_ref[TOP_K:2 * TOP_K, :]


def _positions(seg, meta):
    return pl.pallas_call(
        _positions_kernel,
        in_specs=[pl.BlockSpec(memory_space=pltpu.SMEM), pl.BlockSpec(memory_space=pltpu.VMEM)],
        out_specs=pl.BlockSpec(memory_space=pltpu.VMEM),
        out_shape=jax.ShapeDtypeStruct(meta.shape, jnp.int32),
        compiler_params=pltpu.CompilerParams(vmem_limit_bytes=VMEM_LIMIT),
        name="positions",
    )(seg, meta)


def _dispatch_kernel(seg_ref, pos_ref, m_ref, xs_ref, zbuf, sem, zsem, *, tm, tmm, n_tiles):
    i = pl.program_id(0)

    rc = ROW_CHUNKS

    def zero_copy(e):
        tail = pl.multiple_of((seg_ref[N_EXPERTS + e] - tmm) * rc, tmm * rc)
        return pltpu.make_async_copy(zbuf, xs_ref.at[pl.ds(tail, tmm * rc), :], zsem)

    @pl.when(i == 0)
    def _():
        zbuf[...] = jnp.zeros_like(zbuf)
        for e in range(N_EXPERTS):
            @pl.when(seg_ref[2 * N_EXPERTS + e] > 0)
            def _():
                zero_copy(e).start()
        for e in range(N_EXPERTS):
            @pl.when(seg_ref[2 * N_EXPERTS + e] > 0)
            def _():
                zero_copy(e).wait()

        def unused_copy(t):
            first = pl.multiple_of(t * (tmm * rc), tmm * rc)
            return pltpu.make_async_copy(zbuf, xs_ref.at[pl.ds(first, tmm * rc), :], zsem)

        def start_unused(t, _):
            unused_copy(t).start()
            return 0

        def wait_unused(t, _):
            unused_copy(t).wait()
            return 0

        lax.fori_loop(seg_ref[3 * N_EXPERTS], n_tiles, start_unused, 0)
        lax.fori_loop(seg_ref[3 * N_EXPERTS], n_tiles, wait_unused, 0)

    for r in range(tm):
        src = m_ref.at[pl.ds(r * rc, rc), :]
        for k in range(TOP_K):
            dst = pl.multiple_of(pos_ref[k, r] * rc, rc)
            pltpu.make_async_copy(src, xs_ref.at[pl.ds(dst, rc), :], sem).start(priority=k)
    for _ in range(TOP_K):
        pltpu.make_async_copy(m_ref, xs_ref.at[pl.ds(0, tm * rc), :], sem).wait()


def _dispatch(seg, pos, m, n_rows, *, tm, tmm):
    n = m.shape[0] // ROW_CHUNKS
    return pl.pallas_call(
        functools.partial(_dispatch_kernel, tm=tm, tmm=tmm, n_tiles=n_rows // tmm),
        grid=(n // tm,),
        in_specs=[pl.BlockSpec(memory_space=pltpu.SMEM),
                  pl.BlockSpec((8, tm), lambda i: (0, i), memory_space=pltpu.SMEM),
                  pl.BlockSpec((tm * ROW_CHUNKS, LANES), lambda i: (i, 0))],
        out_specs=pl.BlockSpec(memory_space=pl.ANY),
        out_shape=jax.ShapeDtypeStruct((n_rows * ROW_CHUNKS, LANES), F32),
        scratch_shapes=[pltpu.VMEM((tmm * ROW_CHUNKS, LANES), F32), pltpu.SemaphoreType.DMA(()),
                        pltpu.SemaphoreType.DMA(())],
        compiler_params=pltpu.CompilerParams(
            dimension_semantics=("arbitrary",), vmem_limit_bytes=VMEM_LIMIT),
        name="dispatch",
    )(seg, pos, m)


def _experts_kernel(te_ref, ts_ref, first_ref, par_ref, nxt_ref, xs_ref, wg_hbm, wu_hbm, wd_hbm, y_ref,
                    wg_f, wu_f, wd_f, wg_b, wu_b, wd_b, sem, *, tmm):
    t = pl.program_id(0)

    def copies(e, slot):
        return (pltpu.make_async_copy(wg_hbm.at[e], wg_f.at[slot], sem.at[slot, 0]),
                pltpu.make_async_copy(wu_hbm.at[e], wu_f.at[slot], sem.at[slot, 1]),
                pltpu.make_async_copy(wd_hbm.at[e], wd_f.at[slot], sem.at[slot, 2]))

    @pl.when(t == 0)
    def _():
        for c in copies(te_ref[0], 0):
            c.start()

    for parity in range(2):
        @pl.when((first_ref[t] == 1) & (par_ref[t] == parity))
        def _():
            for c in copies(0, parity):
                c.wait()

            @pl.when(nxt_ref[t] >= 0)
            def _():
                for c in copies(nxt_ref[t], 1 - parity):
                    c.start()

            wg_b[...] = wg_f[parity].astype(BF16)
            wu_b[...] = wu_f[parity].astype(BF16)
            wd_b[...] = wd_f[parity].astype(BF16)

    @pl.when(ts_ref[t] == t)
    def _():
        x = _load_row_tiles(xs_ref, tmm).astype(BF16)
        hg = jnp.dot(x, wg_b[...], preferred_element_type=F32)
        hu = jnp.dot(x, wu_b[...], preferred_element_type=F32)
        act = (hg * _sigmoid(hg) * hu).astype(BF16)
        _store_row_tiles(y_ref, jnp.dot(act, wd_b[...], preferred_element_type=F32))

    @pl.when(ts_ref[t] != t)
    def _():
        y_ref[...] = jnp.zeros_like(y_ref)


def _experts(tile_expert, tile_src, tile_first, tile_parity, tile_next, xs, wg, wu, wd, *, tmm):
    n_rows = xs.shape[0] // ROW_CHUNKS
    d, de = wg.shape[1], wg.shape[2]
    assert d == ROW_CHUNKS * LANES
    grid_spec = pltpu.PrefetchScalarGridSpec(
        num_scalar_prefetch=5,
        grid=(n_rows // tmm,),
        in_specs=[pl.BlockSpec((tmm * ROW_CHUNKS, LANES), lambda t, te, ts, *_: (ts[t], 0)),
                  pl.BlockSpec(memory_space=pl.ANY),
                  pl.BlockSpec(memory_space=pl.ANY),
                  pl.BlockSpec(memory_space=pl.ANY)],
        out_specs=pl.BlockSpec((tmm * ROW_CHUNKS, LANES), lambda t, *_: (t, 0)),
        scratch_shapes=[pltpu.VMEM((2, d, de), F32), pltpu.VMEM((2, d, de), F32), pltpu.VMEM((2, de, d), F32),
                        pltpu.VMEM((d, de), BF16), pltpu.VMEM((d, de), BF16), pltpu.VMEM((de, d), BF16),
                        pltpu.SemaphoreType.DMA((2, 3))],
    )
    return pl.pallas_call(
        functools.partial(_experts_kernel, tmm=tmm),
        grid_spec=grid_spec,
        out_shape=jax.ShapeDtypeStruct((n_rows * ROW_CHUNKS, LANES), F32),
        compiler_params=pltpu.CompilerParams(
            dimension_semantics=("arbitrary",), vmem_limit_bytes=VMEM_LIMIT),
        name="experts",
    )(tile_expert, tile_src, tile_first, tile_parity, tile_next, xs, wg, wu, wd)


def _combine_kernel(pos_ref, posn_ref, h1_ref, gates_ref, p_ref, pnw_ref, wpp_ref, wpg_ref, fnw_ref,
                    y_ref, o_ref, ybuf_even, ybuf_odd, sem, *, tm, n_steps):
    i = pl.program_id(0)

    rc = ROW_CHUNKS
    ybufs = (ybuf_even, ybuf_odd)

    def gather(p_ref_, slot):
        for r in range(tm):
            for k in range(TOP_K):
                src = pl.multiple_of(p_ref_[k, r] * rc, rc)
                pltpu.make_async_copy(y_ref.at[pl.ds(src, rc), :],
                                      ybufs[slot].at[k, pl.ds(r * rc, rc), :], sem.at[slot]).start(priority=k)

    def wait(slot):
        for k in range(TOP_K):
            pltpu.make_async_copy(y_ref.at[pl.ds(0, tm * rc), :], ybufs[slot].at[k], sem.at[slot]).wait()

    @pl.when(i == 0)
    def _():
        gather(pos_ref, 0)

    def step(slot):
        wait(slot)
        gather(posn_ref, 1 - slot)
        gts = gates_ref[...]
        h2 = (h1_ref[...] + gts[:, 0:1] * _load_row_tiles(ybufs[slot].at[0], tm)
              + gts[:, 1:2] * _load_row_tiles(ybufs[slot].at[1], tm))
        e = jnp.dot(p_ref[...].astype(BF16), wpp_ref[...], preferred_element_type=F32)
        ms = jnp.mean(h2 * h2, axis=-1, keepdims=True)
        hn = (h2 * lax.rsqrt(ms + EPS) * pnw_ref[...]).astype(BF16)
        gate = _sigmoid(jnp.dot(hn, wpg_ref[...], preferred_element_type=F32))
        h3 = h2 + gate * e
        ms3 = jnp.mean(h3 * h3, axis=-1, keepdims=True)
        o_ref[...] = h3 * lax.rsqrt(ms3 + EPS) * fnw_ref[...]

        @pl.when(i == n_steps - 1)
        def _():
            wait(1 - slot)

    for parity in range(2):
        @pl.when(i % 2 == parity)
        def _():
            step(parity)


def _combine(pos, h1, gates, p2, pnw, wpp, wpg, fnw, y, *, tm):
    n, d = h1.shape
    n_steps = n // tm
    tok = lambda width: pl.BlockSpec((tm, width), lambda i: (i, 0))
    const = lambda shape: pl.BlockSpec(shape, lambda i: (0,) * len(shape))
    return pl.pallas_call(
        functools.partial(_combine_kernel, tm=tm, n_steps=n_steps),
        grid=(n_steps,),
        in_specs=[pl.BlockSpec((8, tm), lambda i: (0, i), memory_space=pltpu.SMEM),
                  pl.BlockSpec((8, tm), lambda i: (0, jnp.minimum(i + 1, n_steps - 1)),
                               memory_space=pltpu.SMEM),
                  tok(d), tok(LANES), tok(p2.shape[1]), const((1, d)), const(wpp.shape),
                  const(wpg.shape), const((1, d)),
                  pl.BlockSpec(memory_space=pl.ANY)],
        out_specs=tok(d),
        out_shape=jax.ShapeDtypeStruct((n, d), F32),
        scratch_shapes=[pltpu.VMEM((TOP_K, tm * ROW_CHUNKS, LANES), F32),
                        pltpu.VMEM((TOP_K, tm * ROW_CHUNKS, LANES), F32), pltpu.SemaphoreType.DMA((2,))],
        compiler_params=pltpu.CompilerParams(
            dimension_semantics=("arbitrary",), vmem_limit_bytes=VMEM_LIMIT),
        name="combine",
    )(pos, pos, h1, gates, p2, pnw, wpp, wpg, fnw, y)


def kernel(x, p, attn_norm_w, w_in, sb_norm_w, hg_lower_bounds, hg_norm_w, w_out, ffn_norm_w,
           w_group_router, b_group_router, w_expert_router, b_expert_router, w_exp_gate, w_exp_up,
           w_exp_down, ple_norm_w, w_ple_proj, w_ple_gate, final_norm_w):
    b, s, d = x.shape
    depth = w_in.shape[0]
    assert depth == 1, "single-layer trunk"
    sbw = sb_norm_w.shape[1]
    hgw = hg_norm_w.shape[1]
    n = b * s
    tm_proj = min(TOKEN_TILE, s)
    tm_route = min(ROUTE_TILE, n)
    tm_disp = min(TOKEN_TILE, n)
    tm_comb = min(COMBINE_TILE, n)
    tmm = min(EXPERT_TILE, n)

    wi = w_in[0]
    w_main = jnp.concatenate([wi[:, 0:sbw], wi[:, 2 * sbw:]], axis=1).astype(BF16)
    w_kt = wi[:, sbw:2 * sbw].T.astype(BF16)
    gap = ROUTER_ROW0 - N_GROUPS
    tail = ROUTER_ROWS - ROUTER_ROW0 - N_EXPERTS
    wr_t = jnp.concatenate([w_group_router[0].T, jnp.zeros((gap, d), F32), w_expert_router[0].T,
                            jnp.zeros((tail, d), F32)], axis=0)
    br = jnp.concatenate([b_group_router[0], jnp.zeros((gap,), F32), b_expert_router[0],
                          jnp.zeros((tail,), F32)])[:, None]

    sbq, sbkt, sbve, sbvo, hq, hgl, hk, hv, hgate = _in_proj(
        x, attn_norm_w[0][None, :], w_main, w_kt, hg_lower_bounds, tm=tm_proj)
    sbo = _sb_attn(sbq, sbkt, sbve, sbvo, sb_norm_w, chains=SB_CHAINS, qb=SB_QUERY_BLOCKS)
    hgo = _hgrn2(hq, hgl, hk, hv, hgate, hg_norm_w, chunks_per_iter=HG_CHUNKS_PER_ITER)

    h1, m, meta, gates, counts = _out_route(
        x.reshape(n, d), sbo.reshape(n, sbw), hgo.reshape(n, hgw), w_out[0].astype(BF16),
        ffn_norm_w[0][None, :], wr_t, br, tm=tm_route)

    cnt = counts[ROUTER_ROW0:ROUTER_ROW0 + N_EXPERTS, 0].astype(jnp.int32)
    padded = ((cnt + tmm - 1) // tmm) * tmm
    ends = jnp.cumsum(padded)
    starts = ends - padded
    n_rows = n * TOP_K + N_EXPERTS * tmm
    n_tiles = n_rows // tmm
    last_tile = ends[-1] // tmm - 1
    tile_src = jnp.minimum(jnp.arange(n_tiles, dtype=jnp.int32), last_tile)
    tile_expert = jnp.sum((ends[None, :] <= (tile_src * tmm)[:, None]).astype(jnp.int32), axis=1)
    seg = jnp.concatenate([starts, ends, padded, (last_tile + 1)[None],
                           jnp.zeros((LANES - 3 * N_EXPERTS - 1,), jnp.int32)])

    pos = _positions(seg, meta)
    xs = _dispatch(seg, pos, m, n_rows, tm=tm_disp, tmm=tmm)
    tiles = jnp.arange(n_tiles, dtype=jnp.int32)
    changed = jnp.concatenate([jnp.ones((1,), bool), tile_expert[1:] != tile_expert[:-1]])
    tile_first = ((tile_src == tiles) & changed).astype(jnp.int32)
    tile_parity = (jnp.cumsum(tile_first) - 1) % 2
    experts = jnp.arange(N_EXPERTS, dtype=jnp.int32)
    later = (experts[None, :] > tile_expert[:, None]) & (padded[None, :] > 0)
    tile_next = jnp.min(jnp.where(later, experts[None, :], N_EXPERTS), axis=1)
    tile_next = jnp.where(tile_next == N_EXPERTS, -1, tile_next)
    y = _experts(tile_expert, tile_src, tile_first, tile_parity.astype(jnp.int32), tile_next.astype(jnp.int32),
                 xs, w_exp_gate[0], w_exp_up[0], w_exp_down[0], tmm=tmm)
    out = _combine(pos, h1, gates, p[0].reshape(n, -1), ple_norm_w[0][None, :],
                   w_ple_proj[0].astype(BF16), w_ple_gate[0].astype(BF16), final_norm_w[None, :],
                   y, tm=tm_comb)
    return out.reshape(b, s, d)
```
